```python
import jax, jax.numpy as jnp
from jax import lax
import numpy as np

D_MODEL = 1024
BATCH = 4
SEQ = 8192
DEPTH = 1
DEC_BATCH = 32
DEC_SEQ = 4
PAST_LEN = 16384
PAGE_SIZE = 128

H_A = 8
DH_A = 64
W_A = H_A * DH_A
Q_BLOCK = 128
H_B = 8
DH_B = 64
W_B = H_B * DH_B
DECAY_LORA = 64
AAA_LORA = 64
GATE_LORA = 128
GN_EPS = 64e-5
N_GROUPS = 4
EXPERTS_PER_GROUP = 8
N_EXPERTS = N_GROUPS * EXPERTS_PER_GROUP
TOP_K_IN_GROUP = 2
D_EXPERT = 256
ROUTE_BLOCK = 256
NORM_EPS = 1e-6

FOX_COLS = 3 * W_A + H_A
RW_COLS = 3 * W_B + DECAY_LORA + AAA_LORA + GATE_LORA
GATE_COLS = 2 * D_MODEL
IN_COLS = FOX_COLS + RW_COLS + GATE_COLS
FOX_SPLITS = (W_A, 2 * W_A, 3 * W_A)
RW_SPLITS = (W_B, 2 * W_B, 3 * W_B, 3 * W_B + DECAY_LORA, 3 * W_B + DECAY_LORA + AAA_LORA)

kernel_name = "fox_rwkv7_hier_moe_step"


def rmsnorm(x, g):
    x32 = x.astype(jnp.float32)
    y = x32 * lax.rsqrt(jnp.mean(x32 * x32, axis=-1, keepdims=True) + NORM_EPS)
    return (y * g).astype(x.dtype)


def fox_attend(q, k, v, c_q, c_k, q_pos, k_pos):
    s = jnp.einsum('bqhd,bkhd->bhqk', q, k).astype(jnp.float32) * (DH_A ** -0.5)
    s = s + jnp.transpose(c_q, (0, 2, 1))[..., :, None] - jnp.transpose(c_k, (0, 2, 1))[..., None, :]
    s = jnp.where(k_pos[None, :] <= q_pos[:, None], s, -jnp.inf)
    p = jax.nn.softmax(s, axis=-1)
    return jnp.einsum('bhqk,bkhd->bqhd', p.astype(v.dtype), v)


def fox_prompt(q, k, v, logf):
    B, T = q.shape[:2]
    nb = T // Q_BLOCK
    c = jnp.cumsum(logf, axis=1)
    k_pos = jnp.arange(T)
    qb = jnp.swapaxes(q.reshape(B, nb, Q_BLOCK, H_A, DH_A), 0, 1)
    cb = jnp.swapaxes(c.reshape(B, nb, Q_BLOCK, H_A), 0, 1)
    pb = k_pos.reshape(nb, Q_BLOCK)
    ob = lax.map(lambda a: fox_attend(a[0], k, v, a[1], c, a[2], k_pos), (qb, cb, pb))
    return jnp.swapaxes(ob, 0, 1).reshape(B, T, H_A, DH_A)


def fox_sample(q, k, v, logf, k_past, v_past, logf_past):
    past_len, T = k_past.shape[1], q.shape[1]
    k_all = jnp.concatenate([k_past.astype(k.dtype), k], axis=1)
    v_all = jnp.concatenate([v_past.astype(v.dtype), v], axis=1)
    c_all = jnp.cumsum(jnp.concatenate([logf_past.astype(jnp.float32), logf], axis=1), axis=1)
    q_pos = past_len + jnp.arange(T)
    k_pos = jnp.arange(past_len + T)
    return fox_attend(q, k_all, v_all, c_all[:, past_len:], c_all, q_pos, k_pos)


def gather_pages(pool, page_table):
    g = pool[page_table]
    return g.reshape(g.shape[0], g.shape[1] * g.shape[2], *g.shape[3:])


def _wkv_step(S, inp):
    r, d, k, v, kk, a = inp
    s_kk = jnp.einsum('bhvk,bhk->bhv', S, kk)
    S = S * d[:, :, None, :] - s_kk[..., None] * (kk * a)[:, :, None, :] + v[..., None] * k[:, :, None, :]
    return S, jnp.einsum('bhvk,bhk->bhv', S, r)


def rwkv_branch(u, shift_prev, wkv_prev, mu_rw, w0, w2, a0, a2, g2, k_k, k_a, r_k, lnx_w, lnx_b):
    B, T, _ = u.shape
    f32 = jnp.float32
    u_prev = jnp.concatenate([shift_prev[:, None, :].astype(u.dtype), u[:, :-1]], axis=1)
    us = u + mu_rw * (u_prev - u)
    r, k, v, wl, al, gl = jnp.split(us, RW_SPLITS, axis=-1)
    w = -jax.nn.softplus(-(w0 + jnp.tanh(wl) @ w2).astype(f32)) - 0.5
    decay = jnp.exp(-jnp.exp(w))
    a = jax.nn.sigmoid((a0 + al @ a2).astype(f32))
    g = jax.nn.sigmoid(gl) @ g2
    heads = lambda t: t.astype(f32).reshape(B, T, H_B, DH_B)
    kk = heads(k * k_k)
    kk = kk / jnp.maximum(jnp.sqrt(jnp.sum(kk * kk, axis=-1, keepdims=True)), 1e-12)
    k_mod = k.astype(f32) * (1.0 + (a - 1.0) * k_a)
    r_h, k_h, v_h, a_h, d_h = heads(r), heads(k_mod), heads(v), heads(a), heads(decay)
    xs = tuple(jnp.moveaxis(t, 1, 0) for t in (r_h, d_h, k_h, v_h, kk, a_h))
    wkv_new, o = lax.scan(_wkv_step, wkv_prev.astype(f32), xs)
    o = jnp.moveaxis(o, 0, 1)
    mean = jnp.mean(o, axis=-1, keepdims=True)
    var = jnp.mean(jnp.square(o - mean), axis=-1, keepdims=True)
    o = ((o - mean) * lax.rsqrt(var + GN_EPS)).reshape(B, T, W_B) * lnx_w + lnx_b
    bonus = (jnp.sum(r_h * k_h * r_k.reshape(H_B, DH_B), axis=-1, keepdims=True) * v_h).reshape(B, T, W_B)
    o = ((o + bonus) * g).astype(u.dtype)
    return o, u[:, -1], wkv_new


def hier_moe(h, w_grp, w_exp, we_gate, we_up, we_down):
    B, T, D = h.shape
    hf = h.reshape(-1, D)
    n = hf.shape[0]
    grp_logits = (hf @ w_grp).astype(jnp.float32)
    grp_prob = jax.nn.softmax(grp_logits, axis=-1)
    g_idx = jnp.argmax(grp_logits, axis=-1).astype(jnp.int32)
    p_grp = jnp.take_along_axis(grp_prob, g_idx[:, None], axis=-1)
    exp_logits = (hf @ w_exp).astype(jnp.float32).reshape(n, N_GROUPS, EXPERTS_PER_GROUP)
    in_grp = jnp.take_along_axis(exp_logits, g_idx[:, None, None], axis=1)[:, 0]
    top_v, top_i = lax.top_k(in_grp, TOP_K_IN_GROUP)
    weights = jax.nn.softmax(top_v, axis=-1) * p_grp
    eid = (g_idx[:, None] * EXPERTS_PER_GROUP + top_i).reshape(-1).astype(jnp.int32)
    tok = jnp.repeat(jnp.arange(n, dtype=jnp.int32), TOP_K_IN_GROUP)
    wts = weights.reshape(-1)
    n_assign = eid.shape[0]
    n_blocks = -(-n_assign // ROUTE_BLOCK) + N_EXPERTS
    rows = n_blocks * ROUTE_BLOCK
    counts = jnp.zeros((N_EXPERTS,), jnp.int32).at[eid].add(1)
    padded = (counts + ROUTE_BLOCK - 1) // ROUTE_BLOCK * ROUTE_BLOCK
    pad_end = jnp.cumsum(padded).astype(jnp.int32)
    pad_start = pad_end - padded
    start = jnp.cumsum(counts).astype(jnp.int32) - counts
    order = jnp.argsort(eid)
    e_sorted = eid[order]
    dest = pad_start[e_sorted] + jnp.arange(n_assign, dtype=jnp.int32) - start[e_sorted]
    row_tok = jnp.full((rows,), n, jnp.int32).at[dest].set(tok[order])
    row_w = jnp.zeros((rows,), jnp.float32).at[dest].set(wts[order])
    blk_start = jnp.arange(n_blocks, dtype=jnp.int32) * ROUTE_BLOCK
    blk_exp = jnp.minimum(jnp.searchsorted(pad_end, blk_start, side='right'), N_EXPERTS - 1)
    h_pad = jnp.concatenate([hf, jnp.zeros((1, D), hf.dtype)], axis=0)

    def run_block(args):
        tok_b, w_b, e = args
        xb = h_pad[tok_b]
        yb = (jax.nn.silu(xb @ we_gate[e]) * (xb @ we_up[e])) @ we_down[e]
        return yb * w_b[:, None].astype(yb.dtype)

    ys = lax.map(run_block, (row_tok.reshape(n_blocks, ROUTE_BLOCK), row_w.reshape(n_blocks, ROUTE_BLOCK), blk_exp))
    out = jnp.zeros((n + 1, D), h.dtype).at[row_tok].add(ys.reshape(rows, D).astype(h.dtype))
    return out[:n].reshape(B, T, D)


def hybrid_layer(x, past, shift_prev, wkv_prev, norm_mix, w_in, b_f, mu_rw, w0, w2, a0, a2, g2,
                 k_k, k_a, r_k, lnx_w, lnx_b, p_a, p_b, w_o, norm_ffn, w_grp, w_exp, we_gate, we_up, we_down):
    B, T, _ = x.shape
    xn = rmsnorm(x, norm_mix)
    u = xn @ w_in
    u_fox, u_rw, u_gate = jnp.split(u, (FOX_COLS, FOX_COLS + RW_COLS), axis=-1)
    q, k, v, fl = jnp.split(u_fox, FOX_SPLITS, axis=-1)
    q = q.reshape(B, T, H_A, DH_A)
    k = k.reshape(B, T, H_A, DH_A)
    v = v.reshape(B, T, H_A, DH_A)
    logf = jax.nn.log_sigmoid((fl + b_f).astype(jnp.float32))
    if past is None:
        o_a = fox_prompt(q, k, v, logf)
    else:
        o_a = fox_sample(q, k, v, logf, *past)
    o_b, shift_new, wkv_new = rwkv_branch(u_rw, shift_prev, wkv_prev, mu_rw, w0, w2, a0, a2, g2,
                                          k_k, k_a, r_k, lnx_w, lnx_b)
    g_a, g_b = jnp.split(jax.nn.sigmoid(u_gate), 2, axis=-1)
    mixed = (g_a * (o_a.reshape(B, T, W_A) @ p_a) + g_b * (o_b @ p_b)) @ w_o
    x = x + mixed
    x = x + hier_moe(rmsnorm(x, norm_ffn), w_grp, w_exp, we_gate, we_up, we_down)
    return x, (k, v, logf, wkv_new, shift_new)


def setup_inputs(seed: int = 0) -> dict:
    key = jax.random.key(seed)
    ks = iter(jax.random.split(key, 40))
    f32 = jnp.float32

    def nrm(shape, scale):
        return jax.random.normal(next(ks), shape, f32) * scale

    n_pages = PAST_LEN // PAGE_SIZE
    used = DEC_BATCH * n_pages
    n_phys = used + max(1, used // 4)
    L = DEPTH
    inp = {}
    inp['x_prompt'] = nrm((BATCH, SEQ, D_MODEL), 1.0)
    inp['x_sample'] = nrm((DEC_BATCH, DEC_SEQ, D_MODEL), 1.0)
    inp['cache_k'] = nrm((L, n_phys, PAGE_SIZE, H_A, DH_A), 1.0)
    inp['cache_v'] = nrm((L, n_phys, PAGE_SIZE, H_A, DH_A), 1.0)
    inp['cache_logf'] = jax.nn.log_sigmoid(3.0 + nrm((L, n_phys, PAGE_SIZE, H_A), 1.0))
    inp['page_table'] = jax.random.permutation(next(ks), n_phys)[:used].reshape(DEC_BATCH, n_pages).astype(jnp.int32)
    inp['state_wkv'] = nrm((L, DEC_BATCH, H_B, DH_B, DH_B), 0.5)
    inp['state_shift'] = nrm((L, DEC_BATCH, RW_COLS), 1.0)
    inp['norm_mix'] = 1.0 + nrm((L, D_MODEL), 0.02)
    inp['w_in'] = nrm((L, D_MODEL, IN_COLS), D_MODEL ** -0.5)
    inp['b_f'] = 3.0 + nrm((L, H_A), 0.1)
    inp['mu_rw'] = jax.random.uniform(next(ks), (L, RW_COLS), f32)
    inp['w0'] = nrm((L, W_B), 0.5)
    inp['w2'] = nrm((L, DECAY_LORA, W_B), 0.1)
    inp['a0'] = nrm((L, W_B), 0.1)
    inp['a2'] = nrm((L, AAA_LORA, W_B), AAA_LORA ** -0.5)
    inp['g2'] = nrm((L, GATE_LORA, W_B), 0.1)
    inp['k_k'] = 0.85 + nrm((L, W_B), 0.02)
    inp['k_a'] = 1.0 + nrm((L, W_B), 0.02)
    inp['r_k'] = nrm((L, H_B, DH_B), 0.1)
    inp['lnx_w'] = 1.0 + nrm((L, W_B), 0.02)
    inp['lnx_b'] = nrm((L, W_B), 0.02)
    inp['p_a'] = nrm((L, W_A, D_MODEL), W_A ** -0.5)
    inp['p_b'] = nrm((L, W_B, D_MODEL), W_B ** -0.5)
    inp['w_o'] = nrm((L, D_MODEL, D_MODEL), D_MODEL ** -0.5)
    inp['norm_ffn'] = 1.0 + nrm((L, D_MODEL), 0.02)
    inp['w_grp'] = nrm((L, D_MODEL, N_GROUPS), D_MODEL ** -0.5)
    inp['w_exp'] = nrm((L, D_MODEL, N_EXPERTS), D_MODEL ** -0.5)
    inp['we_gate'] = nrm((L, N_EXPERTS, D_MODEL, D_EXPERT), D_MODEL ** -0.5)
    inp['we_up'] = nrm((L, N_EXPERTS, D_MODEL, D_EXPERT), D_MODEL ** -0.5)
    inp['we_down'] = nrm((L, N_EXPERTS, D_EXPERT, D_MODEL), D_EXPERT ** -0.5)
    inp['norm_final'] = 1.0 + nrm((D_MODEL,), 0.02)
    return inp


def reference(x_prompt, x_sample, cache_k, cache_v, cache_logf, page_table, state_wkv, state_shift,
              norm_mix, w_in, b_f, mu_rw, w0, w2, a0, a2, g2, k_k, k_a, r_k, lnx_w, lnx_b,
              p_a, p_b, w_o, norm_ffn, w_grp, w_exp, we_gate, we_up, we_down, norm_final):
    xp, xs = x_prompt, x_sample
    new_p, new_s = [], []
    for l in range(DEPTH):
        lw = (norm_mix[l], w_in[l], b_f[l], mu_rw[l], w0[l], w2[l], a0[l], a2[l], g2[l], k_k[l], k_a[l],
              r_k[l], lnx_w[l], lnx_b[l], p_a[l], p_b[l], w_o[l], norm_ffn[l], w_grp[l], w_exp[l],
              we_gate[l], we_up[l], we_down[l])
        shift0 = jnp.zeros((xp.shape[0], RW_COLS), xp.dtype)
        wkv0 = jnp.zeros((xp.shape[0], H_B, DH_B, DH_B), jnp.float32)
        xp, st_p = hybrid_layer(xp, None, shift0, wkv0, *lw)
        past = (gather_pages(cache_k[l], page_table), gather_pages(cache_v[l], page_table),
                gather_pages(cache_logf[l], page_table))
        xs, st_s = hybrid_layer(xs, past, state_shift[l], state_wkv[l], *lw)
        new_p.append(st_p)
        new_s.append(st_s)
    y_prompt = rmsnorm(xp, norm_final)
    y_sample = rmsnorm(xs, norm_final)
    k_prompt = jnp.stack([s[0] for s in new_p])
    v_prompt = jnp.stack([s[1] for s in new_p])
    logf_prompt = jnp.stack([s[2] for s in new_p])
    wkv_prompt = jnp.stack([s[3] for s in new_p])
    shift_prompt = jnp.stack([s[4] for s in new_p])
    k_sample = jnp.stack([s[0] for s in new_s])
    v_sample = jnp.stack([s[1] for s in new_s])
    logf_sample = jnp.stack([s[2] for s in new_s])
    wkv_sample = jnp.stack([s[3] for s in new_s])
    shift_sample = jnp.stack([s[4] for s in new_s])
    return (y_prompt, y_sample, k_prompt, v_prompt, logf_prompt, wkv_prompt, shift_prompt,
            k_sample, v_sample, logf_sample, wkv_sample, shift_sample)
```

```python
import functools

import jax
import jax.numpy as jnp
from jax import lax
from jax.experimental import pallas as pl
from jax.experimental.pallas import tpu as pltpu

F32 = jnp.float32
BF16 = jnp.bfloat16

D_MODEL = 1024
N_HEADS = 8
D_HEAD = 64
W_MIX = N_HEADS * D_HEAD
LORA_W = 64
LORA_G = 128
RW_COLS = 3 * W_MIX + 2 * LORA_W + LORA_G
RW_PAD = 3 * W_MIX + 3 * 128
FOX_COLS = 3 * W_MIX + N_HEADS
N_EXPERTS = 32
EXPERTS_PER_GROUP = 8
N_GROUPS = 4
D_EXPERT = 256
PAGE = 128
NORM_EPS = 1e-6
GN_EPS = 64e-5
QK_SCALE = D_HEAD ** -0.5
LANES = 128
CHUNK = 64
NEG = -1e30
VMEM_LIMIT = 56 * 1024 * 1024


def _dot(a, b):
    return jnp.dot(a.astype(BF16), b.astype(BF16), preferred_element_type=F32)


def _dot_nt(a, b):
    return lax.dot_general(a.astype(BF16), b.astype(BF16), (((1,), (1,)), ((), ())),
                           preferred_element_type=F32)


def _split3(x):
    hi = x.astype(BF16)
    r1 = x - hi.astype(F32)
    mid = r1.astype(BF16)
    lo = (r1 - mid.astype(F32)).astype(BF16)
    return hi, mid, lo


def _dot_sel_lhs(sel, x):
    hi, mid, lo = _split3(x)
    d = lambda p: jnp.dot(sel, p, preferred_element_type=F32)
    return d(hi) + d(mid) + d(lo)


def _dot_sel_rhs(x, sel):
    hi, mid, lo = _split3(x)
    d = lambda p: jnp.dot(p, sel, preferred_element_type=F32)
    return d(hi) + d(mid) + d(lo)


def _dot_nt_2(a, s):
    hi = s.astype(BF16)
    lo = (s - hi.astype(F32)).astype(BF16)
    ab = a.astype(BF16)
    dn = (((1,), (1,)), ((), ()))
    return (lax.dot_general(ab, hi, dn, preferred_element_type=F32)
            + lax.dot_general(ab, lo, dn, preferred_element_type=F32))


def _sigmoid(x):
    return 1.0 / (1.0 + jnp.exp(-x))


def _log_sigmoid(x):
    return jnp.minimum(x, 0.0) - jnp.log(1.0 + jnp.exp(-jnp.abs(x)))


def _rms(x, g):
    return (x * lax.rsqrt(jnp.mean(x * x, axis=-1, keepdims=True) + NORM_EPS)) * g


def _params(*sem):
    return pltpu.CompilerParams(dimension_semantics=sem, vmem_limit_bytes=VMEM_LIMIT)


def _inproj_kernel(x_ref, g_ref, w_ref, bf_ref, q_ref, k_ref, v_ref, kb_ref, vb_ref, lf_ref, rw_ref):
    xb = _rms(x_ref[...], g_ref[...]).astype(BF16)
    d = lambda lo, hi: jnp.dot(xb, w_ref[:, lo:hi], preferred_element_type=F32)
    q_ref[...] = (d(0, W_MIX) * QK_SCALE).astype(BF16)
    k = d(W_MIX, 2 * W_MIX)
    k_ref[...] = k
    kb_ref[...] = k.astype(BF16)
    v = d(2 * W_MIX, 3 * W_MIX)
    v_ref[...] = v
    vb_ref[...] = v.astype(BF16)
    lf_ref[...] = _log_sigmoid(d(3 * W_MIX, 3 * W_MIX + LANES) + bf_ref[...])
    rw_ref[...] = d(3 * W_MIX + LANES, 3 * W_MIX + LANES + RW_PAD)


def _inproj(x2, g, w_a, bf_pad, tm):
    n = x2.shape[0]
    wcols = w_a.shape[1]
    row = lambda c: pl.BlockSpec((tm, c), lambda i: (i, 0))
    full = lambda r, c: pl.BlockSpec((r, c), lambda i: (0, 0))
    return pl.pallas_call(
        _inproj_kernel,
        grid=(n // tm,),
        in_specs=[row(D_MODEL), full(1, D_MODEL), full(D_MODEL, wcols), full(1, LANES)],
        out_specs=[row(W_MIX), row(W_MIX), row(W_MIX), row(W_MIX), row(W_MIX), row(LANES), row(RW_PAD)],
        out_shape=[jax.ShapeDtypeStruct((n, W_MIX), BF16), jax.ShapeDtypeStruct((n, W_MIX), F32),
                   jax.ShapeDtypeStruct((n, W_MIX), F32), jax.ShapeDtypeStruct((n, W_MIX), BF16),
                   jax.ShapeDtypeStruct((n, W_MIX), BF16), jax.ShapeDtypeStruct((n, LANES), F32),
                   jax.ShapeDtypeStruct((n, RW_PAD), F32)],
        compiler_params=_params("arbitrary"),
        name="inproj",
    )(x2, g, w_a, bf_pad)


def _cumsum_kernel(lf_ref, c_ref, carry):
    @pl.when(pl.program_id(1) == 0)
    def _():
        carry[...] = jnp.zeros_like(carry)

    lf = lf_ref[0]
    tm = lf.shape[0]
    row = lax.broadcasted_iota(jnp.int32, (tm, tm), 0)
    col = lax.broadcasted_iota(jnp.int32, (tm, tm), 1)
    c = _dot_sel_lhs((col <= row).astype(BF16), lf) + carry[...]
    c_ref[0] = c
    carry[...] = c[tm - 1:tm, :]


def _cumsum(lf3, tm):
    b, t, _ = lf3.shape
    spec = pl.BlockSpec((1, tm, LANES), lambda i, j: (i, j, 0))
    return pl.pallas_call(
        _cumsum_kernel,
        grid=(b, t // tm),
        in_specs=[spec],
        out_specs=spec,
        out_shape=jax.ShapeDtypeStruct(lf3.shape, F32),
        scratch_shapes=[pltpu.VMEM((1, LANES), F32)],
        compiler_params=_params("arbitrary", "arbitrary"),
        name="logf_cumsum",
    )(lf3)


HEADS_PER_GROUP = 4
GROUP_W = HEADS_PER_GROUP * D_HEAD


def _fox_prompt_kernel(q_ref, k_ref, v_ref, cq_ref, ck_ref, o_ref, qm, m_s, l_s, acc, *, tq, tk):
    i = pl.program_id(2)
    j = pl.program_id(3)
    lane = lax.broadcasted_iota(jnp.int32, (tq, GROUP_W), 1)

    @pl.when(j == 0)
    def _():
        q = q_ref[0].astype(F32)
        for h in range(HEADS_PER_GROUP):
            qm[h] = jnp.where((lane >= h * D_HEAD) & (lane < (h + 1) * D_HEAD), q, 0.0).astype(BF16)
        m_s[...] = jnp.full_like(m_s, NEG)
        l_s[...] = jnp.zeros_like(l_s)
        acc[...] = jnp.zeros_like(acc)

    @pl.when(j <= i)
    def _():
        k = k_ref[0]
        v = v_ref[0]
        cq = cq_ref[0, 0]
        ck = ck_ref[0, 0]
        row = i * tq + lax.broadcasted_iota(jnp.int32, (tq, tk), 0)
        col = j * tk + lax.broadcasted_iota(jnp.int32, (tq, tk), 1)
        causal = col <= row
        for h in range(HEADS_PER_GROUP):
            s = lax.dot_general(qm[h], k, (((1,), (1,)), ((), ())), preferred_element_type=F32)
            s = s + (cq[:, h:h + 1] - ck[h:h + 1, :])
            s = jnp.where(causal, s, NEG)
            m_prev = m_s[h]
            m_new = jnp.maximum(m_prev, jnp.max(s, axis=1, keepdims=True))
            alpha = jnp.exp(m_prev - m_new)
            p = jnp.exp(s - m_new)
            l_s[h] = alpha * l_s[h] + jnp.sum(p, axis=1, keepdims=True)
            acc[h] = alpha * acc[h] + jnp.dot(p.astype(BF16), v, preferred_element_type=F32)
            m_s[h] = m_new

    @pl.when(j == i)
    def _():
        out = jnp.zeros((tq, GROUP_W), F32)
        for h in range(HEADS_PER_GROUP):
            sel = (lane >= h * D_HEAD) & (lane < (h + 1) * D_HEAD)
            out = out + jnp.where(sel, acc[h] / l_s[h], 0.0)
        o_ref[0] = out.astype(o_ref.dtype)


def _fox_prompt(qb, kb, vb, cq, ck, tq):
    b, t, _ = qb.shape
    tk = tq
    n_g = W_MIX // GROUP_W
    nq = t // tq
    kern = functools.partial(_fox_prompt_kernel, tq=tq, tk=tk)
    return pl.pallas_call(
        kern,
        grid=(b, n_g, nq, nq),
        in_specs=[
            pl.BlockSpec((1, tq, GROUP_W), lambda bi, g, i, j: (bi, i, g)),
            pl.BlockSpec((1, tk, GROUP_W), lambda bi, g, i, j: (bi, jnp.minimum(j, i), g)),
            pl.BlockSpec((1, tk, GROUP_W), lambda bi, g, i, j: (bi, jnp.minimum(j, i), g)),
            pl.BlockSpec((1, 1, tq, HEADS_PER_GROUP), lambda bi, g, i, j: (bi, g, i, 0)),
            pl.BlockSpec((1, 1, HEADS_PER_GROUP, tk), lambda bi, g, i, j: (bi, g, 0, jnp.minimum(j, i))),
        ],
        out_specs=pl.BlockSpec((1, tq, GROUP_W), lambda bi, g, i, j: (bi, i, g)),
        out_shape=jax.ShapeDtypeStruct((b, t, W_MIX), BF16),
        scratch_shapes=[pltpu.VMEM((HEADS_PER_GROUP, tq, GROUP_W), BF16),
                        pltpu.VMEM((HEADS_PER_GROUP, tq, 1), F32),
                        pltpu.VMEM((HEADS_PER_GROUP, tq, 1), F32),
                        pltpu.VMEM((HEADS_PER_GROUP, tq, GROUP_W), F32)],
        compiler_params=_params("arbitrary", "arbitrary", "arbitrary", "arbitrary"),
        name="fox_prompt",
    )(qb, kb, vb, cq, ck)


def _fox_sample_kernel(pt_ref, q_ref, kn_ref, vn_ref, lfn_ref, kp_ref, vp_ref, lfp_ref, o_ref,
                       qm, m_s, l_s, acc, carry, cq_s, *, n_new, n_pages):
    del pt_ref
    j = pl.program_id(1)
    rows = n_new * N_HEADS
    lane8 = lax.broadcasted_iota(jnp.int32, (N_HEADS, W_MIX), 1)
    head8 = lax.broadcasted_iota(jnp.int32, (N_HEADS, W_MIX), 0)
    head_sel = (lane8 >> 6) == head8
    kr = lax.broadcasted_iota(jnp.int32, (PAGE, PAGE), 0)
    kc = lax.broadcasted_iota(jnp.int32, (PAGE, PAGE), 1)
    tile = lambda x: jnp.concatenate([x] * n_new, axis=0)

    def update(s, v):
        m_prev = m_s[...]
        m_new = jnp.maximum(m_prev, jnp.max(s, axis=1, keepdims=True))
        alpha = jnp.exp(m_prev - m_new)
        p = jnp.exp(s - m_new)
        l_s[...] = alpha * l_s[...] + jnp.sum(p, axis=1, keepdims=True)
        acc[...] = alpha * acc[...] + jnp.dot(p.astype(BF16), v, preferred_element_type=F32)
        m_s[...] = m_new

    @pl.when(j == 0)
    def _():
        q = q_ref[0].astype(F32)
        qrows = [jnp.where(head_sel, jnp.broadcast_to(q[t:t + 1, :], (N_HEADS, W_MIX)), 0.0)
                 for t in range(n_new)]
        qm[...] = jnp.concatenate(qrows, axis=0).astype(BF16)
        m_s[...] = jnp.full_like(m_s, NEG)
        l_s[...] = jnp.zeros_like(l_s)
        acc[...] = jnp.zeros_like(acc)
        carry[...] = jnp.zeros_like(carry)
        cn = _dot_sel_rhs(lfn_ref[0], (kr <= kc).astype(BF16))
        for t in range(n_new):
            cq_s[t * N_HEADS:(t + 1) * N_HEADS, :] = cn[:, t:t + 1]
        s = lax.dot_general(qm[...], kn_ref[0], (((1,), (1,)), ((), ())), preferred_element_type=F32)
        s = s + (cq_s[...] - tile(cn))
        tok = lax.broadcasted_iota(jnp.int32, (rows, PAGE), 0) >> 3
        key = lax.broadcasted_iota(jnp.int32, (rows, PAGE), 1)
        s = jnp.where(key <= tok, s, NEG)
        update(s, vn_ref[0])

    @pl.when(j > 0)
    def _():
        lf = lfp_ref[0]
        suffix = _dot_sel_rhs(lf, (kr > kc).astype(BF16))
        bias8 = suffix + carry[...]
        s = lax.dot_general(qm[...], kp_ref[0].astype(BF16), (((1,), (1,)), ((), ())),
                            preferred_element_type=F32)
        s = s + (cq_s[...] + tile(bias8))
        update(s, vp_ref[0].astype(BF16))
        carry[...] = carry[...] + jnp.sum(lf, axis=1, keepdims=True)

    @pl.when(j == n_pages)
    def _():
        o = acc[...] / l_s[...]
        for t in range(n_new):
            ot = jnp.where(head_sel, o[t * N_HEADS:(t + 1) * N_HEADS, :], 0.0)
            o_ref[0, t:t + 1, :] = jnp.sum(ot, axis=0, keepdims=True).astype(o_ref.dtype)


def _fox_sample(page_table, qb, kn, vn, lfn_t, pool_k, pool_v, pool_lf_t, n_new):
    r, n_pages = page_table.shape
    rows = n_new * N_HEADS
    kern = functools.partial(_fox_sample_kernel, n_new=n_new, n_pages=n_pages)
    page = lambda ri, j, pt: (pt[ri, n_pages - jnp.maximum(j, 1)], 0, 0)
    req = lambda ri, j, pt: (ri, 0, 0)
    grid_spec = pltpu.PrefetchScalarGridSpec(
        num_scalar_prefetch=1,
        grid=(r, n_pages + 1),
        in_specs=[
            pl.BlockSpec((1, n_new, W_MIX), req),
            pl.BlockSpec((1, PAGE, W_MIX), req),
            pl.BlockSpec((1, PAGE, W_MIX), req),
            pl.BlockSpec((1, N_HEADS, PAGE), req),
            pl.BlockSpec((1, PAGE, W_MIX), page),
            pl.BlockSpec((1, PAGE, W_MIX), page),
            pl.BlockSpec((1, N_HEADS, PAGE), page),
        ],
        out_specs=pl.BlockSpec((1, n_new, W_MIX), req),
        scratch_shapes=[pltpu.VMEM((rows, W_MIX), BF16), pltpu.VMEM((rows, 1), F32),
                        pltpu.VMEM((rows, 1), F32), pltpu.VMEM((rows, W_MIX), F32),
                        pltpu.VMEM((N_HEADS, 1), F32), pltpu.VMEM((rows, 1), F32)],
    )
    return pl.pallas_call(
        kern,
        grid_spec=grid_spec,
        out_shape=jax.ShapeDtypeStruct((r, n_new, W_MIX), F32),
        compiler_params=_params("arbitrary", "arbitrary"),
        name="fox_sample",
    )(page_table, qb, kn, vn, lfn_t, pool_k, pool_v, pool_lf_t)


def _tri_inv(a, m8, m16, m32, eye):
    x = jnp.where(m8, -a, 0.0)
    x2 = _dot(x, x)
    p = eye + x
    p = p + _dot(p, x2)
    x4 = _dot(x2, x2)
    p = p + _dot(p, x4)
    for outer, inner in ((m16, m8), (m32, m16), (None, m32)):
        off = ~inner if outer is None else (outer & ~inner)
        a_off = jnp.where(off, a, 0.0)
        p = p - _dot(_dot(p, a_off), p)
    return p


def _rwkv_kernel(u_ref, sh_ref, s0_ref, mu_ref, w0_ref, w2_ref, a0_ref, a2_ref, g2_ref, kk_ref, ka_ref,
                 rk_ref, lw_ref, lb_ref, e_ref, o_ref, sout_ref,
                 carry, st, kt_s, rt_s, kh_s, bh_s, k0_s, r0_s, khp_s, bhp_s, v_s, gt_s, o_s,
                 *, tt, t_real, t_total):
    ti = pl.program_id(1)

    @pl.when(ti == 0)
    def _():
        carry[...] = sh_ref[0]
        st[...] = s0_ref[0]

    u = u_ref[0]
    row1 = lax.broadcasted_iota(jnp.int32, (tt, 1), 0)
    u_prev = jnp.where(row1 == 0, carry[...], pltpu.roll(u, 1, 0))
    carry[...] = u[tt - 1:tt, :]
    us = u + mu_ref[...] * (u_prev - u)
    r = us[:, 0:W_MIX]
    k = us[:, W_MIX:2 * W_MIX]
    v = us[:, 2 * W_MIX:3 * W_MIX]
    wl = us[:, 3 * W_MIX:3 * W_MIX + LANES]
    al = us[:, 3 * W_MIX + LANES:3 * W_MIX + 2 * LANES]
    gl = us[:, 3 * W_MIX + 2 * LANES:3 * W_MIX + 3 * LANES]
    z = w0_ref[...] + _dot(jnp.tanh(wl), w2_ref[...])
    w = -(jnp.maximum(-z, 0.0) + jnp.log(1.0 + jnp.exp(-jnp.abs(z)))) - 0.5
    logd = -jnp.exp(w)
    a = _sigmoid(a0_ref[...] + _dot(al, a2_ref[...]))
    g = _dot(_sigmoid(gl), g2_ref[...])
    e_heads = e_ref[...]
    kk = k * kk_ref[...]
    kk = kk / jnp.maximum(jnp.sqrt(_dot(kk * kk, e_heads)), 1e-12)
    k_mod = k * (1.0 + (a - 1.0) * ka_ref[...])
    b = kk * a
    bonus = _dot(r * k_mod * rk_ref[...], e_heads) * v
    if t_real < t_total:
        valid = (ti * tt + row1) < t_real
        zero = lambda t: jnp.where(valid, t, 0.0)
        kk, b, k_mod, v, logd = zero(kk), zero(b), zero(k_mod), zero(v), zero(logd)

    rr = lax.broadcasted_iota(jnp.int32, (tt, tt), 0)
    cc = lax.broadcasted_iota(jnp.int32, (tt, tt), 1)
    same = (rr >> 6) == (cc >> 6)
    half = CHUNK // 2
    hi, mid, lo = _split3(logd)
    sel3 = lambda m: (jnp.dot(m.astype(BF16), hi, preferred_element_type=F32)
                      + jnp.dot(m.astype(BF16), mid, preferred_element_type=F32)
                      + jnp.dot(m.astype(BF16), lo, preferred_element_type=F32))
    cum = sel3(same & (cc <= rr))
    ref = sel3(same & ((cc & (CHUNK - 1)) < half))
    tot = sel3(same)
    cum_prev = cum - logd
    kt_s[...] = kk * jnp.exp(cum_prev - ref)
    rt_s[...] = r * jnp.exp(cum - ref)
    e_inv = jnp.exp(ref - cum)
    kh_s[...] = k_mod * e_inv
    bh_s[...] = b * e_inv
    k0_s[...] = kk * jnp.exp(cum_prev)
    r0_s[...] = r * jnp.exp(cum)
    e_tot = jnp.exp(tot - cum)
    khp_s[...] = k_mod * e_tot
    bhp_s[...] = b * e_tot
    v_s[...] = v
    gt_s[...] = jnp.exp(tot)

    ci = lax.broadcasted_iota(jnp.int32, (CHUNK, CHUNK), 0)
    cj = lax.broadcasted_iota(jnp.int32, (CHUNK, CHUNK), 1)
    strict = cj < ci
    incl = cj <= ci
    eye = (ci == cj).astype(F32)
    m8 = (ci >> 3) == (cj >> 3)
    m16 = (ci >> 4) == (cj >> 4)
    m32 = (ci >> 5) == (cj >> 5)

    def chunk_body(c, _):
        c0 = pl.multiple_of(c * CHUNK, CHUNK)
        rows = pl.ds(c0, CHUNK)
        for h in range(N_HEADS):
            hs = pl.ds(h * D_HEAD, D_HEAD)
            kt, rt, kh, bh = kt_s[rows, hs], rt_s[rows, hs], kh_s[rows, hs], bh_s[rows, hs]
            k0, r0, khp, bhp, vh = k0_s[rows, hs], r0_s[rows, hs], khp_s[rows, hs], bhp_s[rows, hs], v_s[rows, hs]
            a_kk = jnp.where(strict, _dot_nt(kt, kh), 0.0)
            a_kb = jnp.where(strict, _dot_nt(kt, bh), 0.0)
            a_rk = jnp.where(incl, _dot_nt(rt, kh), 0.0)
            a_rb = jnp.where(incl, _dot_nt(rt, bh), 0.0)
            t_inv = _tri_inv(a_kb, m8, m16, m32, eye)
            w1 = _dot(t_inv, k0)
            u1 = _dot(t_inv, _dot(a_kk, vh))
            q_t = r0 - _dot(a_rb, w1)
            o1 = _dot(a_rk, vh) - _dot(a_rb, u1)
            s_h = st[h]
            zz = _dot_nt_2(w1, s_h) + u1
            o_s[rows, hs] = _dot_nt_2(q_t, s_h) + o1
            grow = gt_s[pl.ds(c0, 1), hs]
            st[h] = s_h * grow + _dot(vh.T, khp) - _dot(zz.T, bhp)
        return 0

    lax.fori_loop(0, tt // CHUNK, chunk_body, 0)

    o = o_s[...]
    mean = _dot(o, e_heads) * (1.0 / D_HEAD)
    xc = o - mean
    var = _dot(xc * xc, e_heads) * (1.0 / D_HEAD)
    on = xc * lax.rsqrt(var + GN_EPS) * lw_ref[...] + lb_ref[...]
    o_ref[0] = ((on + bonus) * g).astype(o_ref.dtype)
    sout_ref[0] = st[...]


def _rwkv(u3, shift0, s0, vecs, w2p, a2p, g2, e_heads, tt, t_real):
    b, t, _ = u3.shape
    mu, w0, a0, k_k, k_a, r_k, lnw, lnb = vecs
    kern = functools.partial(_rwkv_kernel, tt=tt, t_real=t_real, t_total=t)
    vec = lambda c: pl.BlockSpec((1, c), lambda bi, ti: (0, 0))
    mat = lambda r_, c: pl.BlockSpec((r_, c), lambda bi, ti: (0, 0))
    wide = lambda: pltpu.VMEM((tt, W_MIX), F32)
    return pl.pallas_call(
        kern,
        grid=(b, t // tt),
        in_specs=[
            pl.BlockSpec((1, tt, RW_PAD), lambda bi, ti: (bi, ti, 0)),
            pl.BlockSpec((1, 1, RW_PAD), lambda bi, ti: (bi, 0, 0)),
            pl.BlockSpec((1, N_HEADS, D_HEAD, D_HEAD), lambda bi, ti: (bi, 0, 0, 0)),
            vec(RW_PAD), vec(W_MIX), mat(LANES, W_MIX), vec(W_MIX), mat(LANES, W_MIX), mat(LANES, W_MIX),
            vec(W_MIX), vec(W_MIX), vec(W_MIX), vec(W_MIX), vec(W_MIX), mat(W_MIX, W_MIX),
        ],
        out_specs=[pl.BlockSpec((1, tt, W_MIX), lambda bi, ti: (bi, ti, 0)),
                   pl.BlockSpec((1, N_HEADS, D_HEAD, D_HEAD), lambda bi, ti: (bi, 0, 0, 0))],
        out_shape=[jax.ShapeDtypeStruct((b, t, W_MIX), BF16),
                   jax.ShapeDtypeStruct((b, N_HEADS, D_HEAD, D_HEAD), F32)],
        scratch_shapes=[pltpu.VMEM((1, RW_PAD), F32), pltpu.VMEM((N_HEADS, D_HEAD, D_HEAD), F32)]
                       + [wide() for _ in range(11)],
        compiler_params=_params("arbitrary", "arbitrary"),
        name="rwkv7_chunked",
    )(u3, shift0, s0, mu, w0, w2p, a0, a2p, g2, k_k, k_a, r_k, lnw, lnb, e_heads)


def _mix_route_kernel(x_ref, oa_ref, ob_ref, gm_ref, wg_ref, pa_ref, pb_ref, wo_ref, gf_ref, wr_ref,
                      x1_ref, hn_ref, rw_ref):
    x = x_ref[...]
    xb = _rms(x, gm_ref[...]).astype(BF16)
    gates = _sigmoid(jnp.dot(xb, wg_ref[...], preferred_element_type=F32))
    mixed = (gates[:, :D_MODEL] * jnp.dot(oa_ref[...], pa_ref[...], preferred_element_type=F32)
             + gates[:, D_MODEL:] * jnp.dot(ob_ref[...], pb_ref[...], preferred_element_type=F32))
    x1 = x + _dot(mixed, wo_ref[...])
    x1_ref[...] = x1
    hb = _rms(x1, gf_ref[...]).astype(BF16)
    hn_ref[...] = hb
    logits = jnp.dot(hb, wr_ref[...], preferred_element_type=F32)
    tm = logits.shape[0]
    lane = lax.broadcasted_iota(jnp.int32, (tm, LANES), 1)
    lane_f = lane.astype(F32)
    first = lambda hit: jnp.min(jnp.where(hit, lane_f, 1e9), axis=1, keepdims=True)
    is_grp = (lane >= N_EXPERTS) & (lane < N_EXPERTS + N_GROUPS)
    gl = jnp.where(is_grp, logits, NEG)
    gmax = jnp.max(gl, axis=1, keepdims=True)
    g_idx = first(gl == gmax) - float(N_EXPERTS)
    p_grp = 1.0 / jnp.sum(jnp.where(is_grp, jnp.exp(gl - gmax), 0.0), axis=1, keepdims=True)
    in_grp = (lane < N_EXPERTS) & ((lane >> 3).astype(F32) == g_idx)
    el = jnp.where(in_grp, logits, NEG)
    v1 = jnp.max(el, axis=1, keepdims=True)
    i1 = first(el == v1)
    el2 = jnp.where(lane_f == i1, NEG, el)
    v2 = jnp.max(el2, axis=1, keepdims=True)
    i2 = first(el2 == v2)
    e2 = jnp.exp(v2 - v1)
    w1 = p_grp / (1.0 + e2)
    rw_ref[...] = jnp.where(lane_f == i1, w1, 0.0) + jnp.where(lane_f == i2, w1 * e2, 0.0)


def _mix_route(x2, oa, ob, gm, wg, pa, pb, wo, gf, wr, tm):
    n = x2.shape[0]
    row = lambda c: pl.BlockSpec((tm, c), lambda i: (i, 0))
    full = lambda r, c: pl.BlockSpec((r, c), lambda i: (0, 0))
    return pl.pallas_call(
        _mix_route_kernel,
        grid=(n // tm,),
        in_specs=[row(D_MODEL), row(W_MIX), row(W_MIX), full(1, D_MODEL), full(D_MODEL, 2 * D_MODEL),
                  full(W_MIX, D_MODEL), full(W_MIX, D_MODEL), full(D_MODEL, D_MODEL), full(1, D_MODEL),
                  full(D_MODEL, LANES)],
        out_specs=[row(D_MODEL), row(D_MODEL), row(LANES)],
        out_shape=[jax.ShapeDtypeStruct((n, D_MODEL), F32), jax.ShapeDtypeStruct((n, D_MODEL), BF16),
                   jax.ShapeDtypeStruct((n, LANES), F32)],
        compiler_params=_params("arbitrary"),
        name="mix_route",
    )(x2, oa, ob, gm, wg, pa, pb, wo, gf, wr)


def _moe_kernel(hn_ref, rw_ref, x1_ref, wgu_ref, wd_ref, gn_ref, y_ref, acc):
    e = pl.program_id(1)

    @pl.when(e == 0)
    def _():
        acc[...] = jnp.zeros_like(acc)

    h = jnp.dot(hn_ref[...], wgu_ref[0], preferred_element_type=F32)
    gate = h[:, :D_EXPERT]
    act = gate * _sigmoid(gate) * h[:, D_EXPERT:]
    yb = _dot(act, wd_ref[0])
    rw = rw_ref[...]
    lane = lax.broadcasted_iota(jnp.int32, rw.shape, 1)
    w_col = jnp.sum(jnp.where(lane == e, rw, 0.0), axis=1, keepdims=True)
    acc[...] += yb * w_col

    @pl.when(e == N_EXPERTS - 1)
    def _():
        y_ref[...] = _rms(x1_ref[...] + acc[...], gn_ref[...])


def _moe(hn, rw, x1, wgu, wd, gn, tm):
    n = hn.shape[0]
    row = lambda c: pl.BlockSpec((tm, c), lambda i, e: (i, 0))
    return pl.pallas_call(
        _moe_kernel,
        grid=(n // tm, N_EXPERTS),
        in_specs=[row(D_MODEL), row(LANES), row(D_MODEL),
                  pl.BlockSpec((1, D_MODEL, 2 * D_EXPERT), lambda i, e: (e, 0, 0)),
                  pl.BlockSpec((1, D_EXPERT, D_MODEL), lambda i, e: (e, 0, 0)),
                  pl.BlockSpec((1, D_MODEL), lambda i, e: (0, 0))],
        out_specs=row(D_MODEL),
        out_shape=jax.ShapeDtypeStruct((n, D_MODEL), F32),
        scratch_shapes=[pltpu.VMEM((tm, D_MODEL), F32)],
        compiler_params=_params("arbitrary", "arbitrary"),
        name="moe_dense",
    )(hn, rw, x1, wgu, wd, gn)


def _pad_rw_cols(m):
    z = jnp.zeros(m.shape[:-1] + (LANES - LORA_W,), m.dtype)
    c = 3 * W_MIX
    return jnp.concatenate([m[..., :c], m[..., c:c + LORA_W], z, m[..., c + LORA_W:c + 2 * LORA_W], z,
                            m[..., c + 2 * LORA_W:]], axis=-1)


def _unpad_rw_cols(m):
    c = 3 * W_MIX
    return jnp.concatenate([m[..., :c], m[..., c:c + LORA_W], m[..., c + LANES:c + LANES + LORA_W],
                            m[..., c + 2 * LANES:]], axis=-1)


def _pad_rows(m, rows):
    return jnp.concatenate([m, jnp.zeros((rows - m.shape[0],) + m.shape[1:], m.dtype)], axis=0)


def _prep_weights(l, norm_mix, w_in, b_f, mu_rw, w0, w2, a0, a2, g2, k_k, k_a, r_k, lnx_w, lnx_b,
                  p_a, p_b, w_o, norm_ffn, w_grp, w_exp, we_gate, we_up, we_down):
    w = w_in[l]
    w_fl = jnp.concatenate([w[:, 3 * W_MIX:FOX_COLS], jnp.zeros((D_MODEL, LANES - N_HEADS), F32)], axis=1)
    w_rw = _pad_rw_cols(w[:, FOX_COLS:FOX_COLS + RW_COLS])
    row = lambda vct: vct.reshape(1, -1)
    head_id = jnp.arange(W_MIX) // D_HEAD
    return dict(
        gm=row(norm_mix[l]),
        w_a=jnp.concatenate([w[:, :3 * W_MIX], w_fl, w_rw], axis=1).astype(BF16),
        bf=jnp.concatenate([b_f[l], jnp.zeros((LANES - N_HEADS,), F32)]).reshape(1, LANES),
        wg=w[:, FOX_COLS + RW_COLS:].astype(BF16),
        vecs=(row(_pad_rw_cols(mu_rw[l])), row(w0[l]), row(a0[l]), row(k_k[l]), row(k_a[l]),
              row(r_k[l].reshape(-1)), row(lnx_w[l]), row(lnx_b[l])),
        w2p=_pad_rows(w2[l], LANES).astype(BF16),
        a2p=_pad_rows(a2[l], LANES).astype(BF16),
        g2=g2[l].astype(BF16),
        e_heads=(head_id[:, None] == head_id[None, :]).astype(BF16),
        pa=p_a[l].astype(BF16), pb=p_b[l].astype(BF16), wo=w_o[l].astype(BF16),
        gf=row(norm_ffn[l]),
        wr=jnp.concatenate([w_exp[l], w_grp[l], jnp.zeros((D_MODEL, LANES - N_EXPERTS - N_GROUPS), F32)],
                           axis=1).astype(BF16),
        wgu=jnp.concatenate([we_gate[l], we_up[l]], axis=-1).astype(BF16),
        wd=we_down[l].astype(BF16),
    )


def _layer(x, past, shift0, wkv0, page_table, p, gn, tiles):
    b, t, _ = x.shape
    n = b * t
    tm, tq, tt, tm_moe = tiles
    x2 = x.reshape(n, D_MODEL)
    qb, k, v, kb, vb, lf, rw = _inproj(x2, p["gm"], p["w_a"], p["bf"], tm)
    to3 = lambda m: m.reshape(b, t, m.shape[-1])
    if past is None:
        c = _cumsum(to3(lf), tq)[:, :, :N_HEADS]
        n_g = W_MIX // GROUP_W
        cq = c.reshape(b, t, n_g, HEADS_PER_GROUP).transpose(0, 2, 1, 3)
        ck = cq.transpose(0, 1, 3, 2)
        o_a = _fox_prompt(to3(qb), to3(kb), to3(vb), cq, ck, tq)
        t_pad = t
        u3 = to3(rw)
    else:
        pool_k, pool_v, pool_lf_t = past
        pad_keys = lambda m: jnp.concatenate([to3(m), jnp.zeros((b, PAGE - t, W_MIX), BF16)], axis=1)
        lfn_t = jnp.concatenate([to3(lf)[:, :, :N_HEADS].transpose(0, 2, 1),
                                 jnp.zeros((b, N_HEADS, PAGE - t), F32)], axis=2)
        o_a = _fox_sample(page_table, to3(qb).astype(F32), pad_keys(kb), pad_keys(vb), lfn_t,
                          pool_k, pool_v, pool_lf_t, t).astype(BF16)
        t_pad = tt
        u3 = jnp.concatenate([to3(rw), jnp.zeros((b, t_pad - t, RW_PAD), F32)], axis=1)
    o_b, wkv_new = _rwkv(u3, shift0, wkv0, p["vecs"], p["w2p"], p["a2p"], p["g2"], p["e_heads"], tt, t)
    o_b = o_b[:, :t].reshape(n, W_MIX)
    x1, hn, route = _mix_route(x2, o_a.reshape(n, W_MIX), o_b, p["gm"], p["wg"], p["pa"], p["pb"], p["wo"],
                               p["gf"], p["wr"], tm)
    y = _moe(hn, route, x1, p["wgu"], p["wd"], gn, tm_moe)
    shift_new = _unpad_rw_cols(to3(rw)[:, t - 1, :])
    return (y.reshape(b, t, D_MODEL), k.reshape(b, t, N_HEADS, D_HEAD), v.reshape(b, t, N_HEADS, D_HEAD),
            to3(lf)[:, :, :N_HEADS], wkv_new, shift_new)


def kernel(x_prompt, x_sample, cache_k, cache_v, cache_logf, page_table, state_wkv, state_shift, norm_mix, w_in, b_f, mu_rw, w0, w2, a0, a2, g2, k_k, k_a, r_k, lnx_w, lnx_b, p_a, p_b, w_o, norm_ffn, w_grp, w_exp, we_gate, we_up, we_down, norm_final):
    depth = w_in.shape[0]
    assert depth == 1, "final norm is fused into the layer's last kernel"
    bp, tp, _ = x_prompt.shape
    bs, ts, _ = x_sample.shape
    n_phys = cache_k.shape[1]
    gn = norm_final.reshape(1, D_MODEL)
    l = 0
    p = _prep_weights(l, norm_mix, w_in, b_f, mu_rw, w0, w2, a0, a2, g2, k_k, k_a, r_k, lnx_w, lnx_b,
                      p_a, p_b, w_o, norm_ffn, w_grp, w_exp, we_gate, we_up, we_down)
    tq = min(512, tp)
    prompt = _layer(x_prompt, None, jnp.zeros((bp, 1, RW_PAD), F32),
                    jnp.zeros((bp, N_HEADS, D_HEAD, D_HEAD), F32), None, p, gn,
                    (min(512, bp * tp), tq, min(256, tp), min(1024, bp * tp)))
    past = (cache_k[l].reshape(n_phys, PAGE, W_MIX), cache_v[l].reshape(n_phys, PAGE, W_MIX),
            cache_logf[l].transpose(0, 2, 1))
    sample = _layer(x_sample, past, _pad_rw_cols(state_shift[l])[:, None, :], state_wkv[l], page_table, p, gn,
                    (bs * ts, None, CHUNK, bs * ts))
    outs = []
    for y, k, v, lf, wkv, sh in (prompt, sample):
        outs.append((y, k[None], v[None], lf[None], wkv[None], sh[None]))
    (yp, kp, vp, lp, wp, sp), (ys, ks, vs, ls, ws, ss) = outs
    return (yp, ys, kp, vp, lp, wp, sp, ks, vs, ls, ws, ss)
```

```python
import functools

import jax
import jax.numpy as jnp
from jax import lax
from jax.experimental import pallas as pl
from jax.experimental.pallas import tpu as pltpu

F32 = jnp.float32
BF16 = jnp.bfloat16

D_MODEL = 1024
N_HEADS = 8
D_HEAD = 64
W_MIX = N_HEADS * D_HEAD
LORA_W = 64
LORA_G = 128
RW_COLS = 3 * W_MIX + 2 * LORA_W + LORA_G
RW_PAD = 3 * W_MIX + 3 * 128
FOX_COLS = 3 * W_MIX + N_HEADS
N_EXPERTS = 32
EXPERTS_PER_GROUP = 8
N_GROUPS = 4
D_EXPERT = 256
PAGE = 128
NORM_EPS = 1e-6
GN_EPS = 64e-5
QK_SCALE = D_HEAD ** -0.5
LANES = 128
CHUNK = 64
NEG = -1e30
VMEM_LIMIT = 56 * 1024 * 1024


def _dot(a, b):
    return jnp.dot(a.astype(BF16), b.astype(BF16), preferred_element_type=F32)


def _bdot(a, b):
    return jnp.dot(a, b, preferred_element_type=F32)


def _bdot_nt(a, b):
    return lax.dot_general(a, b, (((1,), (1,)), ((), ())), preferred_element_type=F32)


def _each(f, *lists):
    return [f(*xs) for xs in zip(*lists)]


def _split3(x):
    hi = x.astype(BF16)
    r1 = x - hi.astype(F32)
    mid = r1.astype(BF16)
    lo = (r1 - mid.astype(F32)).astype(BF16)
    return hi, mid, lo


def _dot_sel_lhs(sel, x):
    hi, mid, lo = _split3(x)
    d = lambda p: jnp.dot(sel, p, preferred_element_type=F32)
    return d(hi) + d(mid) + d(lo)


def _dot_sel_rhs(x, sel):
    hi, mid, lo = _split3(x)
    d = lambda p: jnp.dot(p, sel, preferred_element_type=F32)
    return d(hi) + d(mid) + d(lo)


def _sigmoid(x):
    return 1.0 / (1.0 + jnp.exp(-x))


def _log_sigmoid(x):
    return jnp.minimum(x, 0.0) - jnp.log(1.0 + jnp.exp(-jnp.abs(x)))


def _rms(x, g):
    return (x * lax.rsqrt(jnp.mean(x * x, axis=-1, keepdims=True) + NORM_EPS)) * g


def _params(*sem):
    return pltpu.CompilerParams(dimension_semantics=sem, vmem_limit_bytes=VMEM_LIMIT)


def _inproj_kernel(x_ref, g_ref, w_ref, bf_ref, q_ref, k_ref, v_ref, kb_ref, vb_ref, lf_ref, rw_ref):
    xb = _rms(x_ref[...], g_ref[...]).astype(BF16)
    d = lambda lo, hi: jnp.dot(xb, w_ref[:, lo:hi], preferred_element_type=F32)
    q_ref[...] = (d(0, W_MIX) * QK_SCALE).astype(BF16)
    k = d(W_MIX, 2 * W_MIX)
    k_ref[...] = k
    kb_ref[...] = k.astype(BF16)
    v = d(2 * W_MIX, 3 * W_MIX)
    v_ref[...] = v
    vb_ref[...] = v.astype(BF16)
    lf_ref[...] = _log_sigmoid(d(3 * W_MIX, 3 * W_MIX + LANES) + bf_ref[...])
    rw_ref[...] = d(3 * W_MIX + LANES, 3 * W_MIX + LANES + RW_PAD)


def _inproj(x2, g, w_a, bf_pad, tm):
    n = x2.shape[0]
    wcols = w_a.shape[1]
    row = lambda c: pl.BlockSpec((tm, c), lambda i: (i, 0))
    full = lambda r, c: pl.BlockSpec((r, c), lambda i: (0, 0))
    return pl.pallas_call(
        _inproj_kernel,
        grid=(n // tm,),
        in_specs=[row(D_MODEL), full(1, D_MODEL), full(D_MODEL, wcols), full(1, LANES)],
        out_specs=[row(W_MIX), row(W_MIX), row(W_MIX), row(W_MIX), row(W_MIX), row(LANES), row(RW_PAD)],
        out_shape=[jax.ShapeDtypeStruct((n, W_MIX), BF16), jax.ShapeDtypeStruct((n, W_MIX), F32),
                   jax.ShapeDtypeStruct((n, W_MIX), F32), jax.ShapeDtypeStruct((n, W_MIX), BF16),
                   jax.ShapeDtypeStruct((n, W_MIX), BF16), jax.ShapeDtypeStruct((n, LANES), F32),
                   jax.ShapeDtypeStruct((n, RW_PAD), F32)],
        compiler_params=_params("arbitrary"),
        name="inproj",
    )(x2, g, w_a, bf_pad)


def _cumsum_kernel(lf_ref, c_ref, carry):
    @pl.when(pl.program_id(1) == 0)
    def _():
        carry[...] = jnp.zeros_like(carry)

    lf = lf_ref[0]
    tm = lf.shape[0]
    row = lax.broadcasted_iota(jnp.int32, (tm, tm), 0)
    col = lax.broadcasted_iota(jnp.int32, (tm, tm), 1)
    c = _dot_sel_lhs((col <= row).astype(BF16), lf) + carry[...]
    c_ref[0] = c
    carry[...] = c[tm - 1:tm, :]


def _cumsum(lf3, tm):
    b, t, _ = lf3.shape
    spec = pl.BlockSpec((1, tm, LANES), lambda i, j: (i, j, 0))
    return pl.pallas_call(
        _cumsum_kernel,
        grid=(b, t // tm),
        in_specs=[spec],
        out_specs=spec,
        out_shape=jax.ShapeDtypeStruct(lf3.shape, F32),
        scratch_shapes=[pltpu.VMEM((1, LANES), F32)],
        compiler_params=_params("arbitrary", "arbitrary"),
        name="logf_cumsum",
    )(lf3)


HEADS_PER_GROUP = 4
GROUP_W = HEADS_PER_GROUP * D_HEAD


def _fox_prompt_kernel(q_ref, k_ref, v_ref, cq_ref, ck_ref, o_ref, qm, m_s, l_s, acc, *, tq, tk):
    i = pl.program_id(2)
    j = pl.program_id(3)
    lane = lax.broadcasted_iota(jnp.int32, (tq, GROUP_W), 1)

    @pl.when(j == 0)
    def _():
        q = q_ref[0].astype(F32)
        for h in range(HEADS_PER_GROUP):
            qm[h] = jnp.where((lane >= h * D_HEAD) & (lane < (h + 1) * D_HEAD), q, 0.0).astype(BF16)
        m_s[...] = jnp.full_like(m_s, NEG)
        l_s[...] = jnp.zeros_like(l_s)
        acc[...] = jnp.zeros_like(acc)

    @pl.when(j <= i)
    def _():
        k = k_ref[0]
        v = v_ref[0]
        cq = cq_ref[0, 0]
        ck = ck_ref[0, 0]
        row = i * tq + lax.broadcasted_iota(jnp.int32, (tq, tk), 0)
        col = j * tk + lax.broadcasted_iota(jnp.int32, (tq, tk), 1)
        causal = col <= row
        for h in range(HEADS_PER_GROUP):
            s = lax.dot_general(qm[h], k, (((1,), (1,)), ((), ())), preferred_element_type=F32)
            s = s + (cq[:, h:h + 1] - ck[h:h + 1, :])
            s = jnp.where(causal, s, NEG)
            m_prev = m_s[h]
            m_new = jnp.maximum(m_prev, jnp.max(s, axis=1, keepdims=True))
            alpha = jnp.exp(m_prev - m_new)
            p = jnp.exp(s - m_new)
            l_s[h] = alpha * l_s[h] + jnp.sum(p, axis=1, keepdims=True)
            acc[h] = alpha * acc[h] + jnp.dot(p.astype(BF16), v, preferred_element_type=F32)
            m_s[h] = m_new

    @pl.when(j == i)
    def _():
        out = jnp.zeros((tq, GROUP_W), F32)
        for h in range(HEADS_PER_GROUP):
            sel = (lane >= h * D_HEAD) & (lane < (h + 1) * D_HEAD)
            out = out + jnp.where(sel, acc[h] / l_s[h], 0.0)
        o_ref[0] = out.astype(o_ref.dtype)


def _fox_prompt(qb, kb, vb, cq, ck, tq):
    b, t, _ = qb.shape
    tk = tq
    n_g = W_MIX // GROUP_W
    nq = t // tq
    kern = functools.partial(_fox_prompt_kernel, tq=tq, tk=tk)
    return pl.pallas_call(
        kern,
        grid=(b, n_g, nq, nq),
        in_specs=[
            pl.BlockSpec((1, tq, GROUP_W), lambda bi, g, i, j: (bi, i, g)),
            pl.BlockSpec((1, tk, GROUP_W), lambda bi, g, i, j: (bi, jnp.minimum(j, i), g)),
            pl.BlockSpec((1, tk, GROUP_W), lambda bi, g, i, j: (bi, jnp.minimum(j, i), g)),
            pl.BlockSpec((1, 1, tq, HEADS_PER_GROUP), lambda bi, g, i, j: (bi, g, i, 0)),
            pl.BlockSpec((1, 1, HEADS_PER_GROUP, tk), lambda bi, g, i, j: (bi, g, 0, jnp.minimum(j, i))),
        ],
        out_specs=pl.BlockSpec((1, tq, GROUP_W), lambda bi, g, i, j: (bi, i, g)),
        out_shape=jax.ShapeDtypeStruct((b, t, W_MIX), BF16),
        scratch_shapes=[pltpu.VMEM((HEADS_PER_GROUP, tq, GROUP_W), BF16),
                        pltpu.VMEM((HEADS_PER_GROUP, tq, 1), F32),
                        pltpu.VMEM((HEADS_PER_GROUP, tq, 1), F32),
                        pltpu.VMEM((HEADS_PER_GROUP, tq, GROUP_W), F32)],
        compiler_params=_params("arbitrary", "arbitrary", "arbitrary", "arbitrary"),
        name="fox_prompt",
    )(qb, kb, vb, cq, ck)


def _fox_sample_kernel(pt_ref, q_ref, kn_ref, vn_ref, lfn_ref, kp_ref, vp_ref, lfp_ref, o_ref,
                       qm, m_s, l_s, acc, carry, cq_s, *, n_new, n_pages):
    del pt_ref
    j = pl.program_id(1)
    rows = n_new * N_HEADS
    lane8 = lax.broadcasted_iota(jnp.int32, (N_HEADS, W_MIX), 1)
    head8 = lax.broadcasted_iota(jnp.int32, (N_HEADS, W_MIX), 0)
    head_sel = (lane8 >> 6) == head8
    kr = lax.broadcasted_iota(jnp.int32, (PAGE, PAGE), 0)
    kc = lax.broadcasted_iota(jnp.int32, (PAGE, PAGE), 1)
    tile = lambda x: jnp.concatenate([x] * n_new, axis=0)

    def update(s, pv):
        m_prev = m_s[...]
        m_new = jnp.maximum(m_prev, jnp.max(s, axis=1, keepdims=True))
        alpha = jnp.exp(m_prev - m_new)
        p = jnp.exp(s - m_new)
        l_s[...] = alpha * l_s[...] + jnp.sum(p, axis=1, keepdims=True)
        acc[...] = alpha * acc[...] + pv(p.astype(BF16))
        m_s[...] = m_new

    @pl.when(j == 0)
    def _():
        q = q_ref[0].astype(F32)
        qrows = [jnp.where(head_sel, jnp.broadcast_to(q[t:t + 1, :], (N_HEADS, W_MIX)), 0.0)
                 for t in range(n_new)]
        qm[...] = jnp.concatenate(qrows, axis=0).astype(BF16)
        m_s[...] = jnp.full_like(m_s, NEG)
        l_s[...] = jnp.zeros_like(l_s)
        acc[...] = jnp.zeros_like(acc)
        carry[...] = jnp.zeros_like(carry)
        cn = _dot_sel_rhs(lfn_ref[0], (kr <= kc).astype(BF16))
        for t in range(n_new):
            cq_s[t * N_HEADS:(t + 1) * N_HEADS, :] = cn[:, t:t + 1]
        s = lax.dot_general(qm[...], kn_ref[0], (((1,), (1,)), ((), ())), preferred_element_type=F32)
        s = s + (cq_s[...] - tile(cn))
        tok = lax.broadcasted_iota(jnp.int32, (rows, PAGE), 0) >> 3
        key = lax.broadcasted_iota(jnp.int32, (rows, PAGE), 1)
        s = jnp.where(key <= tok, s, NEG)
        update(s, lambda p: _bdot(p, vn_ref[0]))

    @pl.when(j > 0)
    def _():
        lf = lfp_ref[0]
        suffix = _dot_sel_rhs(lf, (kr > kc).astype(BF16))
        bias8 = suffix + carry[...]
        s = _bdot(qm[...], kp_ref[0].astype(BF16))
        s = s + (cq_s[...] + tile(bias8))
        update(s, lambda p: _bdot_nt(p, vp_ref[0].astype(BF16)))
        carry[...] = carry[...] + jnp.sum(lf, axis=1, keepdims=True)

    @pl.when(j == n_pages)
    def _():
        o = acc[...] / l_s[...]
        for t in range(n_new):
            ot = jnp.where(head_sel, o[t * N_HEADS:(t + 1) * N_HEADS, :], 0.0)
            o_ref[0, t:t + 1, :] = jnp.sum(ot, axis=0, keepdims=True).astype(o_ref.dtype)


def _fox_sample(page_table, qb, kn, vn, lfn_t, pool_k, pool_v, pool_lf_t, n_new):
    r, n_pages = page_table.shape
    rows = n_new * N_HEADS
    kern = functools.partial(_fox_sample_kernel, n_new=n_new, n_pages=n_pages)
    page = lambda ri, j, pt: (pt[ri, n_pages - jnp.maximum(j, 1)], 0, 0)
    req = lambda ri, j, pt: (ri, 0, 0)
    grid_spec = pltpu.PrefetchScalarGridSpec(
        num_scalar_prefetch=1,
        grid=(r, n_pages + 1),
        in_specs=[
            pl.BlockSpec((1, n_new, W_MIX), req),
            pl.BlockSpec((1, PAGE, W_MIX), req),
            pl.BlockSpec((1, PAGE, W_MIX), req),
            pl.BlockSpec((1, N_HEADS, PAGE), req),
            pl.BlockSpec((1, W_MIX, PAGE), page),
            pl.BlockSpec((1, W_MIX, PAGE), page),
            pl.BlockSpec((1, N_HEADS, PAGE), page),
        ],
        out_specs=pl.BlockSpec((1, n_new, W_MIX), req),
        scratch_shapes=[pltpu.VMEM((rows, W_MIX), BF16), pltpu.VMEM((rows, 1), F32),
                        pltpu.VMEM((rows, 1), F32), pltpu.VMEM((rows, W_MIX), F32),
                        pltpu.VMEM((N_HEADS, 1), F32), pltpu.VMEM((rows, 1), F32)],
    )
    return pl.pallas_call(
        kern,
        grid_spec=grid_spec,
        out_shape=jax.ShapeDtypeStruct((r, n_new, W_MIX), F32),
        compiler_params=_params("arbitrary", "arbitrary"),
        name="fox_sample",
    )(page_table, qb, kn, vn, lfn_t, pool_k, pool_v, pool_lf_t)


HEADS_PER_ITER = 2


def _rwkv_kernel(u_ref, sh_ref, s0_ref, mu_ref, w0_ref, w2_ref, a0_ref, a2_ref, g2_ref, kk_ref, ka_ref,
                 rk_ref, lw_ref, lb_ref, e_ref, msk_ref, o_ref, sout_ref,
                 carry, st, kt_s, rt_s, kh_s, bh_s, k0_s, r0_s, khp_s, bhp_s, v_s, gt_s, o_s,
                 *, tt, t_real, t_total):
    ti = pl.program_id(1)
    nch = tt // CHUNK

    @pl.when(ti == 0)
    def _():
        carry[...] = sh_ref[0]
        st[...] = s0_ref[0]

    u = u_ref[0]
    row1 = lax.broadcasted_iota(jnp.int32, (tt, 1), 0)
    u_prev = jnp.where(row1 == 0, carry[...], pltpu.roll(u, 1, 0))
    carry[...] = u[tt - 1:tt, :]
    us = u + mu_ref[...] * (u_prev - u)
    r = us[:, 0:W_MIX]
    k = us[:, W_MIX:2 * W_MIX]
    v = us[:, 2 * W_MIX:3 * W_MIX]
    wl = us[:, 3 * W_MIX:3 * W_MIX + LANES]
    al = us[:, 3 * W_MIX + LANES:3 * W_MIX + 2 * LANES]
    gl = us[:, 3 * W_MIX + 2 * LANES:3 * W_MIX + 3 * LANES]
    z = w0_ref[...] + _dot(jnp.tanh(wl), w2_ref[...])
    w = -(jnp.maximum(-z, 0.0) + jnp.log(1.0 + jnp.exp(-jnp.abs(z)))) - 0.5
    logd = -jnp.exp(w)
    a = _sigmoid(a0_ref[...] + _dot(al, a2_ref[...]))
    g = _dot(_sigmoid(gl), g2_ref[...])
    e_heads = e_ref[...]
    kk = k * kk_ref[...]
    kk = kk / jnp.maximum(jnp.sqrt(_dot(kk * kk, e_heads)), 1e-12)
    k_mod = k * (1.0 + (a - 1.0) * ka_ref[...])
    b = kk * a
    bonus = _dot(r * k_mod * rk_ref[...], e_heads) * v
    if t_real < t_total:
        valid = (ti * tt + row1) < t_real
        zero = lambda t: jnp.where(valid, t, 0.0)
        kk, b, k_mod, v, logd = zero(kk), zero(b), zero(k_mod), zero(v), zero(logd)

    hi, mid, lo = _split3(logd)
    sel3 = lambda m: (_bdot(m, hi) + _bdot(m, mid) + _bdot(m, lo))
    cum = sel3(msk_ref[1].astype(BF16))
    ref = sel3(msk_ref[7].astype(BF16))
    tot = sel3(msk_ref[8].astype(BF16))
    cum_prev = cum - logd
    e_inv = jnp.exp(ref - cum)
    e_tot = jnp.exp(tot - cum)
    wide = dict(kt=kk * jnp.exp(cum_prev - ref), rt=r * jnp.exp(cum - ref), kh=k_mod * e_inv, bh=b * e_inv,
                k0=kk * jnp.exp(cum_prev), r0=r * jnp.exp(cum), khp=k_mod * e_tot, bhp=b * e_tot, v=v,
                gt=jnp.exp(tot))
    dst = dict(kt=kt_s, rt=rt_s, kh=kh_s, bh=bh_s, k0=k0_s, r0=r0_s, khp=khp_s, bhp=bhp_s, v=v_s, gt=gt_s)
    for name, val in wide.items():
        for h in range(N_HEADS):
            dst[name][h] = val[:, h * D_HEAD:(h + 1) * D_HEAD]

    def head_body(p, _):
        hs = [p * HEADS_PER_ITER + i for i in range(HEADS_PER_ITER)]
        ld = lambda s: [s[h] for h in hs]
        ldb = lambda s: [s[h].astype(BF16) for h in hs]
        kt, rt, kh, bh = ldb(kt_s), ldb(rt_s), ldb(kh_s), ldb(bh_s)
        strict = msk_ref[0] != 0.0
        incl = msk_ref[1] != 0.0
        a_kk = _each(lambda x, y: jnp.where(strict, _bdot_nt(x, y), 0.0).astype(BF16), kt, kh)
        a_kb = _each(lambda x, y: jnp.where(strict, _bdot_nt(x, y), 0.0), kt, bh)
        a_rk = _each(lambda x, y: jnp.where(incl, _bdot_nt(x, y), 0.0).astype(BF16), rt, kh)
        a_rb = _each(lambda x, y: jnp.where(incl, _bdot_nt(x, y), 0.0).astype(BF16), rt, bh)
        x = _each(lambda t: (-(t * msk_ref[2])).astype(BF16), a_kb)
        x2 = _each(lambda t: _bdot(t, t).astype(BF16), x)
        pm = _each(lambda t: msk_ref[6] + t.astype(F32), x)
        pm = _each(lambda q, t: q + _bdot(q.astype(BF16), t), pm, x2)
        x4 = _each(lambda t: _bdot(t, t).astype(BF16), x2)
        pm = _each(lambda q, t: q + _bdot(q.astype(BF16), t), pm, x4)
        for lvl in (3, 4, 5):
            pb = _each(lambda q: q.astype(BF16), pm)
            a_off = _each(lambda t: (t * msk_ref[lvl]).astype(BF16), a_kb)
            pa = _each(lambda q, t: _bdot(q, t).astype(BF16), pb, a_off)
            pm = _each(lambda q, s, qb: q - _bdot(s, qb), pm, pa, pb)
        t_inv = _each(lambda q: q.astype(BF16), pm)
        vb = ldb(v_s)
        w1 = _each(lambda t, y: _bdot(t, y), t_inv, ldb(k0_s))
        av = _each(lambda t, y: _bdot(t, y).astype(BF16), a_kk, vb)
        u1 = _each(lambda t, y: _bdot(t, y), t_inv, av)
        w1b = _each(lambda t: t.astype(BF16), w1)
        u1b = _each(lambda t: t.astype(BF16), u1)
        q_t = _each(lambda r0, t, y: (r0 - _bdot(t, y)).astype(BF16), ld(r0_s), a_rb, w1b)
        o1 = _each(lambda t, y, t2, y2: _bdot(t, y) - _bdot(t2, y2), a_rk, vb, a_rb, u1b)
        khp, bhp, gt = ld(khp_s), ld(bhp_s), ld(gt_s)
        for c in range(nch):
            rows = slice(c * CHUNK, (c + 1) * CHUNK)
            bt = _each(lambda t: t[rows].T.astype(BF16), bhp)
            kt_ = _each(lambda t: t[rows].T.astype(BF16), khp)
            gcol = _each(lambda t: t[rows].T[:, 0:1], gt)
            m_low = _each(lambda t, y: _bdot(t, y[rows]).astype(BF16), bt, w1b)
            n1 = _each(lambda t, y, t2, y2: _bdot(t, y[rows]) - _bdot(t2, y2[rows]), kt_, vb, bt, u1b)
            for i, h in enumerate(hs):
                s_t = st[h]
                s_hi = s_t.astype(BF16)
                s_lo = (s_t - s_hi.astype(F32)).astype(BF16)
                qc = q_t[i][rows]
                o_s[h, rows, :] = _bdot(qc, s_hi) + _bdot(qc, s_lo) + o1[i][rows]
                st[h] = gcol[i] * s_t - (_bdot(m_low[i], s_hi) + _bdot(m_low[i], s_lo)) + n1[i]
        return 0

    lax.fori_loop(0, N_HEADS // HEADS_PER_ITER, head_body, 0)

    o = jnp.concatenate([o_s[h] for h in range(N_HEADS)], axis=1)
    mean = _dot(o, e_heads) * (1.0 / D_HEAD)
    xc = o - mean
    var = _dot(xc * xc, e_heads) * (1.0 / D_HEAD)
    on = xc * lax.rsqrt(var + GN_EPS) * lw_ref[...] + lb_ref[...]
    o_ref[0] = ((on + bonus) * g).astype(o_ref.dtype)
    sout_ref[0] = st[...]


def _chunk_masks(tt):
    i = jnp.arange(tt)[:, None]
    j = jnp.arange(tt)[None, :]
    blk = lambda s: (i >> s) == (j >> s)
    same = blk(6)
    ms = [same & (j < i), same & (j <= i), blk(3), blk(4) & ~blk(3), blk(5) & ~blk(4), same & ~blk(5),
          i == j, same & ((j & (CHUNK - 1)) < CHUNK // 2), same]
    return jnp.stack(ms).astype(F32)


def _rwkv(u3, shift0, s0_t, vecs, w2p, a2p, g2, e_heads, tt, t_real):
    b, t, _ = u3.shape
    mu, w0, a0, k_k, k_a, r_k, lnw, lnb = vecs
    masks = _chunk_masks(tt)
    kern = functools.partial(_rwkv_kernel, tt=tt, t_real=t_real, t_total=t)
    vec = lambda c: pl.BlockSpec((1, c), lambda bi, ti: (0, 0))
    mat = lambda r_, c: pl.BlockSpec((r_, c), lambda bi, ti: (0, 0))
    wide = lambda: pltpu.VMEM((N_HEADS, tt, D_HEAD), F32)
    return pl.pallas_call(
        kern,
        grid=(b, t // tt),
        in_specs=[
            pl.BlockSpec((1, tt, RW_PAD), lambda bi, ti: (bi, ti, 0)),
            pl.BlockSpec((1, 1, RW_PAD), lambda bi, ti: (bi, 0, 0)),
            pl.BlockSpec((1, N_HEADS, D_HEAD, D_HEAD), lambda bi, ti: (bi, 0, 0, 0)),
            vec(RW_PAD), vec(W_MIX), mat(LANES, W_MIX), vec(W_MIX), mat(LANES, W_MIX), mat(LANES, W_MIX),
            vec(W_MIX), vec(W_MIX), vec(W_MIX), vec(W_MIX), vec(W_MIX), mat(W_MIX, W_MIX),
            pl.BlockSpec(masks.shape, lambda bi, ti: (0, 0, 0)),
        ],
        out_specs=[pl.BlockSpec((1, tt, W_MIX), lambda bi, ti: (bi, ti, 0)),
                   pl.BlockSpec((1, N_HEADS, D_HEAD, D_HEAD), lambda bi, ti: (bi, 0, 0, 0))],
        out_shape=[jax.ShapeDtypeStruct((b, t, W_MIX), BF16),
                   jax.ShapeDtypeStruct((b, N_HEADS, D_HEAD, D_HEAD), F32)],
        scratch_shapes=[pltpu.VMEM((1, RW_PAD), F32), pltpu.VMEM((N_HEADS, D_HEAD, D_HEAD), F32)]
                       + [wide() for _ in range(11)],
        compiler_params=_params("arbitrary", "arbitrary"),
        name="rwkv7_chunked",
    )(u3, shift0, s0_t, mu, w0, w2p, a0, a2p, g2, k_k, k_a, r_k, lnw, lnb, e_heads, masks)


def _mix_route_kernel(x_ref, oa_ref, ob_ref, gm_ref, wg_ref, pa_ref, pb_ref, wo_ref, gf_ref, wr_ref,
                      x1_ref, hn_ref, rw_ref):
    x = x_ref[...]
    xb = _rms(x, gm_ref[...]).astype(BF16)
    gates = _sigmoid(jnp.dot(xb, wg_ref[...], preferred_element_type=F32))
    mixed = (gates[:, :D_MODEL] * jnp.dot(oa_ref[...], pa_ref[...], preferred_element_type=F32)
             + gates[:, D_MODEL:] * jnp.dot(ob_ref[...], pb_ref[...], preferred_element_type=F32))
    x1 = x + _dot(mixed, wo_ref[...])
    x1_ref[...] = x1
    hb = _rms(x1, gf_ref[...]).astype(BF16)
    hn_ref[...] = hb
    logits = jnp.dot(hb, wr_ref[...], preferred_element_type=F32)
    tm = logits.shape[0]
    lane = lax.broadcasted_iota(jnp.int32, (tm, LANES), 1)
    lane_f = lane.astype(F32)
    first = lambda hit: jnp.min(jnp.where(hit, lane_f, 1e9), axis=1, keepdims=True)
    is_grp = (lane >= N_EXPERTS) & (lane < N_EXPERTS + N_GROUPS)
    gl = jnp.where(is_grp, logits, NEG)
    gmax = jnp.max(gl, axis=1, keepdims=True)
    g_idx = first(gl == gmax) - float(N_EXPERTS)
    p_grp = 1.0 / jnp.sum(jnp.where(is_grp, jnp.exp(gl - gmax), 0.0), axis=1, keepdims=True)
    in_grp = (lane < N_EXPERTS) & ((lane >> 3).astype(F32) == g_idx)
    el = jnp.where(in_grp, logits, NEG)
    v1 = jnp.max(el, axis=1, keepdims=True)
    i1 = first(el == v1)
    el2 = jnp.where(lane_f == i1, NEG, el)
    v2 = jnp.max(el2, axis=1, keepdims=True)
    i2 = first(el2 == v2)
    e2 = jnp.exp(v2 - v1)
    w1 = p_grp / (1.0 + e2)
    rw_ref[...] = jnp.where(lane_f == i1, w1, 0.0) + jnp.where(lane_f == i2, w1 * e2, 0.0)


def _mix_route(x2, oa, ob, gm, wg, pa, pb, wo, gf, wr, tm):
    n = x2.shape[0]
    row = lambda c: pl.BlockSpec((tm, c), lambda i: (i, 0))
    full = lambda r, c: pl.BlockSpec((r, c), lambda i: (0, 0))
    return pl.pallas_call(
        _mix_route_kernel,
        grid=(n // tm,),
        in_specs=[row(D_MODEL), row(W_MIX), row(W_MIX), full(1, D_MODEL), full(D_MODEL, 2 * D_MODEL),
                  full(W_MIX, D_MODEL), full(W_MIX, D_MODEL), full(D_MODEL, D_MODEL), full(1, D_MODEL),
                  full(D_MODEL, LANES)],
        out_specs=[row(D_MODEL), row(D_MODEL), row(LANES)],
        out_shape=[jax.ShapeDtypeStruct((n, D_MODEL), F32), jax.ShapeDtypeStruct((n, D_MODEL), BF16),
                   jax.ShapeDtypeStruct((n, LANES), F32)],
        compiler_params=_params("arbitrary"),
        name="mix_route",
    )(x2, oa, ob, gm, wg, pa, pb, wo, gf, wr)


def _moe_kernel(hn_ref, rw_ref, x1_ref, wgu_ref, wd_ref, gn_ref, y_ref, acc):
    e = pl.program_id(1)

    @pl.when(e == 0)
    def _():
        acc[...] = jnp.zeros_like(acc)

    h = jnp.dot(hn_ref[...], wgu_ref[0], preferred_element_type=F32)
    gate = h[:, :D_EXPERT]
    act = gate * _sigmoid(gate) * h[:, D_EXPERT:]
    yb = _dot(act, wd_ref[0])
    rw = rw_ref[...]
    lane = lax.broadcasted_iota(jnp.int32, rw.shape, 1)
    w_col = jnp.sum(jnp.where(lane == e, rw, 0.0), axis=1, keepdims=True)
    acc[...] += yb * w_col

    @pl.when(e == N_EXPERTS - 1)
    def _():
        y_ref[...] = _rms(x1_ref[...] + acc[...], gn_ref[...])


def _moe(hn, rw, x1, wgu, wd, gn, tm):
    n = hn.shape[0]
    row = lambda c: pl.BlockSpec((tm, c), lambda i, e: (i, 0))
    return pl.pallas_call(
        _moe_kernel,
        grid=(n // tm, N_EXPERTS),
        in_specs=[row(D_MODEL), row(LANES), row(D_MODEL),
                  pl.BlockSpec((1, D_MODEL, 2 * D_EXPERT), lambda i, e: (e, 0, 0)),
                  pl.BlockSpec((1, D_EXPERT, D_MODEL), lambda i, e: (e, 0, 0)),
                  pl.BlockSpec((1, D_MODEL), lambda i, e: (0, 0))],
        out_specs=row(D_MODEL),
        out_shape=jax.ShapeDtypeStruct((n, D_MODEL), F32),
        scratch_shapes=[pltpu.VMEM((tm, D_MODEL), F32)],
        compiler_params=_params("arbitrary", "arbitrary"),
        name="moe_dense",
    )(hn, rw, x1, wgu, wd, gn)


def _pad_rw_cols(m):
    z = jnp.zeros(m.shape[:-1] + (LANES - LORA_W,), m.dtype)
    c = 3 * W_MIX
    return jnp.concatenate([m[..., :c], m[..., c:c + LORA_W], z, m[..., c + LORA_W:c + 2 * LORA_W], z,
                            m[..., c + 2 * LORA_W:]], axis=-1)


def _unpad_rw_cols(m):
    c = 3 * W_MIX
    return jnp.concatenate([m[..., :c], m[..., c:c + LORA_W], m[..., c + LANES:c + LANES + LORA_W],
                            m[..., c + 2 * LANES:]], axis=-1)


def _pad_rows(m, rows):
    return jnp.concatenate([m, jnp.zeros((rows - m.shape[0],) + m.shape[1:], m.dtype)], axis=0)


def _prep_weights(l, norm_mix, w_in, b_f, mu_rw, w0, w2, a0, a2, g2, k_k, k_a, r_k, lnx_w, lnx_b,
                  p_a, p_b, w_o, norm_ffn, w_grp, w_exp, we_gate, we_up, we_down):
    w = w_in[l]
    w_fl = jnp.concatenate([w[:, 3 * W_MIX:FOX_COLS], jnp.zeros((D_MODEL, LANES - N_HEADS), F32)], axis=1)
    w_rw = _pad_rw_cols(w[:, FOX_COLS:FOX_COLS + RW_COLS])
    row = lambda vct: vct.reshape(1, -1)
    head_id = jnp.arange(W_MIX) // D_HEAD
    return dict(
        gm=row(norm_mix[l]),
        w_a=jnp.concatenate([w[:, :3 * W_MIX], w_fl, w_rw], axis=1).astype(BF16),
        bf=jnp.concatenate([b_f[l], jnp.zeros((LANES - N_HEADS,), F32)]).reshape(1, LANES),
        wg=w[:, FOX_COLS + RW_COLS:].astype(BF16),
        vecs=(row(_pad_rw_cols(mu_rw[l])), row(w0[l]), row(a0[l]), row(k_k[l]), row(k_a[l]),
              row(r_k[l].reshape(-1)), row(lnx_w[l]), row(lnx_b[l])),
        w2p=_pad_rows(w2[l], LANES).astype(BF16),
        a2p=_pad_rows(a2[l], LANES).astype(BF16),
        g2=g2[l].astype(BF16),
        e_heads=(head_id[:, None] == head_id[None, :]).astype(BF16),
        pa=p_a[l].astype(BF16), pb=p_b[l].astype(BF16), wo=w_o[l].astype(BF16),
        gf=row(norm_ffn[l]),
        wr=jnp.concatenate([w_exp[l], w_grp[l], jnp.zeros((D_MODEL, LANES - N_EXPERTS - N_GROUPS), F32)],
                           axis=1).astype(BF16),
        wgu=jnp.concatenate([we_gate[l], we_up[l]], axis=-1).astype(BF16),
        wd=we_down[l].astype(BF16),
    )


def _layer(x, past, shift0, wkv0, page_table, p, gn, tiles):
    b, t, _ = x.shape
    n = b * t
    tm, tq, tt, tm_moe = tiles
    x2 = x.reshape(n, D_MODEL)
    qb, k, v, kb, vb, lf, rw = _inproj(x2, p["gm"], p["w_a"], p["bf"], tm)
    to3 = lambda m: m.reshape(b, t, m.shape[-1])
    if past is None:
        c = _cumsum(to3(lf), tq)[:, :, :N_HEADS]
        n_g = W_MIX // GROUP_W
        cq = c.reshape(b, t, n_g, HEADS_PER_GROUP).transpose(0, 2, 1, 3)
        ck = cq.transpose(0, 1, 3, 2)
        o_a = _fox_prompt(to3(qb), to3(kb), to3(vb), cq, ck, tq)
        t_pad = t
        u3 = to3(rw)
    else:
        pool_k, pool_v, pool_lf_t = past
        pad_keys = lambda m: jnp.concatenate([to3(m), jnp.zeros((b, PAGE - t, W_MIX), BF16)], axis=1)
        lfn_t = jnp.concatenate([to3(lf)[:, :, :N_HEADS].transpose(0, 2, 1),
                                 jnp.zeros((b, N_HEADS, PAGE - t), F32)], axis=2)
        o_a = _fox_sample(page_table, to3(qb).astype(F32), pad_keys(kb), pad_keys(vb), lfn_t,
                          pool_k, pool_v, pool_lf_t, t).astype(BF16)
        t_pad = tt
        u3 = jnp.concatenate([to3(rw), jnp.zeros((b, t_pad - t, RW_PAD), F32)], axis=1)
    o_b, wkv_t = _rwkv(u3, shift0, jnp.swapaxes(wkv0, -1, -2), p["vecs"], p["w2p"], p["a2p"], p["g2"],
                       p["e_heads"], tt, t)
    o_b = o_b[:, :t].reshape(n, W_MIX)
    wkv_new = jnp.swapaxes(wkv_t, -1, -2)
    x1, hn, route = _mix_route(x2, o_a.reshape(n, W_MIX), o_b, p["gm"], p["wg"], p["pa"], p["pb"], p["wo"],
                               p["gf"], p["wr"], tm)
    y = _moe(hn, route, x1, p["wgu"], p["wd"], gn, tm_moe)
    shift_new = _unpad_rw_cols(to3(rw)[:, t - 1, :])
    return (y.reshape(b, t, D_MODEL), k.reshape(b, t, N_HEADS, D_HEAD), v.reshape(b, t, N_HEADS, D_HEAD),
            to3(lf)[:, :, :N_HEADS], wkv_new, shift_new)


def kernel(x_prompt, x_sample, cache_k, cache_v, cache_logf, page_table, state_wkv, state_shift, norm_mix, w_in, b_f, mu_rw, w0, w2, a0, a2, g2, k_k, k_a, r_k, lnx_w, lnx_b, p_a, p_b, w_o, norm_ffn, w_grp, w_exp, we_gate, we_up, we_down, norm_final):
    depth = w_in.shape[0]
    assert depth == 1, "final norm is fused into the layer's last kernel"
    bp, tp, _ = x_prompt.shape
    bs, ts, _ = x_sample.shape
    n_phys = cache_k.shape[1]
    gn = norm_final.reshape(1, D_MODEL)
    l = 0
    p = _prep_weights(l, norm_mix, w_in, b_f, mu_rw, w0, w2, a0, a2, g2, k_k, k_a, r_k, lnx_w, lnx_b,
                      p_a, p_b, w_o, norm_ffn, w_grp, w_exp, we_gate, we_up, we_down)
    tq = min(512, tp)
    prompt = _layer(x_prompt, None, jnp.zeros((bp, 1, RW_PAD), F32),
                    jnp.zeros((bp, N_HEADS, D_HEAD, D_HEAD), F32), None, p, gn,
                    (min(512, bp * tp), tq, min(256, tp), min(1024, bp * tp)))
    past = (cache_k[l].transpose(0, 2, 3, 1).reshape(n_phys, W_MIX, PAGE),
            cache_v[l].transpose(0, 2, 3, 1).reshape(n_phys, W_MIX, PAGE),
            cache_logf[l].transpose(0, 2, 1))
    sample = _layer(x_sample, past, _pad_rw_cols(state_shift[l])[:, None, :], state_wkv[l], page_table, p, gn,
                    (bs * ts, None, CHUNK, bs * ts))
    outs = []
    for y, k, v, lf, wkv, sh in (prompt, sample):
        outs.append((y, k[None], v[None], lf[None], wkv[None], sh[None]))
    (yp, kp, vp, lp, wp, sp), (ys, ks, vs, ls, ws, ss) = outs
    return (yp, ys, kp, vp, lp, wp, sp, ks, vs, ls, ws, ss)
```

```python
import functools

import numpy as np
import jax
import jax.numpy as jnp
from jax import lax
from jax.experimental import pallas as pl
from jax.experimental.pallas import tpu as pltpu

F32 = jnp.float32
BF16 = jnp.bfloat16

D_MODEL = 1024
N_HEADS = 8
D_HEAD = 64
W_MIX = N_HEADS * D_HEAD
LORA_W = 64
LORA_G = 128
RW_COLS = 3 * W_MIX + 2 * LORA_W + LORA_G
RW_PAD = 3 * W_MIX + 3 * 128
FOX_COLS = 3 * W_MIX + N_HEADS
N_EXPERTS = 32
EXPERTS_PER_GROUP = 8
N_GROUPS = 4
D_EXPERT = 256
PAGE = 128
MAX_PAGES_PER_STEP = 8
NORM_EPS = 1e-6
GN_EPS = 64e-5
QK_SCALE = D_HEAD ** -0.5
LANES = 128
CHUNK = 64
NEG = -1e30
VMEM_LIMIT = 56 * 1024 * 1024


def _dot(a, b):
    return jnp.dot(a.astype(BF16), b.astype(BF16), preferred_element_type=F32)


def _bdot(a, b):
    return jnp.dot(a, b, preferred_element_type=F32)


def _bdot_nt(a, b):
    return lax.dot_general(a, b, (((1,), (1,)), ((), ())), preferred_element_type=F32)


def _each(f, *lists):
    return [f(*xs) for xs in zip(*lists)]


def _split3(x):
    hi = x.astype(BF16)
    r1 = x - hi.astype(F32)
    mid = r1.astype(BF16)
    lo = (r1 - mid.astype(F32)).astype(BF16)
    return hi, mid, lo


def _dot_sel_lhs(sel, x):
    hi, mid, lo = _split3(x)
    d = lambda p: jnp.dot(sel, p, preferred_element_type=F32)
    return d(hi) + d(mid) + d(lo)


def _dot_sel_rhs(x, sel):
    hi, mid, lo = _split3(x)
    d = lambda p: jnp.dot(p, sel, preferred_element_type=F32)
    return d(hi) + d(mid) + d(lo)


def _sigmoid(x):
    return 1.0 / (1.0 + jnp.exp(-x))


def _log_sigmoid(x):
    return jnp.minimum(x, 0.0) - jnp.log(1.0 + jnp.exp(-jnp.abs(x)))


def _rms(x, g):
    return (x * lax.rsqrt(jnp.mean(x * x, axis=-1, keepdims=True) + NORM_EPS)) * g


def _params(*sem):
    return pltpu.CompilerParams(dimension_semantics=sem, vmem_limit_bytes=VMEM_LIMIT)


def _inproj_kernel(x_ref, g_ref, w_ref, wvt_ref, bf_ref, q_ref, k_ref, v_ref, kb_ref, vb_ref, lf_ref, rw_ref,
                   *maybe_vt_ref, q_scale):
    xb = _rms(x_ref[...], g_ref[...]).astype(BF16)
    d = lambda lo, hi: jnp.dot(xb, w_ref[:, lo:hi], preferred_element_type=F32)
    q_ref[...] = (d(0, W_MIX) * q_scale).astype(BF16)
    k = d(W_MIX, 2 * W_MIX)
    k_ref[...] = k
    kb_ref[...] = k.astype(BF16)
    v = d(2 * W_MIX, 3 * W_MIX)
    v_ref[...] = v
    vb_ref[...] = v.astype(BF16)
    lf_ref[...] = _log_sigmoid(d(3 * W_MIX, 3 * W_MIX + LANES) + bf_ref[...])
    rw_ref[...] = d(3 * W_MIX + LANES, 3 * W_MIX + LANES + RW_PAD)
    if maybe_vt_ref:
        maybe_vt_ref[0][0] = _bdot_nt(wvt_ref[...], xb).astype(BF16)


def _inproj(x2, g, w_a, w_vt, bf_pad, tm, q_scale, seq_for_vt=None):
    n = x2.shape[0]
    wcols = w_a.shape[1]
    row = lambda c: pl.BlockSpec((tm, c), lambda i: (i, 0))
    full = lambda r, c: pl.BlockSpec((r, c), lambda i: (0, 0))
    out_specs = [row(W_MIX), row(W_MIX), row(W_MIX), row(W_MIX), row(W_MIX), row(LANES), row(RW_PAD)]
    out_shape = [jax.ShapeDtypeStruct((n, W_MIX), BF16), jax.ShapeDtypeStruct((n, W_MIX), F32),
                 jax.ShapeDtypeStruct((n, W_MIX), F32), jax.ShapeDtypeStruct((n, W_MIX), BF16),
                 jax.ShapeDtypeStruct((n, W_MIX), BF16), jax.ShapeDtypeStruct((n, LANES), F32),
                 jax.ShapeDtypeStruct((n, RW_PAD), F32)]
    if seq_for_vt is not None:
        per_seq = seq_for_vt // tm
        out_specs.append(pl.BlockSpec((1, W_MIX, tm), lambda i: (i // per_seq, 0, i % per_seq)))
        out_shape.append(jax.ShapeDtypeStruct((n // seq_for_vt, W_MIX, seq_for_vt), BF16))
    return pl.pallas_call(
        functools.partial(_inproj_kernel, q_scale=q_scale),
        grid=(n // tm,),
        in_specs=[row(D_MODEL), full(1, D_MODEL), full(D_MODEL, wcols), full(W_MIX, D_MODEL), full(1, LANES)],
        out_specs=out_specs,
        out_shape=out_shape,
        compiler_params=_params("arbitrary"),
        name="inproj",
    )(x2, g, w_a, w_vt, bf_pad)


PAIR_W = 2 * D_HEAD
N_PAIRS = N_HEADS // 2
QK_W = PAIR_W + LANES
PAIRS_PER_STEP = 2
HEADS_PER_STEP = 2 * PAIRS_PER_STEP
KEY_BLK = 32
LOG2E = 1.4426950408889634


def _bias_tables():
    pq = np.zeros((3, LANES, N_PAIRS * LANES), np.float32)
    pk = np.zeros((3, LANES, N_PAIRS * LANES), np.float32)
    one_q = np.zeros((1, N_PAIRS * LANES), np.float32)
    one_k = np.zeros((1, N_PAIRS * LANES), np.float32)
    for h in range(N_HEADS):
        base = LANES * (h // 2)
        for piece in range(3):
            pq[piece, h, base + 3 * (h % 2) + piece] = 1.0
            pk[piece, h, base + 6 + 3 * (h % 2) + piece] = -1.0
    for g in range(N_PAIRS):
        one_q[0, LANES * g + 6:LANES * g + 12] = 1.0
        one_k[0, LANES * g + 0:LANES * g + 6] = 1.0
    return (jnp.asarray(pq, BF16), jnp.asarray(pk, BF16), jnp.asarray(one_q), jnp.asarray(one_k))


def _qk_aug_kernel(lf_ref, q_ref, k_ref, pq_ref, pk_ref, oq_ref, ok_ref, qa_ref, ka_ref, carry):
    @pl.when(pl.program_id(1) == 0)
    def _():
        carry[...] = jnp.zeros_like(carry)

    lf = lf_ref[0] * LOG2E
    tm = lf.shape[0]
    row = lax.broadcasted_iota(jnp.int32, (tm, tm), 0)
    col = lax.broadcasted_iota(jnp.int32, (tm, tm), 1)
    c = _dot_sel_lhs((col <= row).astype(BF16), lf) + carry[...]
    carry[...] = c[tm - 1:tm, :]
    pieces = _split3(c)
    aug_q = oq_ref[...] + sum(_bdot(pc, pq_ref[i]) for i, pc in enumerate(pieces))
    aug_k = ok_ref[...] + sum(_bdot(pc, pk_ref[i]) for i, pc in enumerate(pieces))
    q = q_ref[0]
    k = k_ref[0]
    for g in range(N_PAIRS):
        qa_ref[0, :, g * QK_W:g * QK_W + PAIR_W] = q[:, g * PAIR_W:(g + 1) * PAIR_W]
        qa_ref[0, :, g * QK_W + PAIR_W:(g + 1) * QK_W] = aug_q[:, g * LANES:(g + 1) * LANES].astype(BF16)
        ka_ref[0, :, g * QK_W:g * QK_W + PAIR_W] = k[:, g * PAIR_W:(g + 1) * PAIR_W]
        ka_ref[0, :, g * QK_W + PAIR_W:(g + 1) * QK_W] = aug_k[:, g * LANES:(g + 1) * LANES].astype(BF16)


def _qk_aug(lf3, qb, kb, tm):
    b, t, _ = lf3.shape
    pq, pk, one_q, one_k = _bias_tables()
    blk = lambda c: pl.BlockSpec((1, tm, c), lambda i, j: (i, j, 0))
    cst = lambda a: pl.BlockSpec(a.shape, lambda i, j: (0,) * a.ndim)
    return pl.pallas_call(
        _qk_aug_kernel,
        grid=(b, t // tm),
        in_specs=[blk(LANES), blk(W_MIX), blk(W_MIX), cst(pq), cst(pk), cst(one_q), cst(one_k)],
        out_specs=[blk(N_PAIRS * QK_W), blk(N_PAIRS * QK_W)],
        out_shape=[jax.ShapeDtypeStruct((b, t, N_PAIRS * QK_W), BF16)] * 2,
        scratch_shapes=[pltpu.VMEM((1, LANES), F32)],
        compiler_params=_params("arbitrary", "arbitrary"),
        name="logf_cumsum_qk_aug",
    )(lf3, qb, kb, pq, pk, one_q, one_k)


def _fox_prompt_kernel(qi_ref, kj_ref, q_ref, k_ref, vt_ref, o_ref, qm, m_s, l_s, acc, s_scr, p_scr,
                       *, tq, tk):
    step = pl.program_id(2)
    i = qi_ref[step]
    j = kj_ref[step]

    @pl.when(j == 0)
    def _():
        lane = lax.broadcasted_iota(jnp.int32, (tq, QK_W), 1)
        off = lane - PAIR_W
        for hh in range(HEADS_PER_STEP):
            pr, h = divmod(hh, 2)
            q = q_ref[0, :, pr * QK_W:(pr + 1) * QK_W].astype(F32)
            keep = ((lane >= h * D_HEAD) & (lane < (h + 1) * D_HEAD)) \
                | ((off >= 3 * h) & (off < 3 * h + 3)) | ((off >= 6 + 3 * h) & (off < 9 + 3 * h))
            qm[hh] = jnp.where(keep, q, 0.0).astype(BF16)
        m_s[...] = jnp.full_like(m_s, NEG)
        l_s[...] = jnp.zeros_like(l_s)
        acc[...] = jnp.zeros_like(acc)

    def block(diag):
        key = lax.broadcasted_iota(jnp.int32, (KEY_BLK, tq), 0)
        qry = lax.broadcasted_iota(jnp.int32, (KEY_BLK, tq), 1)
        n_kb = tk // KEY_BLK
        groups = lambda t: [t[8 * r:8 * (r + 1), :] for r in range(KEY_BLK // 8)]

        def scores(h, kb):
            s = s_scr[h, kb * KEY_BLK:(kb + 1) * KEY_BLK, :]
            if diag:
                s = jnp.where(key + kb * KEY_BLK <= qry, s, NEG)
            return s

        for h in range(HEADS_PER_STEP):
            pr = h // 2
            s_scr[h] = _bdot_nt(k_ref[0, :, pr * QK_W:(pr + 1) * QK_W], qm[h])
        for h in range(HEADS_PER_STEP):
            m8 = jnp.full((8, tq), NEG, F32)
            for kb in range(n_kb):
                for g in groups(scores(h, kb)):
                    m8 = jnp.maximum(m8, g)
            m_prev = m_s[h]
            m_new = jnp.maximum(m_prev, jnp.max(m8, axis=0, keepdims=True))
            alpha = jnp.exp2(m_prev - m_new)
            l8 = jnp.zeros((8, tq), F32)
            for kb in range(n_kb):
                p = jnp.exp2(scores(h, kb) - m_new)
                for g in groups(p):
                    l8 = l8 + g
                p_scr[h, kb * KEY_BLK:(kb + 1) * KEY_BLK, :] = p.astype(BF16)
            l_s[h] = alpha * l_s[h] + jnp.sum(l8, axis=0, keepdims=True)
            m_s[h] = m_new
            vt = vt_ref[0, h * D_HEAD:(h + 1) * D_HEAD, :]
            acc[h] = alpha * acc[h] + _bdot(vt, p_scr[h])

    @pl.when(j < i)
    def _():
        block(False)

    @pl.when(j == i)
    def _():
        block(True)
        o_t = jnp.concatenate([acc[h] / l_s[h] for h in range(HEADS_PER_STEP)], axis=0)
        o_ref[0] = o_t.T.astype(o_ref.dtype)


def _fox_prompt(qa, ka, vt, tq):
    b, _, t = vt.shape
    tk = tq
    nq = t // tq
    qi = np.array([i for i in range(nq) for _ in range(i + 1)], np.int32)
    kj = np.array([j for i in range(nq) for j in range(i + 1)], np.int32)
    kern = functools.partial(_fox_prompt_kernel, tq=tq, tk=tk)
    grid_spec = pltpu.PrefetchScalarGridSpec(
        num_scalar_prefetch=2,
        grid=(b, N_PAIRS // PAIRS_PER_STEP, len(qi)),
        in_specs=[
            pl.BlockSpec((1, tq, PAIRS_PER_STEP * QK_W), lambda bi, g, s, qi_, kj_: (bi, qi_[s], g)),
            pl.BlockSpec((1, tk, PAIRS_PER_STEP * QK_W), lambda bi, g, s, qi_, kj_: (bi, kj_[s], g)),
            pl.BlockSpec((1, PAIRS_PER_STEP * PAIR_W, tk), lambda bi, g, s, qi_, kj_: (bi, g, kj_[s])),
        ],
        out_specs=pl.BlockSpec((1, tq, PAIRS_PER_STEP * PAIR_W), lambda bi, g, s, qi_, kj_: (bi, qi_[s], g)),
        scratch_shapes=[pltpu.VMEM((HEADS_PER_STEP, tq, QK_W), BF16), pltpu.VMEM((HEADS_PER_STEP, 1, tq), F32),
                        pltpu.VMEM((HEADS_PER_STEP, 1, tq), F32), pltpu.VMEM((HEADS_PER_STEP, D_HEAD, tq), F32),
                        pltpu.VMEM((HEADS_PER_STEP, tk, tq), F32), pltpu.VMEM((HEADS_PER_STEP, tk, tq), BF16)],
    )
    return pl.pallas_call(
        kern,
        grid_spec=grid_spec,
        out_shape=jax.ShapeDtypeStruct((b, t, W_MIX), BF16),
        compiler_params=_params("arbitrary", "arbitrary", "arbitrary"),
        name="fox_prompt",
    )(jnp.asarray(qi), jnp.asarray(kj), qa, ka, vt)


def _fox_sample_kernel(pt_ref, q_ref, kn_ref, vn_ref, lfn_ref, *rest, n_new, n_steps, n_grp):
    del pt_ref
    kp_refs, vp_refs, lfp_refs = rest[:n_grp], rest[n_grp:2 * n_grp], rest[2 * n_grp:3 * n_grp]
    o_ref, qm, m_s, l_s, acc, carry, cq_s = rest[3 * n_grp:]
    j = pl.program_id(1)
    rows = n_new * N_HEADS
    lane8 = lax.broadcasted_iota(jnp.int32, (N_HEADS, W_MIX), 1)
    head8 = lax.broadcasted_iota(jnp.int32, (N_HEADS, W_MIX), 0)
    head_sel = (lane8 >> 6) == head8
    kr = lax.broadcasted_iota(jnp.int32, (PAGE, PAGE), 0)
    kc = lax.broadcasted_iota(jnp.int32, (PAGE, PAGE), 1)
    tile = lambda x: jnp.concatenate([x] * n_new, axis=0)

    def update(s, pv):
        m_prev = m_s[...]
        m_new = jnp.maximum(m_prev, jnp.max(s, axis=1, keepdims=True))
        alpha = jnp.exp(m_prev - m_new)
        p = jnp.exp(s - m_new)
        l_s[...] = alpha * l_s[...] + jnp.sum(p, axis=1, keepdims=True)
        acc[...] = alpha * acc[...] + pv(p.astype(BF16))
        m_s[...] = m_new

    @pl.when(j == 0)
    def _():
        q = q_ref[0].astype(F32)
        qrows = [jnp.where(head_sel, jnp.broadcast_to(q[t:t + 1, :], (N_HEADS, W_MIX)), 0.0)
                 for t in range(n_new)]
        qm[...] = jnp.concatenate(qrows, axis=0).astype(BF16)
        m_s[...] = jnp.full_like(m_s, NEG)
        l_s[...] = jnp.zeros_like(l_s)
        acc[...] = jnp.zeros_like(acc)
        carry[...] = jnp.zeros_like(carry)
        cn = _dot_sel_rhs(lfn_ref[0], (kr <= kc).astype(BF16))
        for t in range(n_new):
            cq_s[t * N_HEADS:(t + 1) * N_HEADS, :] = cn[:, t:t + 1]
        s = lax.dot_general(qm[...], kn_ref[0], (((1,), (1,)), ((), ())), preferred_element_type=F32)
        s = s + (cq_s[...] - tile(cn))
        tok = lax.broadcasted_iota(jnp.int32, (rows, PAGE), 0) >> 3
        key = lax.broadcasted_iota(jnp.int32, (rows, PAGE), 1)
        s = jnp.where(key <= tok, s, NEG)
        update(s, lambda p: _bdot(p, vn_ref[0]))

    @pl.when(j > 0)
    def _():
        lfs = [ref[0] for ref in lfp_refs]
        suffix = _dot_sel_rhs(jnp.concatenate(lfs, axis=0), (kr > kc).astype(BF16))
        later = carry[...]
        cq = cq_s[...]
        parts = []
        for g in range(n_grp):
            bias8 = suffix[g * N_HEADS:(g + 1) * N_HEADS, :] + later
            s_g = _bdot(qm[...], kp_refs[g][0].astype(BF16))
            parts.append(s_g + (cq + tile(bias8)))
            later = later + jnp.sum(lfs[g], axis=1, keepdims=True)
        carry[...] = later
        s = jnp.concatenate(parts, axis=1)

        def pv(p):
            out = _bdot_nt(p[:, 0:PAGE], vp_refs[0][0].astype(BF16))
            for g in range(1, n_grp):
                out = out + _bdot_nt(p[:, g * PAGE:(g + 1) * PAGE], vp_refs[g][0].astype(BF16))
            return out

        update(s, pv)

    @pl.when(j == n_steps - 1)
    def _():
        o = acc[...] / l_s[...]
        for t in range(n_new):
            ot = jnp.where(head_sel, o[t * N_HEADS:(t + 1) * N_HEADS, :], 0.0)
            o_ref[0, t:t + 1, :] = jnp.sum(ot, axis=0, keepdims=True).astype(o_ref.dtype)


def _fox_sample(page_table, qb, kn, vn, lfn_t, pool_k, pool_v, pool_lf_t, n_new):
    r, n_pages = page_table.shape
    rows = n_new * N_HEADS
    n_grp = max(g for g in range(1, MAX_PAGES_PER_STEP + 1) if n_pages % g == 0)
    n_steps = n_pages // n_grp + 1
    kern = functools.partial(_fox_sample_kernel, n_new=n_new, n_steps=n_steps, n_grp=n_grp)

    def page(g):
        return lambda ri, j, pt: (pt[ri, n_pages - 1 - ((jnp.maximum(j, 1) - 1) * n_grp + g)], 0, 0)

    req = lambda ri, j, pt: (ri, 0, 0)
    grid_spec = pltpu.PrefetchScalarGridSpec(
        num_scalar_prefetch=1,
        grid=(r, n_steps),
        in_specs=[
            pl.BlockSpec((1, n_new, W_MIX), req),
            pl.BlockSpec((1, PAGE, W_MIX), req),
            pl.BlockSpec((1, PAGE, W_MIX), req),
            pl.BlockSpec((1, N_HEADS, PAGE), req),
        ] + [pl.BlockSpec((1, W_MIX, PAGE), page(g)) for g in range(n_grp)]
          + [pl.BlockSpec((1, W_MIX, PAGE), page(g)) for g in range(n_grp)]
          + [pl.BlockSpec((1, N_HEADS, PAGE), page(g)) for g in range(n_grp)],
        out_specs=pl.BlockSpec((1, n_new, W_MIX), req),
        scratch_shapes=[pltpu.VMEM((rows, W_MIX), BF16), pltpu.VMEM((rows, 1), F32),
                        pltpu.VMEM((rows, 1), F32), pltpu.VMEM((rows, W_MIX), F32),
                        pltpu.VMEM((N_HEADS, 1), F32), pltpu.VMEM((rows, 1), F32)],
    )
    return pl.pallas_call(
        kern,
        grid_spec=grid_spec,
        out_shape=jax.ShapeDtypeStruct((r, n_new, W_MIX), F32),
        compiler_params=_params("arbitrary", "arbitrary"),
        name="fox_sample",
    )(page_table, qb, kn, vn, lfn_t, *([pool_k] * n_grp), *([pool_v] * n_grp), *([pool_lf_t] * n_grp))


HEADS_PER_ITER = 2


def _rwkv_kernel(u_ref, sh_ref, s0_ref, mu_ref, w0_ref, w2_ref, a0_ref, a2_ref, g2_ref, kk_ref, ka_ref,
                 rk_ref, lw_ref, lb_ref, e_ref, msk_ref, o_ref, sout_ref,
                 carry, st, kt_s, rt_s, kh_s, bh_s, k0_s, r0_s, khp_s, bhp_s, v_s, gt_s, o_s,
                 *, tt, t_real, t_total):
    ti = pl.program_id(1)
    nch = tt // CHUNK

    @pl.when(ti == 0)
    def _():
        carry[...] = sh_ref[0]
        st[...] = s0_ref[0]

    u = u_ref[0]
    row1 = lax.broadcasted_iota(jnp.int32, (tt, 1), 0)
    u_prev = jnp.where(row1 == 0, carry[...], pltpu.roll(u, 1, 0))
    carry[...] = u[tt - 1:tt, :]
    us = u + mu_ref[...] * (u_prev - u)
    r = us[:, 0:W_MIX]
    k = us[:, W_MIX:2 * W_MIX]
    v = us[:, 2 * W_MIX:3 * W_MIX]
    wl = us[:, 3 * W_MIX:3 * W_MIX + LANES]
    al = us[:, 3 * W_MIX + LANES:3 * W_MIX + 2 * LANES]
    gl = us[:, 3 * W_MIX + 2 * LANES:3 * W_MIX + 3 * LANES]
    z = w0_ref[...] + _dot(jnp.tanh(wl), w2_ref[...])
    w = -(jnp.maximum(-z, 0.0) + jnp.log(1.0 + jnp.exp(-jnp.abs(z)))) - 0.5
    logd = -jnp.exp(w)
    a = _sigmoid(a0_ref[...] + _dot(al, a2_ref[...]))
    g = _dot(_sigmoid(gl), g2_ref[...])
    e_heads = e_ref[...]
    kk = k * kk_ref[...]
    kk = kk / jnp.maximum(jnp.sqrt(_dot(kk * kk, e_heads)), 1e-12)
    k_mod = k * (1.0 + (a - 1.0) * ka_ref[...])
    b = kk * a
    bonus = _dot(r * k_mod * rk_ref[...], e_heads) * v
    if t_real < t_total:
        valid = (ti * tt + row1) < t_real
        zero = lambda t: jnp.where(valid, t, 0.0)
        kk, b, k_mod, v, logd = zero(kk), zero(b), zero(k_mod), zero(v), zero(logd)

    hi, mid, lo = _split3(logd)
    sel3 = lambda m: (_bdot(m, hi) + _bdot(m, mid) + _bdot(m, lo))
    cum = sel3(msk_ref[1].astype(BF16))
    ref = sel3(msk_ref[7].astype(BF16))
    tot = sel3(msk_ref[8].astype(BF16))
    cum_prev = cum - logd
    e_inv = jnp.exp(ref - cum)
    e_tot = jnp.exp(tot - cum)
    wide = dict(kt=kk * jnp.exp(cum_prev - ref), rt=r * jnp.exp(cum - ref), kh=k_mod * e_inv, bh=b * e_inv,
                k0=kk * jnp.exp(cum_prev), r0=r * jnp.exp(cum), khp=k_mod * e_tot, bhp=b * e_tot, v=v,
                gt=jnp.exp(tot))
    dst = dict(kt=kt_s, rt=rt_s, kh=kh_s, bh=bh_s, k0=k0_s, r0=r0_s, khp=khp_s, bhp=bhp_s, v=v_s, gt=gt_s)
    for name, val in wide.items():
        for h in range(N_HEADS):
            dst[name][h] = val[:, h * D_HEAD:(h + 1) * D_HEAD]

    def head_body(p, _):
        hs = [p * HEADS_PER_ITER + i for i in range(HEADS_PER_ITER)]
        ld = lambda s: [s[h] for h in hs]
        ldb = lambda s: [s[h].astype(BF16) for h in hs]
        kt, rt, kh, bh = ldb(kt_s), ldb(rt_s), ldb(kh_s), ldb(bh_s)
        strict = msk_ref[0] != 0.0
        incl = msk_ref[1] != 0.0
        a_kk = _each(lambda x, y: jnp.where(strict, _bdot_nt(x, y), 0.0).astype(BF16), kt, kh)
        a_kb = _each(lambda x, y: jnp.where(strict, _bdot_nt(x, y), 0.0), kt, bh)
        a_rk = _each(lambda x, y: jnp.where(incl, _bdot_nt(x, y), 0.0).astype(BF16), rt, kh)
        a_rb = _each(lambda x, y: jnp.where(incl, _bdot_nt(x, y), 0.0).astype(BF16), rt, bh)
        x = _each(lambda t: (-(t * msk_ref[2])).astype(BF16), a_kb)
        x2 = _each(lambda t: _bdot(t, t).astype(BF16), x)
        pm = _each(lambda t: msk_ref[6] + t.astype(F32), x)
        pm = _each(lambda q, t: q + _bdot(q.astype(BF16), t), pm, x2)
        x4 = _each(lambda t: _bdot(t, t).astype(BF16), x2)
        pm = _each(lambda q, t: q + _bdot(q.astype(BF16), t), pm, x4)
        for lvl in (3, 4, 5):
            pb = _each(lambda q: q.astype(BF16), pm)
            a_off = _each(lambda t: (t * msk_ref[lvl]).astype(BF16), a_kb)
            pa = _each(lambda q, t: _bdot(q, t).astype(BF16), pb, a_off)
            pm = _each(lambda q, s, qb: q - _bdot(s, qb), pm, pa, pb)
        t_inv = _each(lambda q: q.astype(BF16), pm)
        vb = ldb(v_s)
        w1 = _each(lambda t, y: _bdot(t, y), t_inv, ldb(k0_s))
        av = _each(lambda t, y: _bdot(t, y).astype(BF16), a_kk, vb)
        u1 = _each(lambda t, y: _bdot(t, y), t_inv, av)
        w1b = _each(lambda t: t.astype(BF16), w1)
        u1b = _each(lambda t: t.astype(BF16), u1)
        q_t = _each(lambda r0, t, y: (r0 - _bdot(t, y)).astype(BF16), ld(r0_s), a_rb, w1b)
        o1 = _each(lambda t, y, t2, y2: _bdot(t, y) - _bdot(t2, y2), a_rk, vb, a_rb, u1b)
        khp, bhp, gt = ld(khp_s), ld(bhp_s), ld(gt_s)
        for c in range(nch):
            rows = slice(c * CHUNK, (c + 1) * CHUNK)
            bt = _each(lambda t: t[rows].T.astype(BF16), bhp)
            kt_ = _each(lambda t: t[rows].T.astype(BF16), khp)
            gcol = _each(lambda t: t[rows].T[:, 0:1], gt)
            m_low = _each(lambda t, y: _bdot(t, y[rows]).astype(BF16), bt, w1b)
            n1 = _each(lambda t, y, t2, y2: _bdot(t, y[rows]) - _bdot(t2, y2[rows]), kt_, vb, bt, u1b)
            for i, h in enumerate(hs):
                s_t = st[h]
                s_hi = s_t.astype(BF16)
                s_lo = (s_t - s_hi.astype(F32)).astype(BF16)
                qc = q_t[i][rows]
                o_s[h, rows, :] = _bdot(qc, s_hi) + _bdot(qc, s_lo) + o1[i][rows]
                st[h] = gcol[i] * s_t - (_bdot(m_low[i], s_hi) + _bdot(m_low[i], s_lo)) + n1[i]
        return 0

    lax.fori_loop(0, N_HEADS // HEADS_PER_ITER, head_body, 0)

    o = jnp.concatenate([o_s[h] for h in range(N_HEADS)], axis=1)
    mean = _dot(o, e_heads) * (1.0 / D_HEAD)
    xc = o - mean
    var = _dot(xc * xc, e_heads) * (1.0 / D_HEAD)
    on = xc * lax.rsqrt(var + GN_EPS) * lw_ref[...] + lb_ref[...]
    o_ref[0] = ((on + bonus) * g).astype(o_ref.dtype)
    sout_ref[0] = st[...]


def _chunk_masks(tt):
    i = jnp.arange(tt)[:, None]
    j = jnp.arange(tt)[None, :]
    blk = lambda s: (i >> s) == (j >> s)
    same = blk(6)
    ms = [same & (j < i), same & (j <= i), blk(3), blk(4) & ~blk(3), blk(5) & ~blk(4), same & ~blk(5),
          i == j, same & ((j & (CHUNK - 1)) < CHUNK // 2), same]
    return jnp.stack(ms).astype(F32)


def _rwkv(u3, shift0, s0_t, vecs, w2p, a2p, g2, e_heads, tt, t_real):
    b, t, _ = u3.shape
    mu, w0, a0, k_k, k_a, r_k, lnw, lnb = vecs
    masks = _chunk_masks(tt)
    kern = functools.partial(_rwkv_kernel, tt=tt, t_real=t_real, t_total=t)
    vec = lambda c: pl.BlockSpec((1, c), lambda bi, ti: (0, 0))
    mat = lambda r_, c: pl.BlockSpec((r_, c), lambda bi, ti: (0, 0))
    wide = lambda: pltpu.VMEM((N_HEADS, tt, D_HEAD), F32)
    return pl.pallas_call(
        kern,
        grid=(b, t // tt),
        in_specs=[
            pl.BlockSpec((1, tt, RW_PAD), lambda bi, ti: (bi, ti, 0)),
            pl.BlockSpec((1, 1, RW_PAD), lambda bi, ti: (bi, 0, 0)),
            pl.BlockSpec((1, N_HEADS, D_HEAD, D_HEAD), lambda bi, ti: (bi, 0, 0, 0)),
            vec(RW_PAD), vec(W_MIX), mat(LANES, W_MIX), vec(W_MIX), mat(LANES, W_MIX), mat(LANES, W_MIX),
            vec(W_MIX), vec(W_MIX), vec(W_MIX), vec(W_MIX), vec(W_MIX), mat(W_MIX, W_MIX),
            pl.BlockSpec(masks.shape, lambda bi, ti: (0, 0, 0)),
        ],
        out_specs=[pl.BlockSpec((1, tt, W_MIX), lambda bi, ti: (bi, ti, 0)),
                   pl.BlockSpec((1, N_HEADS, D_HEAD, D_HEAD), lambda bi, ti: (bi, 0, 0, 0))],
        out_shape=[jax.ShapeDtypeStruct((b, t, W_MIX), BF16),
                   jax.ShapeDtypeStruct((b, N_HEADS, D_HEAD, D_HEAD), F32)],
        scratch_shapes=[pltpu.VMEM((1, RW_PAD), F32), pltpu.VMEM((N_HEADS, D_HEAD, D_HEAD), F32)]
                       + [wide() for _ in range(11)],
        compiler_params=_params("arbitrary", "arbitrary"),
        name="rwkv7_chunked",
    )(u3, shift0, s0_t, mu, w0, w2p, a0, a2p, g2, k_k, k_a, r_k, lnw, lnb, e_heads, masks)


def _mix_route_kernel(x_ref, oa_ref, ob_ref, gm_ref, wg_ref, pa_ref, pb_ref, wo_ref, gf_ref, wr_ref,
                      x1_ref, hn_ref, rw_ref):
    x = x_ref[...]
    xb = _rms(x, gm_ref[...]).astype(BF16)
    gates = _sigmoid(jnp.dot(xb, wg_ref[...], preferred_element_type=F32))
    mixed = (gates[:, :D_MODEL] * jnp.dot(oa_ref[...], pa_ref[...], preferred_element_type=F32)
             + gates[:, D_MODEL:] * jnp.dot(ob_ref[...], pb_ref[...], preferred_element_type=F32))
    x1 = x + _dot(mixed, wo_ref[...])
    x1_ref[...] = x1
    hb = _rms(x1, gf_ref[...]).astype(BF16)
    hn_ref[...] = hb
    logits = jnp.dot(hb, wr_ref[...], preferred_element_type=F32)
    tm = logits.shape[0]
    lane = lax.broadcasted_iota(jnp.int32, (tm, LANES), 1)
    lane_f = lane.astype(F32)
    first = lambda hit: jnp.min(jnp.where(hit, lane_f, 1e9), axis=1, keepdims=True)
    is_grp = (lane >= N_EXPERTS) & (lane < N_EXPERTS + N_GROUPS)
    gl = jnp.where(is_grp, logits, NEG)
    gmax = jnp.max(gl, axis=1, keepdims=True)
    g_idx = first(gl == gmax) - float(N_EXPERTS)
    p_grp = 1.0 / jnp.sum(jnp.where(is_grp, jnp.exp(gl - gmax), 0.0), axis=1, keepdims=True)
    in_grp = (lane < N_EXPERTS) & ((lane >> 3).astype(F32) == g_idx)
    el = jnp.where(in_grp, logits, NEG)
    v1 = jnp.max(el, axis=1, keepdims=True)
    i1 = first(el == v1)
    el2 = jnp.where(lane_f == i1, NEG, el)
    v2 = jnp.max(el2, axis=1, keepdims=True)
    i2 = first(el2 == v2)
    e2 = jnp.exp(v2 - v1)
    w1 = p_grp / (1.0 + e2)
    rw_ref[...] = jnp.where(lane_f == i1, w1, 0.0) + jnp.where(lane_f == i2, w1 * e2, 0.0)


def _mix_route(x2, oa, ob, gm, wg, pa, pb, wo, gf, wr, tm):
    n = x2.shape[0]
    row = lambda c: pl.BlockSpec((tm, c), lambda i: (i, 0))
    full = lambda r, c: pl.BlockSpec((r, c), lambda i: (0, 0))
    return pl.pallas_call(
        _mix_route_kernel,
        grid=(n // tm,),
        in_specs=[row(D_MODEL), row(W_MIX), row(W_MIX), full(1, D_MODEL), full(D_MODEL, 2 * D_MODEL),
                  full(W_MIX, D_MODEL), full(W_MIX, D_MODEL), full(D_MODEL, D_MODEL), full(1, D_MODEL),
                  full(D_MODEL, LANES)],
        out_specs=[row(D_MODEL), row(D_MODEL), row(LANES)],
        out_shape=[jax.ShapeDtypeStruct((n, D_MODEL), F32), jax.ShapeDtypeStruct((n, D_MODEL), BF16),
                   jax.ShapeDtypeStruct((n, LANES), F32)],
        compiler_params=_params("arbitrary"),
        name="mix_route",
    )(x2, oa, ob, gm, wg, pa, pb, wo, gf, wr)


def _moe_kernel(hn_ref, rw_ref, x1_ref, wgu_ref, wd_ref, gn_ref, y_ref, acc):
    e = pl.program_id(1)

    @pl.when(e == 0)
    def _():
        acc[...] = jnp.zeros_like(acc)

    h = jnp.dot(hn_ref[...], wgu_ref[0], preferred_element_type=F32)
    gate = h[:, :D_EXPERT]
    act = gate * _sigmoid(gate) * h[:, D_EXPERT:]
    yb = _dot(act, wd_ref[0])
    rw = rw_ref[...]
    lane = lax.broadcasted_iota(jnp.int32, rw.shape, 1)
    w_col = jnp.sum(jnp.where(lane == e, rw, 0.0), axis=1, keepdims=True)
    acc[...] += yb * w_col

    @pl.when(e == N_EXPERTS - 1)
    def _():
        y_ref[...] = _rms(x1_ref[...] + acc[...], gn_ref[...])


def _moe(hn, rw, x1, wgu, wd, gn, tm):
    n = hn.shape[0]
    row = lambda c: pl.BlockSpec((tm, c), lambda i, e: (i, 0))
    return pl.pallas_call(
        _moe_kernel,
        grid=(n // tm, N_EXPERTS),
        in_specs=[row(D_MODEL), row(LANES), row(D_MODEL),
                  pl.BlockSpec((1, D_MODEL, 2 * D_EXPERT), lambda i, e: (e, 0, 0)),
                  pl.BlockSpec((1, D_EXPERT, D_MODEL), lambda i, e: (e, 0, 0)),
                  pl.BlockSpec((1, D_MODEL), lambda i, e: (0, 0))],
        out_specs=row(D_MODEL),
        out_shape=jax.ShapeDtypeStruct((n, D_MODEL), F32),
        scratch_shapes=[pltpu.VMEM((tm, D_MODEL), F32)],
        compiler_params=_params("arbitrary", "arbitrary"),
        name="moe_dense",
    )(hn, rw, x1, wgu, wd, gn)


def _pad_rw_cols(m):
    z = jnp.zeros(m.shape[:-1] + (LANES - LORA_W,), m.dtype)
    c = 3 * W_MIX
    return jnp.concatenate([m[..., :c], m[..., c:c + LORA_W], z, m[..., c + LORA_W:c + 2 * LORA_W], z,
                            m[..., c + 2 * LORA_W:]], axis=-1)


def _unpad_rw_cols(m):
    c = 3 * W_MIX
    return jnp.concatenate([m[..., :c], m[..., c:c + LORA_W], m[..., c + LANES:c + LANES + LORA_W],
                            m[..., c + 2 * LANES:]], axis=-1)


def _pad_rows(m, rows):
    return jnp.concatenate([m, jnp.zeros((rows - m.shape[0],) + m.shape[1:], m.dtype)], axis=0)


def _prep_weights(l, norm_mix, w_in, b_f, mu_rw, w0, w2, a0, a2, g2, k_k, k_a, r_k, lnx_w, lnx_b,
                  p_a, p_b, w_o, norm_ffn, w_grp, w_exp, we_gate, we_up, we_down):
    w = w_in[l]
    w_fl = jnp.concatenate([w[:, 3 * W_MIX:FOX_COLS], jnp.zeros((D_MODEL, LANES - N_HEADS), F32)], axis=1)
    w_rw = _pad_rw_cols(w[:, FOX_COLS:FOX_COLS + RW_COLS])
    row = lambda vct: vct.reshape(1, -1)
    head_id = jnp.arange(W_MIX) // D_HEAD
    return dict(
        gm=row(norm_mix[l]),
        w_a=jnp.concatenate([w[:, :3 * W_MIX], w_fl, w_rw], axis=1).astype(BF16),
        w_vt=w[:, 2 * W_MIX:3 * W_MIX].T.astype(BF16),
        bf=jnp.concatenate([b_f[l], jnp.zeros((LANES - N_HEADS,), F32)]).reshape(1, LANES),
        wg=w[:, FOX_COLS + RW_COLS:].astype(BF16),
        vecs=(row(_pad_rw_cols(mu_rw[l])), row(w0[l]), row(a0[l]), row(k_k[l]), row(k_a[l]),
              row(r_k[l].reshape(-1)), row(lnx_w[l]), row(lnx_b[l])),
        w2p=_pad_rows(w2[l], LANES).astype(BF16),
        a2p=_pad_rows(a2[l], LANES).astype(BF16),
        g2=g2[l].astype(BF16),
        e_heads=(head_id[:, None] == head_id[None, :]).astype(BF16),
        pa=p_a[l].astype(BF16), pb=p_b[l].astype(BF16), wo=w_o[l].astype(BF16),
        gf=row(norm_ffn[l]),
        wr=jnp.concatenate([w_exp[l], w_grp[l], jnp.zeros((D_MODEL, LANES - N_EXPERTS - N_GROUPS), F32)],
                           axis=1).astype(BF16),
        wgu=jnp.concatenate([we_gate[l], we_up[l]], axis=-1).astype(BF16),
        wd=we_down[l].astype(BF16),
    )


def _layer(x, past, shift0, wkv0, page_table, p, gn, tiles):
    b, t, _ = x.shape
    n = b * t
    tm, tq, tt, tm_moe = tiles
    x2 = x.reshape(n, D_MODEL)
    to3 = lambda m: m.reshape(b, t, m.shape[-1])
    if past is None:
        qb, k, v, kb, vb, lf, rw, vt = _inproj(x2, p["gm"], p["w_a"], p["w_vt"], p["bf"], tm,
                                               QK_SCALE * LOG2E, seq_for_vt=t)
        qa, ka = _qk_aug(to3(lf), to3(qb), to3(kb), tq)
        o_a = _fox_prompt(qa, ka, vt, tq)
        t_pad = t
        u3 = to3(rw)
    else:
        qb, k, v, kb, vb, lf, rw = _inproj(x2, p["gm"], p["w_a"], p["w_vt"], p["bf"], tm, QK_SCALE)
        pool_k, pool_v, pool_lf_t = past
        pad_keys = lambda m: jnp.concatenate([to3(m), jnp.zeros((b, PAGE - t, W_MIX), BF16)], axis=1)
        lfn_t = jnp.concatenate([to3(lf)[:, :, :N_HEADS].transpose(0, 2, 1),
                                 jnp.zeros((b, N_HEADS, PAGE - t), F32)], axis=2)
        o_a = _fox_sample(page_table, to3(qb).astype(F32), pad_keys(kb), pad_keys(vb), lfn_t,
                          pool_k, pool_v, pool_lf_t, t).astype(BF16)
        t_pad = tt
        u3 = jnp.concatenate([to3(rw), jnp.zeros((b, t_pad - t, RW_PAD), F32)], axis=1)
    o_b, wkv_t = _rwkv(u3, shift0, jnp.swapaxes(wkv0, -1, -2), p["vecs"], p["w2p"], p["a2p"], p["g2"],
                       p["e_heads"], tt, t)
    o_b = o_b[:, :t].reshape(n, W_MIX)
    wkv_new = jnp.swapaxes(wkv_t, -1, -2)
    x1, hn, route = _mix_route(x2, o_a.reshape(n, W_MIX), o_b, p["gm"], p["wg"], p["pa"], p["pb"], p["wo"],
                               p["gf"], p["wr"], tm)
    y = _moe(hn, route, x1, p["wgu"], p["wd"], gn, tm_moe)
    shift_new = _unpad_rw_cols(to3(rw)[:, t - 1, :])
    return (y.reshape(b, t, D_MODEL), k.reshape(b, t, N_HEADS, D_HEAD), v.reshape(b, t, N_HEADS, D_HEAD),
            to3(lf)[:, :, :N_HEADS], wkv_new, shift_new)


def kernel(x_prompt, x_sample, cache_k, cache_v, cache_logf, page_table, state_wkv, state_shift, norm_mix, w_in, b_f, mu_rw, w0, w2, a0, a2, g2, k_k, k_a, r_k, lnx_w, lnx_b, p_a, p_b, w_o, norm_ffn, w_grp, w_exp, we_gate, we_up, we_down, norm_final):
    depth = w_in.shape[0]
    assert depth == 1, "final norm is fused into the layer's last kernel"
    bp, tp, _ = x_prompt.shape
    bs, ts, _ = x_sample.shape
    n_phys = cache_k.shape[1]
    gn = norm_final.reshape(1, D_MODEL)
    l = 0
    p = _prep_weights(l, norm_mix, w_in, b_f, mu_rw, w0, w2, a0, a2, g2, k_k, k_a, r_k, lnx_w, lnx_b,
                      p_a, p_b, w_o, norm_ffn, w_grp, w_exp, we_gate, we_up, we_down)
    tq = min(512, tp)
    prompt = _layer(x_prompt, None, jnp.zeros((bp, 1, RW_PAD), F32),
                    jnp.zeros((bp, N_HEADS, D_HEAD, D_HEAD), F32), None, p, gn,
                    (min(512, bp * tp), tq, min(256, tp), min(1024, bp * tp)))
    past = (cache_k[l].transpose(0, 2, 3, 1).reshape(n_phys, W_MIX, PAGE),
            cache_v[l].transpose(0, 2, 3, 1).reshape(n_phys, W_MIX, PAGE),
            cache_logf[l].transpose(0, 2, 1))
    sample = _layer(x_sample, past, _pad_rw_cols(state_shift[l])[:, None, :], state_wkv[l], page_table, p, gn,
                    (bs * ts, None, CHUNK, bs * ts))
    outs = []
    for y, k, v, lf, wkv, sh in (prompt, sample):
        outs.append((y, k[None], v[None], lf[None], wkv[None], sh[None]))
    (yp, kp, vp, lp, wp, sp), (ys, ks, vs, ls, ws, ss) = outs
    return (yp, ys, kp, vp, lp, wp, sp, ks, vs, ls, ws, ss)
```

```python
import functools

import numpy as np
import jax
import jax.numpy as jnp
from jax import lax
from jax.experimental import pallas as pl
from jax.experimental.pallas import tpu as pltpu

F32 = jnp.float32
BF16 = jnp.bfloat16

D_MODEL = 1024
N_HEADS = 8
D_HEAD = 64
W_MIX = N_HEADS * D_HEAD
LORA_W = 64
LORA_G = 128
RW_COLS = 3 * W_MIX + 2 * LORA_W + LORA_G
RW_PAD = 3 * W_MIX + 3 * 128
FOX_COLS = 3 * W_MIX + N_HEADS
N_EXPERTS = 32
EXPERTS_PER_GROUP = 8
N_GROUPS = 4
D_EXPERT = 256
PAGE = 128
MAX_PAGES_PER_STEP = 8
NORM_EPS = 1e-6
GN_EPS = 64e-5
QK_SCALE = D_HEAD ** -0.5
LANES = 128
CHUNK = 64
EXP_NEG_HALF = 0.6065306597126334
NEG = -1e30
VMEM_LIMIT = 56 * 1024 * 1024


def _dot(a, b):
    return jnp.dot(a.astype(BF16), b.astype(BF16), preferred_element_type=F32)


def _bdot(a, b):
    return jnp.dot(a, b, preferred_element_type=F32)


def _bdot_nt(a, b):
    return lax.dot_general(a, b, (((1,), (1,)), ((), ())), preferred_element_type=F32)


def _each(f, *lists):
    return [f(*xs) for xs in zip(*lists)]


def _split3(x):
    hi = x.astype(BF16)
    r1 = x - hi.astype(F32)
    mid = r1.astype(BF16)
    lo = (r1 - mid.astype(F32)).astype(BF16)
    return hi, mid, lo


def _dot_sel_lhs(sel, x):
    hi, mid, lo = _split3(x)
    d = lambda p: jnp.dot(sel, p, preferred_element_type=F32)
    return d(hi) + d(mid) + d(lo)


def _dot_sel_rhs(x, sel):
    hi, mid, lo = _split3(x)
    d = lambda p: jnp.dot(p, sel, preferred_element_type=F32)
    return d(hi) + d(mid) + d(lo)


def _sigmoid(x):
    return 0.5 * jnp.tanh(0.5 * x) + 0.5


def _log_sigmoid(x):
    return jnp.minimum(x, 0.0) - jnp.log(1.0 + jnp.exp(-jnp.abs(x)))


def _rms(x, g):
    return (x * lax.rsqrt(jnp.mean(x * x, axis=-1, keepdims=True) + NORM_EPS)) * g


def _params(*sem):
    return pltpu.CompilerParams(dimension_semantics=sem, vmem_limit_bytes=VMEM_LIMIT)


def _inproj_kernel(x_ref, g_ref, w_ref, wvt_ref, bf_ref, q_ref, k_ref, v_ref, kb_ref, vb_ref, lf_ref, rw_ref,
                   *maybe_vt_ref, q_scale):
    xb = _rms(x_ref[...], g_ref[...]).astype(BF16)
    d = lambda lo, hi: jnp.dot(xb, w_ref[:, lo:hi], preferred_element_type=F32)
    q_ref[...] = (d(0, W_MIX) * q_scale).astype(BF16)
    k = d(W_MIX, 2 * W_MIX)
    k_ref[...] = k
    kb_ref[...] = k.astype(BF16)
    v = d(2 * W_MIX, 3 * W_MIX)
    v_ref[...] = v
    vb_ref[...] = v.astype(BF16)
    lf_ref[...] = _log_sigmoid(d(3 * W_MIX, 3 * W_MIX + LANES) + bf_ref[...])
    rw_ref[...] = d(3 * W_MIX + LANES, 3 * W_MIX + LANES + RW_PAD)
    if maybe_vt_ref:
        maybe_vt_ref[0][0] = _bdot_nt(wvt_ref[...], xb).astype(BF16)


def _inproj(x2, g, w_a, w_vt, bf_pad, tm, q_scale, seq_for_vt=None):
    n = x2.shape[0]
    wcols = w_a.shape[1]
    row = lambda c: pl.BlockSpec((tm, c), lambda i: (i, 0))
    full = lambda r, c: pl.BlockSpec((r, c), lambda i: (0, 0))
    out_specs = [row(W_MIX), row(W_MIX), row(W_MIX), row(W_MIX), row(W_MIX), row(LANES), row(RW_PAD)]
    out_shape = [jax.ShapeDtypeStruct((n, W_MIX), BF16), jax.ShapeDtypeStruct((n, W_MIX), F32),
                 jax.ShapeDtypeStruct((n, W_MIX), F32), jax.ShapeDtypeStruct((n, W_MIX), BF16),
                 jax.ShapeDtypeStruct((n, W_MIX), BF16), jax.ShapeDtypeStruct((n, LANES), F32),
                 jax.ShapeDtypeStruct((n, RW_PAD), F32)]
    if seq_for_vt is not None:
        per_seq = seq_for_vt // tm
        out_specs.append(pl.BlockSpec((1, W_MIX, tm), lambda i: (i // per_seq, 0, i % per_seq)))
        out_shape.append(jax.ShapeDtypeStruct((n // seq_for_vt, W_MIX, seq_for_vt), BF16))
    return pl.pallas_call(
        functools.partial(_inproj_kernel, q_scale=q_scale),
        grid=(n // tm,),
        in_specs=[row(D_MODEL), full(1, D_MODEL), full(D_MODEL, wcols), full(W_MIX, D_MODEL), full(1, LANES)],
        out_specs=out_specs,
        out_shape=out_shape,
        compiler_params=_params("arbitrary"),
        name="inproj",
    )(x2, g, w_a, w_vt, bf_pad)


PAIR_W = 2 * D_HEAD
N_PAIRS = N_HEADS // 2
QK_W = PAIR_W + LANES
PAIRS_PER_STEP = 2
HEADS_PER_STEP = 2 * PAIRS_PER_STEP
KEY_BLK = 32
LOG2E = 1.4426950408889634


def _bias_tables():
    pq = np.zeros((3, LANES, N_PAIRS * LANES), np.float32)
    pk = np.zeros((3, LANES, N_PAIRS * LANES), np.float32)
    one_q = np.zeros((1, N_PAIRS * LANES), np.float32)
    one_k = np.zeros((1, N_PAIRS * LANES), np.float32)
    for h in range(N_HEADS):
        base = LANES * (h // 2)
        for piece in range(3):
            pq[piece, h, base + 3 * (h % 2) + piece] = 1.0
            pk[piece, h, base + 6 + 3 * (h % 2) + piece] = -1.0
    for g in range(N_PAIRS):
        one_q[0, LANES * g + 6:LANES * g + 12] = 1.0
        one_k[0, LANES * g + 0:LANES * g + 6] = 1.0
    return (jnp.asarray(pq, BF16), jnp.asarray(pk, BF16), jnp.asarray(one_q), jnp.asarray(one_k))


def _qk_aug_kernel(lf_ref, q_ref, k_ref, pq_ref, pk_ref, oq_ref, ok_ref, qa_ref, ka_ref, carry):
    @pl.when(pl.program_id(1) == 0)
    def _():
        carry[...] = jnp.zeros_like(carry)

    lf = lf_ref[0] * LOG2E
    tm = lf.shape[0]
    row = lax.broadcasted_iota(jnp.int32, (tm, tm), 0)
    col = lax.broadcasted_iota(jnp.int32, (tm, tm), 1)
    c = _dot_sel_lhs((col <= row).astype(BF16), lf) + carry[...]
    carry[...] = c[tm - 1:tm, :]
    pieces = _split3(c)
    aug_q = oq_ref[...] + sum(_bdot(pc, pq_ref[i]) for i, pc in enumerate(pieces))
    aug_k = ok_ref[...] + sum(_bdot(pc, pk_ref[i]) for i, pc in enumerate(pieces))
    q = q_ref[0]
    k = k_ref[0]
    for g in range(N_PAIRS):
        qa_ref[0, :, g * QK_W:g * QK_W + PAIR_W] = q[:, g * PAIR_W:(g + 1) * PAIR_W]
        qa_ref[0, :, g * QK_W + PAIR_W:(g + 1) * QK_W] = aug_q[:, g * LANES:(g + 1) * LANES].astype(BF16)
        ka_ref[0, :, g * QK_W:g * QK_W + PAIR_W] = k[:, g * PAIR_W:(g + 1) * PAIR_W]
        ka_ref[0, :, g * QK_W + PAIR_W:(g + 1) * QK_W] = aug_k[:, g * LANES:(g + 1) * LANES].astype(BF16)


def _qk_aug(lf3, qb, kb, tm):
    b, t, _ = lf3.shape
    pq, pk, one_q, one_k = _bias_tables()
    blk = lambda c: pl.BlockSpec((1, tm, c), lambda i, j: (i, j, 0))
    cst = lambda a: pl.BlockSpec(a.shape, lambda i, j: (0,) * a.ndim)
    return pl.pallas_call(
        _qk_aug_kernel,
        grid=(b, t // tm),
        in_specs=[blk(LANES), blk(W_MIX), blk(W_MIX), cst(pq), cst(pk), cst(one_q), cst(one_k)],
        out_specs=[blk(N_PAIRS * QK_W), blk(N_PAIRS * QK_W)],
        out_shape=[jax.ShapeDtypeStruct((b, t, N_PAIRS * QK_W), BF16)] * 2,
        scratch_shapes=[pltpu.VMEM((1, LANES), F32)],
        compiler_params=_params("arbitrary", "arbitrary"),
        name="logf_cumsum_qk_aug",
    )(lf3, qb, kb, pq, pk, one_q, one_k)


def _fox_prompt_kernel(qi_ref, kj_ref, q_ref, k_ref, vt_ref, o_ref, qm, m_s, l_s, acc, s_scr, p_scr,
                       *, tq, tk):
    step = pl.program_id(2)
    i = qi_ref[step]
    j = kj_ref[step]

    @pl.when(j == 0)
    def _():
        lane = lax.broadcasted_iota(jnp.int32, (tq, QK_W), 1)
        off = lane - PAIR_W
        for hh in range(HEADS_PER_STEP):
            pr, h = divmod(hh, 2)
            q = q_ref[0, :, pr * QK_W:(pr + 1) * QK_W].astype(F32)
            keep = ((lane >= h * D_HEAD) & (lane < (h + 1) * D_HEAD)) \
                | ((off >= 3 * h) & (off < 3 * h + 3)) | ((off >= 6 + 3 * h) & (off < 9 + 3 * h))
            qm[hh] = jnp.where(keep, q, 0.0).astype(BF16)
        m_s[...] = jnp.full_like(m_s, NEG)
        l_s[...] = jnp.zeros_like(l_s)
        acc[...] = jnp.zeros_like(acc)

    def block(diag):
        key = lax.broadcasted_iota(jnp.int32, (KEY_BLK, tq), 0)
        qry = lax.broadcasted_iota(jnp.int32, (KEY_BLK, tq), 1)
        n_kb = tk // KEY_BLK
        groups = lambda t: [t[8 * r:8 * (r + 1), :] for r in range(KEY_BLK // 8)]

        def scores(h, kb):
            s = s_scr[h, kb * KEY_BLK:(kb + 1) * KEY_BLK, :]
            if diag:
                s = jnp.where(key + kb * KEY_BLK <= qry, s, NEG)
            return s

        for h in range(HEADS_PER_STEP):
            pr = h // 2
            s_scr[h] = _bdot_nt(k_ref[0, :, pr * QK_W:(pr + 1) * QK_W], qm[h])
        for h in range(HEADS_PER_STEP):
            m8 = jnp.full((8, tq), NEG, F32)
            for kb in range(n_kb):
                for g in groups(scores(h, kb)):
                    m8 = jnp.maximum(m8, g)
            m_prev = m_s[h]
            m_new = jnp.maximum(m_prev, jnp.max(m8, axis=0, keepdims=True))
            alpha = jnp.exp2(m_prev - m_new)
            l8 = jnp.zeros((8, tq), F32)
            for kb in range(n_kb):
                p = jnp.exp2(scores(h, kb) - m_new)
                for g in groups(p):
                    l8 = l8 + g
                p_scr[h, kb * KEY_BLK:(kb + 1) * KEY_BLK, :] = p.astype(BF16)
            l_s[h] = alpha * l_s[h] + jnp.sum(l8, axis=0, keepdims=True)
            m_s[h] = m_new
            vt = vt_ref[0, h * D_HEAD:(h + 1) * D_HEAD, :]
            acc[h] = alpha * acc[h] + _bdot(vt, p_scr[h])

    @pl.when(j < i)
    def _():
        block(False)

    @pl.when(j == i)
    def _():
        block(True)
        o_t = jnp.concatenate([acc[h] / l_s[h] for h in range(HEADS_PER_STEP)], axis=0)
        o_ref[0] = o_t.T.astype(o_ref.dtype)


def _fox_prompt(qa, ka, vt, tq):
    b, _, t = vt.shape
    tk = tq
    nq = t // tq
    qi = np.array([i for i in range(nq) for _ in range(i + 1)], np.int32)
    kj = np.array([j for i in range(nq) for j in range(i + 1)], np.int32)
    kern = functools.partial(_fox_prompt_kernel, tq=tq, tk=tk)
    grid_spec = pltpu.PrefetchScalarGridSpec(
        num_scalar_prefetch=2,
        grid=(b, N_PAIRS // PAIRS_PER_STEP, len(qi)),
        in_specs=[
            pl.BlockSpec((1, tq, PAIRS_PER_STEP * QK_W), lambda bi, g, s, qi_, kj_: (bi, qi_[s], g)),
            pl.BlockSpec((1, tk, PAIRS_PER_STEP * QK_W), lambda bi, g, s, qi_, kj_: (bi, kj_[s], g)),
            pl.BlockSpec((1, PAIRS_PER_STEP * PAIR_W, tk), lambda bi, g, s, qi_, kj_: (bi, g, kj_[s])),
        ],
        out_specs=pl.BlockSpec((1, tq, PAIRS_PER_STEP * PAIR_W), lambda bi, g, s, qi_, kj_: (bi, qi_[s], g)),
        scratch_shapes=[pltpu.VMEM((HEADS_PER_STEP, tq, QK_W), BF16), pltpu.VMEM((HEADS_PER_STEP, 1, tq), F32),
                        pltpu.VMEM((HEADS_PER_STEP, 1, tq), F32), pltpu.VMEM((HEADS_PER_STEP, D_HEAD, tq), F32),
                        pltpu.VMEM((HEADS_PER_STEP, tk, tq), F32), pltpu.VMEM((HEADS_PER_STEP, tk, tq), BF16)],
    )
    return pl.pallas_call(
        kern,
        grid_spec=grid_spec,
        out_shape=jax.ShapeDtypeStruct((b, t, W_MIX), BF16),
        compiler_params=_params("arbitrary", "arbitrary", "arbitrary"),
        name="fox_prompt",
    )(jnp.asarray(qi), jnp.asarray(kj), qa, ka, vt)


def _fox_sample_kernel(pt_ref, q_ref, kn_ref, vn_ref, lfn_ref, *rest, n_new, n_steps, n_grp):
    del pt_ref
    kp_refs, vp_refs, lfp_refs = rest[:n_grp], rest[n_grp:2 * n_grp], rest[2 * n_grp:3 * n_grp]
    o_ref, qm, m_s, l_s, acc, carry, cq_s = rest[3 * n_grp:]
    j = pl.program_id(1)
    rows = n_new * N_HEADS
    lane8 = lax.broadcasted_iota(jnp.int32, (N_HEADS, W_MIX), 1)
    head8 = lax.broadcasted_iota(jnp.int32, (N_HEADS, W_MIX), 0)
    head_sel = (lane8 >> 6) == head8
    kr = lax.broadcasted_iota(jnp.int32, (PAGE, PAGE), 0)
    kc = lax.broadcasted_iota(jnp.int32, (PAGE, PAGE), 1)
    tile = lambda x: jnp.concatenate([x] * n_new, axis=0)

    def update(s, pv):
        m_prev = m_s[...]
        m_new = jnp.maximum(m_prev, jnp.max(s, axis=1, keepdims=True))
        alpha = jnp.exp(m_prev - m_new)
        p = jnp.exp(s - m_new)
        l_s[...] = alpha * l_s[...] + jnp.sum(p, axis=1, keepdims=True)
        acc[...] = alpha * acc[...] + pv(p.astype(BF16))
        m_s[...] = m_new

    @pl.when(j == 0)
    def _():
        q = q_ref[0].astype(F32)
        qrows = [jnp.where(head_sel, jnp.broadcast_to(q[t:t + 1, :], (N_HEADS, W_MIX)), 0.0)
                 for t in range(n_new)]
        qm[...] = jnp.concatenate(qrows, axis=0).astype(BF16)
        m_s[...] = jnp.full_like(m_s, NEG)
        l_s[...] = jnp.zeros_like(l_s)
        acc[...] = jnp.zeros_like(acc)
        carry[...] = jnp.zeros_like(carry)
        cn = _dot_sel_rhs(lfn_ref[0], (kr <= kc).astype(BF16))
        for t in range(n_new):
            cq_s[t * N_HEADS:(t + 1) * N_HEADS, :] = cn[:, t:t + 1]
        s = lax.dot_general(qm[...], kn_ref[0], (((1,), (1,)), ((), ())), preferred_element_type=F32)
        s = s + (cq_s[...] - tile(cn))
        tok = lax.broadcasted_iota(jnp.int32, (rows, PAGE), 0) >> 3
        key = lax.broadcasted_iota(jnp.int32, (rows, PAGE), 1)
        s = jnp.where(key <= tok, s, NEG)
        update(s, lambda p: _bdot(p, vn_ref[0]))

    @pl.when(j > 0)
    def _():
        lfs = [ref[0] for ref in lfp_refs]
        suffix = _dot_sel_rhs(jnp.concatenate(lfs, axis=0), (kr > kc).astype(BF16))
        later = carry[...]
        cq = cq_s[...]
        parts = []
        for g in range(n_grp):
            bias8 = suffix[g * N_HEADS:(g + 1) * N_HEADS, :] + later
            s_g = _bdot(qm[...], kp_refs[g][0].astype(BF16))
            parts.append(s_g + (cq + tile(bias8)))
            later = later + jnp.sum(lfs[g], axis=1, keepdims=True)
        carry[...] = later
        s = jnp.concatenate(parts, axis=1)

        def pv(p):
            out = _bdot_nt(p[:, 0:PAGE], vp_refs[0][0].astype(BF16))
            for g in range(1, n_grp):
                out = out + _bdot_nt(p[:, g * PAGE:(g + 1) * PAGE], vp_refs[g][0].astype(BF16))
            return out

        update(s, pv)

    @pl.when(j == n_steps - 1)
    def _():
        o = acc[...] / l_s[...]
        for t in range(n_new):
            ot = jnp.where(head_sel, o[t * N_HEADS:(t + 1) * N_HEADS, :], 0.0)
            o_ref[0, t:t + 1, :] = jnp.sum(ot, axis=0, keepdims=True).astype(o_ref.dtype)


def _fox_sample(page_table, qb, kn, vn, lfn_t, pool_k, pool_v, pool_lf_t, n_new):
    r, n_pages = page_table.shape
    rows = n_new * N_HEADS
    n_grp = max(g for g in range(1, MAX_PAGES_PER_STEP + 1) if n_pages % g == 0)
    n_steps = n_pages // n_grp + 1
    kern = functools.partial(_fox_sample_kernel, n_new=n_new, n_steps=n_steps, n_grp=n_grp)

    def page(g):
        return lambda ri, j, pt: (pt[ri, n_pages - 1 - ((jnp.maximum(j, 1) - 1) * n_grp + g)], 0, 0)

    req = lambda ri, j, pt: (ri, 0, 0)
    grid_spec = pltpu.PrefetchScalarGridSpec(
        num_scalar_prefetch=1,
        grid=(r, n_steps),
        in_specs=[
            pl.BlockSpec((1, n_new, W_MIX), req),
            pl.BlockSpec((1, PAGE, W_MIX), req),
            pl.BlockSpec((1, PAGE, W_MIX), req),
            pl.BlockSpec((1, N_HEADS, PAGE), req),
        ] + [pl.BlockSpec((1, W_MIX, PAGE), page(g)) for g in range(n_grp)]
          + [pl.BlockSpec((1, W_MIX, PAGE), page(g)) for g in range(n_grp)]
          + [pl.BlockSpec((1, N_HEADS, PAGE), page(g)) for g in range(n_grp)],
        out_specs=pl.BlockSpec((1, n_new, W_MIX), req),
        scratch_shapes=[pltpu.VMEM((rows, W_MIX), BF16), pltpu.VMEM((rows, 1), F32),
                        pltpu.VMEM((rows, 1), F32), pltpu.VMEM((rows, W_MIX), F32),
                        pltpu.VMEM((N_HEADS, 1), F32), pltpu.VMEM((rows, 1), F32)],
    )
    return pl.pallas_call(
        kern,
        grid_spec=grid_spec,
        out_shape=jax.ShapeDtypeStruct((r, n_new, W_MIX), F32),
        compiler_params=_params("arbitrary", "arbitrary"),
        name="fox_sample",
    )(page_table, qb, kn, vn, lfn_t, *([pool_k] * n_grp), *([pool_v] * n_grp), *([pool_lf_t] * n_grp))


HEADS_PER_ITER = 4


def _rwkv_kernel(u_ref, sh_ref, s0_ref, mu_ref, w0_ref, w2_ref, a0_ref, a2_ref, g2_ref, kk_ref, ka_ref,
                 rk_ref, lw_ref, lb_ref, e_ref, msk_ref, o_ref, sout_ref,
                 carry, st, kt_s, rt_s, kh_s, bh_s, k0_s, r0_s, khp_s, bhp_s, v_s, gt_s, o_s,
                 *, tt, t_real, t_total):
    ti = pl.program_id(1)
    nch = tt // CHUNK

    @pl.when(ti == 0)
    def _():
        carry[...] = sh_ref[0]
        st[...] = s0_ref[0]

    u = u_ref[0]
    row1 = lax.broadcasted_iota(jnp.int32, (tt, 1), 0)
    u_prev = jnp.where(row1 == 0, carry[...], pltpu.roll(u, 1, 0))
    carry[...] = u[tt - 1:tt, :]
    us = u + mu_ref[...] * (u_prev - u)
    r = us[:, 0:W_MIX]
    k = us[:, W_MIX:2 * W_MIX]
    v = us[:, 2 * W_MIX:3 * W_MIX]
    wl = us[:, 3 * W_MIX:3 * W_MIX + LANES]
    al = us[:, 3 * W_MIX + LANES:3 * W_MIX + 2 * LANES]
    gl = us[:, 3 * W_MIX + 2 * LANES:3 * W_MIX + 3 * LANES]
    z = w0_ref[...] + _dot(jnp.tanh(wl), w2_ref[...])
    logd = -EXP_NEG_HALF * _sigmoid(z)
    a = _sigmoid(a0_ref[...] + _dot(al, a2_ref[...]))
    g = _dot(_sigmoid(gl), g2_ref[...])
    e_heads = e_ref[...]
    kk = k * kk_ref[...]
    kk = kk * lax.rsqrt(jnp.maximum(_dot(kk * kk, e_heads), 1e-24))
    k_mod = k * (1.0 + (a - 1.0) * ka_ref[...])
    b = kk * a
    bonus = _dot(r * k_mod * rk_ref[...], e_heads) * v
    if t_real < t_total:
        valid = (ti * tt + row1) < t_real
        zero = lambda t: jnp.where(valid, t, 0.0)
        kk, b, k_mod, v, logd = zero(kk), zero(b), zero(k_mod), zero(v), zero(logd)

    hi, mid, lo = _split3(logd)
    sel3 = lambda m: (_bdot(m, hi) + _bdot(m, mid) + _bdot(m, lo))
    cum = sel3(msk_ref[1].astype(BF16))
    ref = sel3(msk_ref[7].astype(BF16))
    tot = sel3(msk_ref[8].astype(BF16))
    cum_prev = cum - logd
    e_inv = jnp.exp(ref - cum)
    e_tot = jnp.exp(tot - cum)
    wide = dict(kt=kk * jnp.exp(cum_prev - ref), rt=r * jnp.exp(cum - ref), kh=k_mod * e_inv, bh=b * e_inv,
                k0=kk * jnp.exp(cum_prev), r0=r * jnp.exp(cum), khp=k_mod * e_tot, bhp=b * e_tot, v=v,
                gt=jnp.exp(tot))
    dst = dict(kt=kt_s, rt=rt_s, kh=kh_s, bh=bh_s, k0=k0_s, r0=r0_s, khp=khp_s, bhp=bhp_s, v=v_s, gt=gt_s)
    for name, val in wide.items():
        val = val.astype(dst[name].dtype)
        for h in range(N_HEADS):
            dst[name][h] = val[:, h * D_HEAD:(h + 1) * D_HEAD]

    def head_body(p, _):
        hs = [p * HEADS_PER_ITER + i for i in range(HEADS_PER_ITER)]
        ld = lambda s: [s[h] for h in hs]
        ldb = lambda s: [s[h].astype(BF16) for h in hs]
        kt, rt, kh, bh = ldb(kt_s), ldb(rt_s), ldb(kh_s), ldb(bh_s)
        strict = msk_ref[0] != 0.0
        incl = msk_ref[1] != 0.0
        a_kk = _each(lambda x, y: jnp.where(strict, _bdot_nt(x, y), 0.0).astype(BF16), kt, kh)
        a_kb = _each(lambda x, y: jnp.where(strict, _bdot_nt(x, y), 0.0), kt, bh)
        a_rk = _each(lambda x, y: jnp.where(incl, _bdot_nt(x, y), 0.0).astype(BF16), rt, kh)
        a_rb = _each(lambda x, y: jnp.where(incl, _bdot_nt(x, y), 0.0).astype(BF16), rt, bh)
        x = _each(lambda t: (-(t * msk_ref[2])).astype(BF16), a_kb)
        x2 = _each(lambda t: _bdot(t, t).astype(BF16), x)
        pm = _each(lambda t: msk_ref[6] + t.astype(F32), x)
        pm = _each(lambda q, t: q + _bdot(q.astype(BF16), t), pm, x2)
        x4 = _each(lambda t: _bdot(t, t).astype(BF16), x2)
        pm = _each(lambda q, t: q + _bdot(q.astype(BF16), t), pm, x4)
        for lvl in (3, 4, 5):
            pb = _each(lambda q: q.astype(BF16), pm)
            a_off = _each(lambda t: (t * msk_ref[lvl]).astype(BF16), a_kb)
            pa = _each(lambda q, t: _bdot(q, t).astype(BF16), pb, a_off)
            pm = _each(lambda q, s, qb: q - _bdot(s, qb), pm, pa, pb)
        t_inv = _each(lambda q: q.astype(BF16), pm)
        vb = ldb(v_s)
        w1 = _each(lambda t, y: _bdot(t, y), t_inv, ldb(k0_s))
        av = _each(lambda t, y: _bdot(t, y).astype(BF16), a_kk, vb)
        u1 = _each(lambda t, y: _bdot(t, y), t_inv, av)
        w1b = _each(lambda t: t.astype(BF16), w1)
        u1b = _each(lambda t: t.astype(BF16), u1)
        q_t = _each(lambda r0, t, y: (r0 - _bdot(t, y)).astype(BF16), ld(r0_s), a_rb, w1b)
        o1 = _each(lambda t, y, t2, y2: _bdot(t, y) - _bdot(t2, y2), a_rk, vb, a_rb, u1b)
        khp, bhp, gt = ld(khp_s), ld(bhp_s), ld(gt_s)
        for c in range(nch):
            rows = slice(c * CHUNK, (c + 1) * CHUNK)
            bt = _each(lambda t: t[rows].T.astype(BF16), bhp)
            kt_ = _each(lambda t: t[rows].T.astype(BF16), khp)
            gcol = _each(lambda t: t[rows].T[:, 0:1], gt)
            m_low = _each(lambda t, y: _bdot(t, y[rows]).astype(BF16), bt, w1b)
            n1 = _each(lambda t, y, t2, y2: _bdot(t, y[rows]) - _bdot(t2, y2[rows]), kt_, vb, bt, u1b)
            for i, h in enumerate(hs):
                s_t = st[h]
                s_hi = s_t.astype(BF16)
                s_lo = (s_t - s_hi.astype(F32)).astype(BF16)
                qc = q_t[i][rows]
                o_s[h, rows, :] = _bdot(qc, s_hi) + _bdot(qc, s_lo) + o1[i][rows]
                st[h] = gcol[i] * s_t - (_bdot(m_low[i], s_hi) + _bdot(m_low[i], s_lo)) + n1[i]
        return 0

    lax.fori_loop(0, N_HEADS // HEADS_PER_ITER, head_body, 0)

    o = jnp.concatenate([o_s[h] for h in range(N_HEADS)], axis=1)
    mean = _dot(o, e_heads) * (1.0 / D_HEAD)
    xc = o - mean
    var = _dot(xc * xc, e_heads) * (1.0 / D_HEAD)
    on = xc * lax.rsqrt(var + GN_EPS) * lw_ref[...] + lb_ref[...]
    o_ref[0] = ((on + bonus) * g).astype(o_ref.dtype)
    sout_ref[0] = st[...]


def _chunk_masks(tt):
    i = jnp.arange(tt)[:, None]
    j = jnp.arange(tt)[None, :]
    blk = lambda s: (i >> s) == (j >> s)
    same = blk(6)
    ms = [same & (j < i), same & (j <= i), blk(3), blk(4) & ~blk(3), blk(5) & ~blk(4), same & ~blk(5),
          i == j, same & ((j & (CHUNK - 1)) < CHUNK // 2), same]
    return jnp.stack(ms).astype(F32)


def _rwkv(u3, shift0, s0_t, vecs, w2p, a2p, g2, e_heads, tt, t_real):
    b, t, _ = u3.shape
    mu, w0, a0, k_k, k_a, r_k, lnw, lnb = vecs
    masks = _chunk_masks(tt)
    kern = functools.partial(_rwkv_kernel, tt=tt, t_real=t_real, t_total=t)
    vec = lambda c: pl.BlockSpec((1, c), lambda bi, ti: (0, 0))
    mat = lambda r_, c: pl.BlockSpec((r_, c), lambda bi, ti: (0, 0))
    wide = lambda dt: pltpu.VMEM((N_HEADS, tt, D_HEAD), dt)
    return pl.pallas_call(
        kern,
        grid=(b, t // tt),
        in_specs=[
            pl.BlockSpec((1, tt, RW_PAD), lambda bi, ti: (bi, ti, 0)),
            pl.BlockSpec((1, 1, RW_PAD), lambda bi, ti: (bi, 0, 0)),
            pl.BlockSpec((1, N_HEADS, D_HEAD, D_HEAD), lambda bi, ti: (bi, 0, 0, 0)),
            vec(RW_PAD), vec(W_MIX), mat(LANES, W_MIX), vec(W_MIX), mat(LANES, W_MIX), mat(LANES, W_MIX),
            vec(W_MIX), vec(W_MIX), vec(W_MIX), vec(W_MIX), vec(W_MIX), mat(W_MIX, W_MIX),
            pl.BlockSpec(masks.shape, lambda bi, ti: (0, 0, 0)),
        ],
        out_specs=[pl.BlockSpec((1, tt, W_MIX), lambda bi, ti: (bi, ti, 0)),
                   pl.BlockSpec((1, N_HEADS, D_HEAD, D_HEAD), lambda bi, ti: (bi, 0, 0, 0))],
        out_shape=[jax.ShapeDtypeStruct((b, t, W_MIX), BF16),
                   jax.ShapeDtypeStruct((b, N_HEADS, D_HEAD, D_HEAD), F32)],
        scratch_shapes=[pltpu.VMEM((1, RW_PAD), F32), pltpu.VMEM((N_HEADS, D_HEAD, D_HEAD), F32)]
                       + [wide(BF16 if name in ("kt", "rt", "kh", "bh", "k0", "v") else F32)
                          for name in ("kt", "rt", "kh", "bh", "k0", "r0", "khp", "bhp", "v", "gt", "o")],
        compiler_params=_params("arbitrary", "arbitrary"),
        name="rwkv7_chunked",
    )(u3, shift0, s0_t, mu, w0, w2p, a0, a2p, g2, k_k, k_a, r_k, lnw, lnb, e_heads, masks)


def _mix_route_kernel(x_ref, oa_ref, ob_ref, gm_ref, wg_ref, pa_ref, pb_ref, wo_ref, gf_ref, wr_ref,
                      x1_ref, hn_ref, rw_ref):
    x = x_ref[...]
    xb = _rms(x, gm_ref[...]).astype(BF16)
    gates = _sigmoid(jnp.dot(xb, wg_ref[...], preferred_element_type=F32))
    mixed = (gates[:, :D_MODEL] * jnp.dot(oa_ref[...], pa_ref[...], preferred_element_type=F32)
             + gates[:, D_MODEL:] * jnp.dot(ob_ref[...], pb_ref[...], preferred_element_type=F32))
    x1 = x + _dot(mixed, wo_ref[...])
    x1_ref[...] = x1
    hb = _rms(x1, gf_ref[...]).astype(BF16)
    hn_ref[...] = hb
    logits = jnp.dot(hb, wr_ref[...], preferred_element_type=F32)
    tm = logits.shape[0]
    lane = lax.broadcasted_iota(jnp.int32, (tm, LANES), 1)
    lane_f = lane.astype(F32)
    first = lambda hit: jnp.min(jnp.where(hit, lane_f, 1e9), axis=1, keepdims=True)
    is_grp = (lane >= N_EXPERTS) & (lane < N_EXPERTS + N_GROUPS)
    gl = jnp.where(is_grp, logits, NEG)
    gmax = jnp.max(gl, axis=1, keepdims=True)
    g_idx = first(gl == gmax) - float(N_EXPERTS)
    p_grp = 1.0 / jnp.sum(jnp.where(is_grp, jnp.exp(gl - gmax), 0.0), axis=1, keepdims=True)
    in_grp = (lane < N_EXPERTS) & ((lane >> 3).astype(F32) == g_idx)
    el = jnp.where(in_grp, logits, NEG)
    v1 = jnp.max(el, axis=1, keepdims=True)
    i1 = first(el == v1)
    el2 = jnp.where(lane_f == i1, NEG, el)
    v2 = jnp.max(el2, axis=1, keepdims=True)
    i2 = first(el2 == v2)
    e2 = jnp.exp(v2 - v1)
    w1 = p_grp / (1.0 + e2)
    rw_ref[...] = jnp.where(lane_f == i1, w1, 0.0) + jnp.where(lane_f == i2, w1 * e2, 0.0)


def _mix_route(x2, oa, ob, gm, wg, pa, pb, wo, gf, wr, tm):
    n = x2.shape[0]
    row = lambda c: pl.BlockSpec((tm, c), lambda i: (i, 0))
    full = lambda r, c: pl.BlockSpec((r, c), lambda i: (0, 0))
    return pl.pallas_call(
        _mix_route_kernel,
        grid=(n // tm,),
        in_specs=[row(D_MODEL), row(W_MIX), row(W_MIX), full(1, D_MODEL), full(D_MODEL, 2 * D_MODEL),
                  full(W_MIX, D_MODEL), full(W_MIX, D_MODEL), full(D_MODEL, D_MODEL), full(1, D_MODEL),
                  full(D_MODEL, LANES)],
        out_specs=[row(D_MODEL), row(D_MODEL), row(LANES)],
        out_shape=[jax.ShapeDtypeStruct((n, D_MODEL), F32), jax.ShapeDtypeStruct((n, D_MODEL), BF16),
                   jax.ShapeDtypeStruct((n, LANES), F32)],
        compiler_params=_params("arbitrary"),
        name="mix_route",
    )(x2, oa, ob, gm, wg, pa, pb, wo, gf, wr)


MOE_CHUNK = 256
MOE_TAIL_CHUNK = 128


def _moe_kernel(cnt_ref, hn_ref, rw_ref, wgu_ref, wd_ref, tri_ref, out_ref, acc, *, tile, chunks):
    i = pl.program_id(0)
    g = pl.program_id(1)

    @pl.when(g == 0)
    def _():
        acc[...] = jnp.zeros_like(acc)

    rw = rw_ref[...]
    lane = lax.broadcasted_iota(jnp.int32, (tile, LANES), 1)
    in_g = (lane >> 3) == g
    member = jnp.sum(jnp.where(in_g, rw, 0.0), axis=1, keepdims=True) > 0.0
    ind = jnp.broadcast_to(jnp.where(member, 1.0, 0.0), (tile, LANES))
    rank = _bdot(tri_ref[...], ind.astype(BF16))
    ind_row = ind.T[0:1, :]
    rank_row = rank.T[0:1, :]
    count = cnt_ref[i * N_GROUPS + g]
    pieces = _split3(rw)

    for start, chunk in chunks:
        @pl.when(start < count)
        def _():
            lane_c = lax.broadcasted_iota(jnp.int32, (chunk, LANES), 1)
            slot_r = (lax.broadcasted_iota(jnp.int32, (chunk, tile), 0) + start).astype(F32)
            slot_c = (lax.broadcasted_iota(jnp.int32, (tile, chunk), 1) + start).astype(F32)
            gather = jnp.where((rank_row == slot_r) & (ind_row > 0.0), 1.0, 0.0).astype(BF16)
            scatter = jnp.where((rank[:, 0:1] == slot_c) & member, 1.0, 0.0).astype(BF16)
            xg = _bdot(gather, hn_ref[...]).astype(BF16)
            wg = sum(_bdot(gather, pc) for pc in pieces)
            yg = jnp.zeros((chunk, D_MODEL), F32)
            for e in range(EXPERTS_PER_GROUP):
                h = _bdot(xg, wgu_ref[0, e])
                gate = h[:, :D_EXPERT]
                act = (gate * _sigmoid(gate) * h[:, D_EXPERT:]).astype(BF16)
                w_e = jnp.sum(jnp.where(lane_c == g * EXPERTS_PER_GROUP + e, wg, 0.0), axis=1, keepdims=True)
                yg = yg + _bdot(act, wd_ref[0, e]) * w_e
            acc[...] += _bdot(scatter, yg.astype(BF16))

    @pl.when(g == N_GROUPS - 1)
    def _():
        out_ref[...] = acc[...]


def _moe(hn, rw, wgu, wd, tile):
    n = hn.shape[0]
    chunks = [(0, min(MOE_CHUNK, tile))]
    while sum(chunks[-1]) < tile:
        chunks.append((sum(chunks[-1]), MOE_TAIL_CHUNK))
    n_tiles = n // tile
    grp_w = rw[:, :N_EXPERTS].reshape(n_tiles, tile, N_GROUPS, EXPERTS_PER_GROUP).sum(-1)
    counts = (grp_w > 0.0).sum(1).astype(jnp.int32).reshape(-1)
    t = jnp.arange(tile)
    tri = (t[None, :] < t[:, None]).astype(BF16)
    kern = functools.partial(_moe_kernel, tile=tile, chunks=tuple(chunks))
    row = lambda c: pl.BlockSpec((tile, c), lambda i, g, cnt: (i, 0))
    grid_spec = pltpu.PrefetchScalarGridSpec(
        num_scalar_prefetch=1,
        grid=(n_tiles, N_GROUPS),
        in_specs=[row(D_MODEL), row(LANES),
                  pl.BlockSpec((1, EXPERTS_PER_GROUP, D_MODEL, 2 * D_EXPERT), lambda i, g, cnt: (g, 0, 0, 0)),
                  pl.BlockSpec((1, EXPERTS_PER_GROUP, D_EXPERT, D_MODEL), lambda i, g, cnt: (g, 0, 0, 0)),
                  pl.BlockSpec((tile, tile), lambda i, g, cnt: (0, 0))],
        out_specs=row(D_MODEL),
        scratch_shapes=[pltpu.VMEM((tile, D_MODEL), F32)],
    )
    return pl.pallas_call(
        kern,
        grid_spec=grid_spec,
        out_shape=jax.ShapeDtypeStruct((n, D_MODEL), F32),
        compiler_params=_params("arbitrary", "arbitrary"),
        name="moe_grouped",
    )(counts, hn, rw, wgu, wd, tri)


def _final_norm_kernel(x_ref, m_ref, g_ref, y_ref):
    y_ref[...] = _rms(x_ref[...] + m_ref[...], g_ref[...])


def _final_norm(x1, moe_out, gn, tm):
    n = x1.shape[0]
    row = pl.BlockSpec((tm, D_MODEL), lambda i: (i, 0))
    return pl.pallas_call(
        _final_norm_kernel,
        grid=(n // tm,),
        in_specs=[row, row, pl.BlockSpec((1, D_MODEL), lambda i: (0, 0))],
        out_specs=row,
        out_shape=jax.ShapeDtypeStruct((n, D_MODEL), F32),
        compiler_params=_params("arbitrary"),
        name="final_norm",
    )(x1, moe_out, gn)


def _pad_rw_cols(m):
    z = jnp.zeros(m.shape[:-1] + (LANES - LORA_W,), m.dtype)
    c = 3 * W_MIX
    return jnp.concatenate([m[..., :c], m[..., c:c + LORA_W], z, m[..., c + LORA_W:c + 2 * LORA_W], z,
                            m[..., c + 2 * LORA_W:]], axis=-1)


def _unpad_rw_cols(m):
    c = 3 * W_MIX
    return jnp.concatenate([m[..., :c], m[..., c:c + LORA_W], m[..., c + LANES:c + LANES + LORA_W],
                            m[..., c + 2 * LANES:]], axis=-1)


def _pad_rows(m, rows):
    return jnp.concatenate([m, jnp.zeros((rows - m.shape[0],) + m.shape[1:], m.dtype)], axis=0)


def _prep_weights(l, norm_mix, w_in, b_f, mu_rw, w0, w2, a0, a2, g2, k_k, k_a, r_k, lnx_w, lnx_b,
                  p_a, p_b, w_o, norm_ffn, w_grp, w_exp, we_gate, we_up, we_down):
    w = w_in[l]
    w_fl = jnp.concatenate([w[:, 3 * W_MIX:FOX_COLS], jnp.zeros((D_MODEL, LANES - N_HEADS), F32)], axis=1)
    w_rw = _pad_rw_cols(w[:, FOX_COLS:FOX_COLS + RW_COLS])
    row = lambda vct: vct.reshape(1, -1)
    head_id = jnp.arange(W_MIX) // D_HEAD
    return dict(
        gm=row(norm_mix[l]),
        w_a=jnp.concatenate([w[:, :3 * W_MIX], w_fl, w_rw], axis=1).astype(BF16),
        w_vt=w[:, 2 * W_MIX:3 * W_MIX].T.astype(BF16),
        bf=jnp.concatenate([b_f[l], jnp.zeros((LANES - N_HEADS,), F32)]).reshape(1, LANES),
        wg=w[:, FOX_COLS + RW_COLS:].astype(BF16),
        vecs=(row(_pad_rw_cols(mu_rw[l])), row(w0[l]), row(a0[l]), row(k_k[l]), row(k_a[l]),
              row(r_k[l].reshape(-1)), row(lnx_w[l]), row(lnx_b[l])),
        w2p=_pad_rows(w2[l], LANES).astype(BF16),
        a2p=_pad_rows(a2[l], LANES).astype(BF16),
        g2=g2[l].astype(BF16),
        e_heads=(head_id[:, None] == head_id[None, :]).astype(BF16),
        pa=p_a[l].astype(BF16), pb=p_b[l].astype(BF16), wo=w_o[l].astype(BF16),
        gf=row(norm_ffn[l]),
        wr=jnp.concatenate([w_exp[l], w_grp[l], jnp.zeros((D_MODEL, LANES - N_EXPERTS - N_GROUPS), F32)],
                           axis=1).astype(BF16),
        wgu=jnp.concatenate([we_gate[l], we_up[l]], axis=-1).astype(BF16).reshape(
            N_GROUPS, EXPERTS_PER_GROUP, D_MODEL, 2 * D_EXPERT),
        wd=we_down[l].astype(BF16).reshape(N_GROUPS, EXPERTS_PER_GROUP, D_EXPERT, D_MODEL),
    )


def _layer(x, past, shift0, wkv0, page_table, p, gn, tiles):
    b, t, _ = x.shape
    n = b * t
    tm, tq, tt, tm_moe = tiles
    x2 = x.reshape(n, D_MODEL)
    to3 = lambda m: m.reshape(b, t, m.shape[-1])
    if past is None:
        qb, k, v, kb, vb, lf, rw, vt = _inproj(x2, p["gm"], p["w_a"], p["w_vt"], p["bf"], tm,
                                               QK_SCALE * LOG2E, seq_for_vt=t)
        qa, ka = _qk_aug(to3(lf), to3(qb), to3(kb), tq)
        o_a = _fox_prompt(qa, ka, vt, tq)
        t_pad = t
        u3 = to3(rw)
    else:
        qb, k, v, kb, vb, lf, rw = _inproj(x2, p["gm"], p["w_a"], p["w_vt"], p["bf"], tm, QK_SCALE)
        pool_k, pool_v, pool_lf_t = past
        pad_keys = lambda m: jnp.concatenate([to3(m), jnp.zeros((b, PAGE - t, W_MIX), BF16)], axis=1)
        lfn_t = jnp.concatenate([to3(lf)[:, :, :N_HEADS].transpose(0, 2, 1),
                                 jnp.zeros((b, N_HEADS, PAGE - t), F32)], axis=2)
        o_a = _fox_sample(page_table, to3(qb).astype(F32), pad_keys(kb), pad_keys(vb), lfn_t,
                          pool_k, pool_v, pool_lf_t, t).astype(BF16)
        t_pad = tt
        u3 = jnp.concatenate([to3(rw), jnp.zeros((b, t_pad - t, RW_PAD), F32)], axis=1)
    o_b, wkv_t = _rwkv(u3, shift0, jnp.swapaxes(wkv0, -1, -2), p["vecs"], p["w2p"], p["a2p"], p["g2"],
                       p["e_heads"], tt, t)
    o_b = o_b[:, :t].reshape(n, W_MIX)
    wkv_new = jnp.swapaxes(wkv_t, -1, -2)
    x1, hn, route = _mix_route(x2, o_a.reshape(n, W_MIX), o_b, p["gm"], p["wg"], p["pa"], p["pb"], p["wo"],
                               p["gf"], p["wr"], tm)
    y = _final_norm(x1, _moe(hn, route, p["wgu"], p["wd"], tm_moe), gn, tm)
    shift_new = _unpad_rw_cols(to3(rw)[:, t - 1, :])
    return (y.reshape(b, t, D_MODEL), k.reshape(b, t, N_HEADS, D_HEAD), v.reshape(b, t, N_HEADS, D_HEAD),
            to3(lf)[:, :, :N_HEADS], wkv_new, shift_new)


def kernel(x_prompt, x_sample, cache_k, cache_v, cache_logf, page_table, state_wkv, state_shift, norm_mix, w_in, b_f, mu_rw, w0, w2, a0, a2, g2, k_k, k_a, r_k, lnx_w, lnx_b, p_a, p_b, w_o, norm_ffn, w_grp, w_exp, we_gate, we_up, we_down, norm_final):
    depth = w_in.shape[0]
    assert depth == 1, "final norm is fused into the layer's last kernel"
    bp, tp, _ = x_prompt.shape
    bs, ts, _ = x_sample.shape
    n_phys = cache_k.shape[1]
    gn = norm_final.reshape(1, D_MODEL)
    l = 0
    p = _prep_weights(l, norm_mix, w_in, b_f, mu_rw, w0, w2, a0, a2, g2, k_k, k_a, r_k, lnx_w, lnx_b,
                      p_a, p_b, w_o, norm_ffn, w_grp, w_exp, we_gate, we_up, we_down)
    tq = min(512, tp)
    prompt = _layer(x_prompt, None, jnp.zeros((bp, 1, RW_PAD), F32),
                    jnp.zeros((bp, N_HEADS, D_HEAD, D_HEAD), F32), None, p, gn,
                    (min(512, bp * tp), tq, min(256, tp), min(1024, bp * tp)))
    past = (cache_k[l].transpose(0, 2, 3, 1).reshape(n_phys, W_MIX, PAGE),
            cache_v[l].transpose(0, 2, 3, 1).reshape(n_phys, W_MIX, PAGE),
            cache_logf[l].transpose(0, 2, 1))
    sample = _layer(x_sample, past, _pad_rw_cols(state_shift[l])[:, None, :], state_wkv[l], page_table, p, gn,
                    (bs * ts, None, CHUNK, bs * ts))
    outs = []
    for y, k, v, lf, wkv, sh in (prompt, sample):
        outs.append((y, k[None], v[None], lf[None], wkv[None], sh[None]))
    (yp, kp, vp, lp, wp, sp), (ys, ks, vs, ls, ws, ss) = outs
    return (yp, ys, kp, vp, lp, wp, sp, ks, vs, ls, ws, ss)
```

```python
import functools

import numpy as np
import jax
import jax.numpy as jnp
from jax import lax
from jax.experimental import pallas as pl
from jax.experimental.pallas import tpu as pltpu

F32 = jnp.float32
BF16 = jnp.bfloat16

D_MODEL = 1024
N_HEADS = 8
D_HEAD = 64
W_MIX = N_HEADS * D_HEAD
LORA_W = 64
LORA_G = 128
RW_COLS = 3 * W_MIX + 2 * LORA_W + LORA_G
RW_PAD = 3 * W_MIX + 3 * 128
FOX_COLS = 3 * W_MIX + N_HEADS
N_EXPERTS = 32
EXPERTS_PER_GROUP = 8
N_GROUPS = 4
D_EXPERT = 256
PAGE = 128
MAX_PAGES_PER_STEP = 16
NORM_EPS = 1e-6
GN_EPS = 64e-5
QK_SCALE = D_HEAD ** -0.5
LANES = 128
CHUNK = 64
EXP_NEG_HALF = 0.6065306597126334
NEG = -1e30
VMEM_LIMIT = 56 * 1024 * 1024


def _dot(a, b):
    return jnp.dot(a.astype(BF16), b.astype(BF16), preferred_element_type=F32)


def _bdot(a, b):
    return jnp.dot(a, b, preferred_element_type=F32)


def _bdot_nt(a, b):
    return lax.dot_general(a, b, (((1,), (1,)), ((), ())), preferred_element_type=F32)


def _each(f, *lists):
    return [f(*xs) for xs in zip(*lists)]


def _split3(x):
    hi = x.astype(BF16)
    r1 = x - hi.astype(F32)
    mid = r1.astype(BF16)
    lo = (r1 - mid.astype(F32)).astype(BF16)
    return hi, mid, lo


def _dot_sel_lhs(sel, x):
    hi, mid, lo = _split3(x)
    d = lambda p: jnp.dot(sel, p, preferred_element_type=F32)
    return d(hi) + d(mid) + d(lo)


def _dot_sel_rhs(x, sel):
    hi, mid, lo = _split3(x)
    d = lambda p: jnp.dot(p, sel, preferred_element_type=F32)
    return d(hi) + d(mid) + d(lo)


def _sigmoid(x):
    return 0.5 * jnp.tanh(0.5 * x) + 0.5


def _log_sigmoid(x):
    return jnp.minimum(x, 0.0) - jnp.log(1.0 + jnp.exp(-jnp.abs(x)))


def _rms(x, g):
    return (x * lax.rsqrt(jnp.mean(x * x, axis=-1, keepdims=True) + NORM_EPS)) * g


def _params(*sem):
    return pltpu.CompilerParams(dimension_semantics=sem, vmem_limit_bytes=VMEM_LIMIT)


def _inproj_kernel(x_ref, g_ref, w_ref, wkvt_ref, bf_ref, q_ref, kb_ref, lf_ref, rw_ref, *kv_refs,
                   q_scale, seq_major_kv):
    xb = _rms(x_ref[...], g_ref[...]).astype(BF16)
    d = lambda lo, hi: jnp.dot(xb, w_ref[:, lo:hi], preferred_element_type=F32)
    q_ref[...] = (d(0, W_MIX) * q_scale).astype(BF16)
    k = d(W_MIX, 2 * W_MIX)
    kb_ref[...] = k.astype(BF16)
    lf_ref[...] = _log_sigmoid(d(3 * W_MIX, 3 * W_MIX + LANES) + bf_ref[...])
    rw_ref[...] = d(3 * W_MIX + LANES, 3 * W_MIX + LANES + RW_PAD)
    k_ref, v_ref, vb_ref = kv_refs
    if seq_major_kv:
        k_ref[0] = _bdot_nt(wkvt_ref[0:W_MIX, :], xb)
        vt = _bdot_nt(wkvt_ref[W_MIX:2 * W_MIX, :], xb)
        v_ref[0] = vt
        vb_ref[0] = vt.astype(BF16)
    else:
        v = d(2 * W_MIX, 3 * W_MIX)
        k_ref[...] = k
        v_ref[...] = v
        vb_ref[...] = v.astype(BF16)


def _inproj(x2, g, w_a, w_kvt, bf_pad, tm, q_scale, seq=None):
    n = x2.shape[0]
    wcols = w_a.shape[1]
    row = lambda c: pl.BlockSpec((tm, c), lambda i: (i, 0))
    full = lambda r, c: pl.BlockSpec((r, c), lambda i: (0, 0))
    out_specs = [row(W_MIX), row(W_MIX), row(LANES), row(RW_PAD)]
    out_shape = [jax.ShapeDtypeStruct((n, W_MIX), BF16), jax.ShapeDtypeStruct((n, W_MIX), BF16),
                 jax.ShapeDtypeStruct((n, LANES), F32), jax.ShapeDtypeStruct((n, RW_PAD), F32)]
    if seq is not None:
        per_seq = seq // tm
        kv_spec = pl.BlockSpec((1, W_MIX, tm), lambda i: (i // per_seq, 0, i % per_seq))
        kv_shape = lambda dt: jax.ShapeDtypeStruct((n // seq, W_MIX, seq), dt)
    else:
        kv_spec = row(W_MIX)
        kv_shape = lambda dt: jax.ShapeDtypeStruct((n, W_MIX), dt)
    out_specs += [kv_spec] * 3
    out_shape += [kv_shape(F32), kv_shape(F32), kv_shape(BF16)]
    return pl.pallas_call(
        functools.partial(_inproj_kernel, q_scale=q_scale, seq_major_kv=seq is not None),
        grid=(n // tm,),
        in_specs=[row(D_MODEL), full(1, D_MODEL), full(D_MODEL, wcols), full(2 * W_MIX, D_MODEL), full(1, LANES)],
        out_specs=out_specs,
        out_shape=out_shape,
        compiler_params=_params("arbitrary"),
        name="inproj",
    )(x2, g, w_a, w_kvt, bf_pad)


PAIR_W = 2 * D_HEAD
N_PAIRS = N_HEADS // 2
QK_W = PAIR_W + LANES
PAIRS_PER_STEP = 2
HEADS_PER_STEP = 2 * PAIRS_PER_STEP
KEY_BLK = 32
LOG2E = 1.4426950408889634


def _bias_tables():
    pq = np.zeros((3, LANES, N_PAIRS * LANES), np.float32)
    pk = np.zeros((3, LANES, N_PAIRS * LANES), np.float32)
    one_q = np.zeros((1, N_PAIRS * LANES), np.float32)
    one_k = np.zeros((1, N_PAIRS * LANES), np.float32)
    for h in range(N_HEADS):
        base = LANES * (h // 2)
        for piece in range(3):
            pq[piece, h, base + 3 * (h % 2) + piece] = 1.0
            pk[piece, h, base + 6 + 3 * (h % 2) + piece] = -1.0
    for g in range(N_PAIRS):
        one_q[0, LANES * g + 6:LANES * g + 12] = 1.0
        one_k[0, LANES * g + 0:LANES * g + 6] = 1.0
    return (jnp.asarray(pq, BF16), jnp.asarray(pk, BF16), jnp.asarray(one_q), jnp.asarray(one_k))


def _qk_aug_kernel(lf_ref, q_ref, k_ref, pq_ref, pk_ref, oq_ref, ok_ref, qa_ref, ka_ref, carry):
    @pl.when(pl.program_id(1) == 0)
    def _():
        carry[...] = jnp.zeros_like(carry)

    lf = lf_ref[0] * LOG2E
    tm = lf.shape[0]
    row = lax.broadcasted_iota(jnp.int32, (tm, tm), 0)
    col = lax.broadcasted_iota(jnp.int32, (tm, tm), 1)
    c = _dot_sel_lhs((col <= row).astype(BF16), lf) + carry[...]
    carry[...] = c[tm - 1:tm, :]
    pieces = _split3(c)
    aug_q = oq_ref[...] + sum(_bdot(pc, pq_ref[i]) for i, pc in enumerate(pieces))
    aug_k = ok_ref[...] + sum(_bdot(pc, pk_ref[i]) for i, pc in enumerate(pieces))
    q = q_ref[0]
    k = k_ref[0]
    for g in range(N_PAIRS):
        qa_ref[0, :, g * QK_W:g * QK_W + PAIR_W] = q[:, g * PAIR_W:(g + 1) * PAIR_W]
        qa_ref[0, :, g * QK_W + PAIR_W:(g + 1) * QK_W] = aug_q[:, g * LANES:(g + 1) * LANES].astype(BF16)
        ka_ref[0, :, g * QK_W:g * QK_W + PAIR_W] = k[:, g * PAIR_W:(g + 1) * PAIR_W]
        ka_ref[0, :, g * QK_W + PAIR_W:(g + 1) * QK_W] = aug_k[:, g * LANES:(g + 1) * LANES].astype(BF16)


def _qk_aug(lf3, qb, kb, tm):
    b, t, _ = lf3.shape
    pq, pk, one_q, one_k = _bias_tables()
    blk = lambda c: pl.BlockSpec((1, tm, c), lambda i, j: (i, j, 0))
    cst = lambda a: pl.BlockSpec(a.shape, lambda i, j: (0,) * a.ndim)
    return pl.pallas_call(
        _qk_aug_kernel,
        grid=(b, t // tm),
        in_specs=[blk(LANES), blk(W_MIX), blk(W_MIX), cst(pq), cst(pk), cst(one_q), cst(one_k)],
        out_specs=[blk(N_PAIRS * QK_W), blk(N_PAIRS * QK_W)],
        out_shape=[jax.ShapeDtypeStruct((b, t, N_PAIRS * QK_W), BF16)] * 2,
        scratch_shapes=[pltpu.VMEM((1, LANES), F32)],
        compiler_params=_params("arbitrary", "arbitrary"),
        name="logf_cumsum_qk_aug",
    )(lf3, qb, kb, pq, pk, one_q, one_k)


def _fox_prompt_kernel(qi_ref, kj_ref, q_ref, k_ref, vt_ref, o_ref, qm, m_s, l_s, acc, s_scr, p_scr,
                       *, tq, tk):
    step = pl.program_id(2)
    i = qi_ref[step]
    j = kj_ref[step]

    @pl.when(j == 0)
    def _():
        lane = lax.broadcasted_iota(jnp.int32, (tq, QK_W), 1)
        off = lane - PAIR_W
        for hh in range(HEADS_PER_STEP):
            pr, h = divmod(hh, 2)
            q = q_ref[0, :, pr * QK_W:(pr + 1) * QK_W].astype(F32)
            keep = ((lane >= h * D_HEAD) & (lane < (h + 1) * D_HEAD)) \
                | ((off >= 3 * h) & (off < 3 * h + 3)) | ((off >= 6 + 3 * h) & (off < 9 + 3 * h))
            qm[hh] = jnp.where(keep, q, 0.0).astype(BF16)
        m_s[...] = jnp.full_like(m_s, NEG)
        l_s[...] = jnp.zeros_like(l_s)
        acc[...] = jnp.zeros_like(acc)

    def block(diag):
        key = lax.broadcasted_iota(jnp.int32, (KEY_BLK, tq), 0)
        qry = lax.broadcasted_iota(jnp.int32, (KEY_BLK, tq), 1)
        n_kb = tk // KEY_BLK
        groups = lambda t: [t[8 * r:8 * (r + 1), :] for r in range(KEY_BLK // 8)]
        ones = jnp.ones((16, tk), BF16)

        def scores(h, kb):
            s = s_scr[h, kb * KEY_BLK:(kb + 1) * KEY_BLK, :]
            if diag:
                s = jnp.where(key + kb * KEY_BLK <= qry, s, NEG)
            return s

        for h in range(HEADS_PER_STEP):
            pr = h // 2
            s_scr[h] = _bdot_nt(k_ref[0, :, pr * QK_W:(pr + 1) * QK_W], qm[h])
        for h in range(HEADS_PER_STEP):
            m8 = jnp.full((8, tq), NEG, F32)
            for kb in range(n_kb):
                for g in groups(scores(h, kb)):
                    m8 = jnp.maximum(m8, g)
            m_prev = m_s[h]
            m_new = jnp.maximum(m_prev, jnp.max(m8, axis=0, keepdims=True))
            alpha = jnp.exp2(m_prev - m_new)
            for kb in range(n_kb):
                p_scr[h, kb * KEY_BLK:(kb + 1) * KEY_BLK, :] = jnp.exp2(scores(h, kb) - m_new).astype(BF16)
            m_s[h] = m_new
            vt = jnp.concatenate([vt_ref[0, h * D_HEAD:(h + 1) * D_HEAD, :], ones], axis=0)
            pv = _bdot(vt, p_scr[h])
            acc[h] = alpha * acc[h] + pv[:D_HEAD]
            l_s[h] = alpha * l_s[h] + pv[D_HEAD:D_HEAD + 1]

    @pl.when(j < i)
    def _():
        block(False)

    @pl.when(j == i)
    def _():
        block(True)
        o_t = jnp.concatenate([acc[h] / l_s[h] for h in range(HEADS_PER_STEP)], axis=0)
        o_ref[0] = o_t.T.astype(o_ref.dtype)


def _fox_prompt(qa, ka, vt, tq):
    b, _, t = vt.shape
    tk = tq
    nq = t // tq
    qi = np.array([i for i in range(nq) for _ in range(i + 1)], np.int32)
    kj = np.array([j for i in range(nq) for j in range(i + 1)], np.int32)
    kern = functools.partial(_fox_prompt_kernel, tq=tq, tk=tk)
    grid_spec = pltpu.PrefetchScalarGridSpec(
        num_scalar_prefetch=2,
        grid=(b, N_PAIRS // PAIRS_PER_STEP, len(qi)),
        in_specs=[
            pl.BlockSpec((1, tq, PAIRS_PER_STEP * QK_W), lambda bi, g, s, qi_, kj_: (bi, qi_[s], g)),
            pl.BlockSpec((1, tk, PAIRS_PER_STEP * QK_W), lambda bi, g, s, qi_, kj_: (bi, kj_[s], g)),
            pl.BlockSpec((1, PAIRS_PER_STEP * PAIR_W, tk), lambda bi, g, s, qi_, kj_: (bi, g, kj_[s])),
        ],
        out_specs=pl.BlockSpec((1, tq, PAIRS_PER_STEP * PAIR_W), lambda bi, g, s, qi_, kj_: (bi, qi_[s], g)),
        scratch_shapes=[pltpu.VMEM((HEADS_PER_STEP, tq, QK_W), BF16), pltpu.VMEM((HEADS_PER_STEP, 1, tq), F32),
                        pltpu.VMEM((HEADS_PER_STEP, 1, tq), F32), pltpu.VMEM((HEADS_PER_STEP, D_HEAD, tq), F32),
                        pltpu.VMEM((HEADS_PER_STEP, tk, tq), F32), pltpu.VMEM((HEADS_PER_STEP, tk, tq), BF16)],
    )
    return pl.pallas_call(
        kern,
        grid_spec=grid_spec,
        out_shape=jax.ShapeDtypeStruct((b, t, W_MIX), BF16),
        compiler_params=_params("arbitrary", "arbitrary", "arbitrary"),
        name="fox_prompt",
    )(jnp.asarray(qi), jnp.asarray(kj), qa, ka, vt)


def _fox_sample_kernel(pt_ref, q_ref, kn_ref, vn_ref, lfn_ref, *rest, n_new, n_steps, n_grp):
    del pt_ref
    kp_refs, vp_refs, lfp_refs = rest[:n_grp], rest[n_grp:2 * n_grp], rest[2 * n_grp:3 * n_grp]
    o_ref, qm, m_s, l_s, acc, carry, cq_s = rest[3 * n_grp:]
    j = pl.program_id(1)
    rows = n_new * N_HEADS
    lane8 = lax.broadcasted_iota(jnp.int32, (N_HEADS, W_MIX), 1)
    head8 = lax.broadcasted_iota(jnp.int32, (N_HEADS, W_MIX), 0)
    head_sel = (lane8 >> 6) == head8
    kr = lax.broadcasted_iota(jnp.int32, (PAGE, PAGE), 0)
    kc = lax.broadcasted_iota(jnp.int32, (PAGE, PAGE), 1)
    tile = lambda x: jnp.concatenate([x] * n_new, axis=0)

    def update(s, pv):
        m_prev = m_s[...]
        m_new = jnp.maximum(m_prev, jnp.max(s, axis=1, keepdims=True))
        alpha = jnp.exp(m_prev - m_new)
        p = jnp.exp(s - m_new)
        l_s[...] = alpha * l_s[...] + jnp.sum(p, axis=1, keepdims=True)
        acc[...] = alpha * acc[...] + pv(p.astype(BF16))
        m_s[...] = m_new

    @pl.when(j == 0)
    def _():
        q = q_ref[0].astype(F32)
        qrows = [jnp.where(head_sel, jnp.broadcast_to(q[t:t + 1, :], (N_HEADS, W_MIX)), 0.0)
                 for t in range(n_new)]
        qm[...] = jnp.concatenate(qrows, axis=0).astype(BF16)
        m_s[...] = jnp.full_like(m_s, NEG)
        l_s[...] = jnp.zeros_like(l_s)
        acc[...] = jnp.zeros_like(acc)
        carry[...] = jnp.zeros_like(carry)
        cn = _dot_sel_rhs(lfn_ref[0], (kr <= kc).astype(BF16))
        for t in range(n_new):
            cq_s[t * N_HEADS:(t + 1) * N_HEADS, :] = cn[:, t:t + 1]
        s = lax.dot_general(qm[...], kn_ref[0], (((1,), (1,)), ((), ())), preferred_element_type=F32)
        s = s + (cq_s[...] - tile(cn))
        tok = lax.broadcasted_iota(jnp.int32, (rows, PAGE), 0) >> 3
        key = lax.broadcasted_iota(jnp.int32, (rows, PAGE), 1)
        s = jnp.where(key <= tok, s, NEG)
        update(s, lambda p: _bdot(p, vn_ref[0]))

    @pl.when(j > 0)
    def _():
        lfs = [ref[0] for ref in lfp_refs]
        suffix = _dot_sel_rhs(jnp.concatenate(lfs, axis=0), (kr > kc).astype(BF16))
        later = carry[...]
        cq = cq_s[...]
        parts = []
        for g in range(n_grp):
            bias8 = suffix[g * N_HEADS:(g + 1) * N_HEADS, :] + later
            s_g = _bdot(qm[...], kp_refs[g][0].astype(BF16))
            parts.append(s_g + (cq + tile(bias8)))
            later = later + jnp.sum(lfs[g], axis=1, keepdims=True)
        carry[...] = later
        s = jnp.concatenate(parts, axis=1)

        def pv(p):
            out = _bdot_nt(p[:, 0:PAGE], vp_refs[0][0].astype(BF16))
            for g in range(1, n_grp):
                out = out + _bdot_nt(p[:, g * PAGE:(g + 1) * PAGE], vp_refs[g][0].astype(BF16))
            return out

        update(s, pv)

    @pl.when(j == n_steps - 1)
    def _():
        o = acc[...] / l_s[...]
        for t in range(n_new):
            ot = jnp.where(head_sel, o[t * N_HEADS:(t + 1) * N_HEADS, :], 0.0)
            o_ref[0, t:t + 1, :] = jnp.sum(ot, axis=0, keepdims=True).astype(o_ref.dtype)


def _fox_sample(page_table, qb, kn, vn, lfn_t, pool_k, pool_v, pool_lf_t, n_new):
    r, n_pages = page_table.shape
    rows = n_new * N_HEADS
    n_grp = max(g for g in range(1, MAX_PAGES_PER_STEP + 1) if n_pages % g == 0)
    n_steps = n_pages // n_grp + 1
    kern = functools.partial(_fox_sample_kernel, n_new=n_new, n_steps=n_steps, n_grp=n_grp)

    def page(g):
        return lambda ri, j, pt: (pt[ri, n_pages - 1 - ((jnp.maximum(j, 1) - 1) * n_grp + g)], 0, 0)

    req = lambda ri, j, pt: (ri, 0, 0)
    grid_spec = pltpu.PrefetchScalarGridSpec(
        num_scalar_prefetch=1,
        grid=(r, n_steps),
        in_specs=[
            pl.BlockSpec((1, n_new, W_MIX), req),
            pl.BlockSpec((1, PAGE, W_MIX), req),
            pl.BlockSpec((1, PAGE, W_MIX), req),
            pl.BlockSpec((1, N_HEADS, PAGE), req),
        ] + [pl.BlockSpec((1, W_MIX, PAGE), page(g)) for g in range(n_grp)]
          + [pl.BlockSpec((1, W_MIX, PAGE), page(g)) for g in range(n_grp)]
          + [pl.BlockSpec((1, N_HEADS, PAGE), page(g)) for g in range(n_grp)],
        out_specs=pl.BlockSpec((1, n_new, W_MIX), req),
        scratch_shapes=[pltpu.VMEM((rows, W_MIX), BF16), pltpu.VMEM((rows, 1), F32),
                        pltpu.VMEM((rows, 1), F32), pltpu.VMEM((rows, W_MIX), F32),
                        pltpu.VMEM((N_HEADS, 1), F32), pltpu.VMEM((rows, 1), F32)],
    )
    return pl.pallas_call(
        kern,
        grid_spec=grid_spec,
        out_shape=jax.ShapeDtypeStruct((r, n_new, W_MIX), F32),
        compiler_params=_params("arbitrary", "arbitrary"),
        name="fox_sample",
    )(page_table, qb, kn, vn, lfn_t, *([pool_k] * n_grp), *([pool_v] * n_grp), *([pool_lf_t] * n_grp))


HEADS_PER_ITER = 8


def _rwkv_kernel(u_ref, sh_ref, s0_ref, mu_ref, w0_ref, w2_ref, a0_ref, a2_ref, g2_ref, kk_ref, ka_ref,
                 rk_ref, lw_ref, lb_ref, e_ref, msk_ref, o_ref, sout_ref,
                 carry, st, kt_s, rt_s, kh_s, bh_s, k0_s, r0_s, khp_s, bhp_s, v_s, gt_s, o_s,
                 *, tt, t_real, t_total):
    ti = pl.program_id(1)
    nch = tt // CHUNK

    @pl.when(ti == 0)
    def _():
        carry[...] = sh_ref[0]
        st[...] = s0_ref[0]

    u = u_ref[0]
    row1 = lax.broadcasted_iota(jnp.int32, (tt, 1), 0)
    u_prev = jnp.where(row1 == 0, carry[...], pltpu.roll(u, 1, 0))
    carry[...] = u[tt - 1:tt, :]
    us = u + mu_ref[...] * (u_prev - u)
    r = us[:, 0:W_MIX]
    k = us[:, W_MIX:2 * W_MIX]
    v = us[:, 2 * W_MIX:3 * W_MIX]
    wl = us[:, 3 * W_MIX:3 * W_MIX + LANES]
    al = us[:, 3 * W_MIX + LANES:3 * W_MIX + 2 * LANES]
    gl = us[:, 3 * W_MIX + 2 * LANES:3 * W_MIX + 3 * LANES]
    z = w0_ref[...] + _dot(jnp.tanh(wl), w2_ref[...])
    logd = -EXP_NEG_HALF * _sigmoid(z)
    a = _sigmoid(a0_ref[...] + _dot(al, a2_ref[...]))
    g = _dot(_sigmoid(gl), g2_ref[...])
    e_heads = e_ref[...]
    kk = k * kk_ref[...]
    kk = kk * lax.rsqrt(jnp.maximum(_dot(kk * kk, e_heads), 1e-24))
    k_mod = k * (1.0 + (a - 1.0) * ka_ref[...])
    b = kk * a
    bonus = _dot(r * k_mod * rk_ref[...], e_heads) * v
    if t_real < t_total:
        valid = (ti * tt + row1) < t_real
        zero = lambda t: jnp.where(valid, t, 0.0)
        kk, b, k_mod, v, logd = zero(kk), zero(b), zero(k_mod), zero(v), zero(logd)

    hi, mid, lo = _split3(logd)
    sel3 = lambda m: (_bdot(m, hi) + _bdot(m, mid) + _bdot(m, lo))
    cum = sel3(msk_ref[1].astype(BF16))
    ref = sel3(msk_ref[7].astype(BF16))
    tot = sel3(msk_ref[8].astype(BF16))
    cum_prev = cum - logd
    e_inv = jnp.exp(ref - cum)
    e_tot = jnp.exp(tot - cum)
    wide = dict(kt=kk * jnp.exp(cum_prev - ref), rt=r * jnp.exp(cum - ref), kh=k_mod * e_inv, bh=b * e_inv,
                k0=kk * jnp.exp(cum_prev), r0=r * jnp.exp(cum), khp=k_mod * e_tot, bhp=b * e_tot, v=v,
                gt=jnp.exp(tot))
    dst = dict(kt=kt_s, rt=rt_s, kh=kh_s, bh=bh_s, k0=k0_s, r0=r0_s, khp=khp_s, bhp=bhp_s, v=v_s, gt=gt_s)
    for name, val in wide.items():
        val = val.astype(dst[name].dtype)
        for h in range(N_HEADS):
            dst[name][h] = val[:, h * D_HEAD:(h + 1) * D_HEAD]

    def head_body(p, _):
        hs = [p * HEADS_PER_ITER + i for i in range(HEADS_PER_ITER)]
        ld = lambda s: [s[h] for h in hs]
        ldb = lambda s: [s[h].astype(BF16) for h in hs]
        kt, rt, kh, bh = ldb(kt_s), ldb(rt_s), ldb(kh_s), ldb(bh_s)
        strict = msk_ref[0] != 0.0
        incl = msk_ref[1] != 0.0
        a_kk = _each(lambda x, y: jnp.where(strict, _bdot_nt(x, y), 0.0).astype(BF16), kt, kh)
        a_kb = _each(lambda x, y: jnp.where(strict, _bdot_nt(x, y), 0.0), kt, bh)
        a_rk = _each(lambda x, y: jnp.where(incl, _bdot_nt(x, y), 0.0).astype(BF16), rt, kh)
        a_rb = _each(lambda x, y: jnp.where(incl, _bdot_nt(x, y), 0.0).astype(BF16), rt, bh)
        x = _each(lambda t: (-(t * msk_ref[2])).astype(BF16), a_kb)
        x2 = _each(lambda t: _bdot(t, t).astype(BF16), x)
        pm = _each(lambda t: msk_ref[6] + t.astype(F32), x)
        pm = _each(lambda q, t: q + _bdot(q.astype(BF16), t), pm, x2)
        x4 = _each(lambda t: _bdot(t, t).astype(BF16), x2)
        pm = _each(lambda q, t: q + _bdot(q.astype(BF16), t), pm, x4)
        for lvl in (3, 4, 5):
            pb = _each(lambda q: q.astype(BF16), pm)
            a_off = _each(lambda t: (t * msk_ref[lvl]).astype(BF16), a_kb)
            pa = _each(lambda q, t: _bdot(q, t).astype(BF16), pb, a_off)
            pm = _each(lambda q, s, qb: q - _bdot(s, qb), pm, pa, pb)
        t_inv = _each(lambda q: q.astype(BF16), pm)
        vb = ldb(v_s)
        w1 = _each(lambda t, y: _bdot(t, y), t_inv, ldb(k0_s))
        av = _each(lambda t, y: _bdot(t, y).astype(BF16), a_kk, vb)
        u1 = _each(lambda t, y: _bdot(t, y), t_inv, av)
        w1b = _each(lambda t: t.astype(BF16), w1)
        u1b = _each(lambda t: t.astype(BF16), u1)
        q_t = _each(lambda r0, t, y: (r0 - _bdot(t, y)).astype(BF16), ld(r0_s), a_rb, w1b)
        o1 = _each(lambda t, y, t2, y2: _bdot(t, y) - _bdot(t2, y2), a_rk, vb, a_rb, u1b)
        khp, bhp, gt = ld(khp_s), ld(bhp_s), ld(gt_s)
        for c in range(nch):
            rows = slice(c * CHUNK, (c + 1) * CHUNK)
            bt = _each(lambda t: t[rows].T.astype(BF16), bhp)
            kt_ = _each(lambda t: t[rows].T.astype(BF16), khp)
            gcol = _each(lambda t: t[rows].T[:, 0:1], gt)
            m_low = _each(lambda t, y: _bdot(t, y[rows]).astype(BF16), bt, w1b)
            n1 = _each(lambda t, y, t2, y2: _bdot(t, y[rows]) - _bdot(t2, y2[rows]), kt_, vb, bt, u1b)
            for i, h in enumerate(hs):
                s_t = st[h]
                s_hi = s_t.astype(BF16)
                s_lo = (s_t - s_hi.astype(F32)).astype(BF16)
                qc = q_t[i][rows]
                o_s[h, rows, :] = _bdot(qc, s_hi) + _bdot(qc, s_lo) + o1[i][rows]
                st[h] = gcol[i] * s_t - (_bdot(m_low[i], s_hi) + _bdot(m_low[i], s_lo)) + n1[i]
        return 0

    lax.fori_loop(0, N_HEADS // HEADS_PER_ITER, head_body, 0)

    o = jnp.concatenate([o_s[h] for h in range(N_HEADS)], axis=1)
    mean = _dot(o, e_heads) * (1.0 / D_HEAD)
    xc = o - mean
    var = _dot(xc * xc, e_heads) * (1.0 / D_HEAD)
    on = xc * lax.rsqrt(var + GN_EPS) * lw_ref[...] + lb_ref[...]
    o_ref[0] = ((on + bonus) * g).astype(o_ref.dtype)
    sout_ref[0] = st[...]


def _chunk_masks(tt):
    i = jnp.arange(tt)[:, None]
    j = jnp.arange(tt)[None, :]
    blk = lambda s: (i >> s) == (j >> s)
    same = blk(6)
    ms = [same & (j < i), same & (j <= i), blk(3), blk(4) & ~blk(3), blk(5) & ~blk(4), same & ~blk(5),
          i == j, same & ((j & (CHUNK - 1)) < CHUNK // 2), same]
    return jnp.stack(ms).astype(F32)


def _rwkv(u3, shift0, s0_t, vecs, w2p, a2p, g2, e_heads, tt, t_real):
    b, t, _ = u3.shape
    mu, w0, a0, k_k, k_a, r_k, lnw, lnb = vecs
    masks = _chunk_masks(tt)
    kern = functools.partial(_rwkv_kernel, tt=tt, t_real=t_real, t_total=t)
    vec = lambda c: pl.BlockSpec((1, c), lambda bi, ti: (0, 0))
    mat = lambda r_, c: pl.BlockSpec((r_, c), lambda bi, ti: (0, 0))
    wide = lambda dt: pltpu.VMEM((N_HEADS, tt, D_HEAD), dt)
    return pl.pallas_call(
        kern,
        grid=(b, t // tt),
        in_specs=[
            pl.BlockSpec((1, tt, RW_PAD), lambda bi, ti: (bi, ti, 0)),
            pl.BlockSpec((1, 1, RW_PAD), lambda bi, ti: (bi, 0, 0)),
            pl.BlockSpec((1, N_HEADS, D_HEAD, D_HEAD), lambda bi, ti: (bi, 0, 0, 0)),
            vec(RW_PAD), vec(W_MIX), mat(LANES, W_MIX), vec(W_MIX), mat(LANES, W_MIX), mat(LANES, W_MIX),
            vec(W_MIX), vec(W_MIX), vec(W_MIX), vec(W_MIX), vec(W_MIX), mat(W_MIX, W_MIX),
            pl.BlockSpec(masks.shape, lambda bi, ti: (0, 0, 0)),
        ],
        out_specs=[pl.BlockSpec((1, tt, W_MIX), lambda bi, ti: (bi, ti, 0)),
                   pl.BlockSpec((1, N_HEADS, D_HEAD, D_HEAD), lambda bi, ti: (bi, 0, 0, 0))],
        out_shape=[jax.ShapeDtypeStruct((b, t, W_MIX), BF16),
                   jax.ShapeDtypeStruct((b, N_HEADS, D_HEAD, D_HEAD), F32)],
        scratch_shapes=[pltpu.VMEM((1, RW_PAD), F32), pltpu.VMEM((N_HEADS, D_HEAD, D_HEAD), F32)]
                       + [wide(BF16 if name in ("kt", "rt", "kh", "bh", "k0", "v") else F32)
                          for name in ("kt", "rt", "kh", "bh", "k0", "r0", "khp", "bhp", "v", "gt", "o")],
        compiler_params=_params("arbitrary", "arbitrary"),
        name="rwkv7_chunked",
    )(u3, shift0, s0_t, mu, w0, w2p, a0, a2p, g2, k_k, k_a, r_k, lnw, lnb, e_heads, masks)


def _mix_route_kernel(x_ref, oa_ref, ob_ref, gm_ref, wg_ref, pa_ref, pb_ref, wo_ref, gf_ref, wr_ref,
                      x1_ref, hn_ref, rw_ref):
    x = x_ref[...]
    xb = _rms(x, gm_ref[...]).astype(BF16)
    gates = _sigmoid(jnp.dot(xb, wg_ref[...], preferred_element_type=F32))
    mixed = (gates[:, :D_MODEL] * jnp.dot(oa_ref[...], pa_ref[...], preferred_element_type=F32)
             + gates[:, D_MODEL:] * jnp.dot(ob_ref[...], pb_ref[...], preferred_element_type=F32))
    x1 = x + _dot(mixed, wo_ref[...])
    x1_ref[...] = x1
    hb = _rms(x1, gf_ref[...]).astype(BF16)
    hn_ref[...] = hb
    logits = jnp.dot(hb, wr_ref[...], preferred_element_type=F32)
    tm = logits.shape[0]
    lane = lax.broadcasted_iota(jnp.int32, (tm, LANES), 1)
    lane_f = lane.astype(F32)
    first = lambda hit: jnp.min(jnp.where(hit, lane_f, 1e9), axis=1, keepdims=True)
    is_grp = (lane >= N_EXPERTS) & (lane < N_EXPERTS + N_GROUPS)
    gl = jnp.where(is_grp, logits, NEG)
    gmax = jnp.max(gl, axis=1, keepdims=True)
    g_idx = first(gl == gmax) - float(N_EXPERTS)
    p_grp = 1.0 / jnp.sum(jnp.where(is_grp, jnp.exp(gl - gmax), 0.0), axis=1, keepdims=True)
    in_grp = (lane < N_EXPERTS) & ((lane >> 3).astype(F32) == g_idx)
    el = jnp.where(in_grp, logits, NEG)
    v1 = jnp.max(el, axis=1, keepdims=True)
    i1 = first(el == v1)
    el2 = jnp.where(lane_f == i1, NEG, el)
    v2 = jnp.max(el2, axis=1, keepdims=True)
    i2 = first(el2 == v2)
    e2 = jnp.exp(v2 - v1)
    w1 = p_grp / (1.0 + e2)
    rw_ref[...] = jnp.where(lane_f == i1, w1, 0.0) + jnp.where(lane_f == i2, w1 * e2, 0.0)


def _mix_route(x2, oa, ob, gm, wg, pa, pb, wo, gf, wr, tm):
    n = x2.shape[0]
    row = lambda c: pl.BlockSpec((tm, c), lambda i: (i, 0))
    full = lambda r, c: pl.BlockSpec((r, c), lambda i: (0, 0))
    return pl.pallas_call(
        _mix_route_kernel,
        grid=(n // tm,),
        in_specs=[row(D_MODEL), row(W_MIX), row(W_MIX), full(1, D_MODEL), full(D_MODEL, 2 * D_MODEL),
                  full(W_MIX, D_MODEL), full(W_MIX, D_MODEL), full(D_MODEL, D_MODEL), full(1, D_MODEL),
                  full(D_MODEL, LANES)],
        out_specs=[row(D_MODEL), row(D_MODEL), row(LANES)],
        out_shape=[jax.ShapeDtypeStruct((n, D_MODEL), F32), jax.ShapeDtypeStruct((n, D_MODEL), BF16),
                   jax.ShapeDtypeStruct((n, LANES), F32)],
        compiler_params=_params("arbitrary"),
        name="mix_route",
    )(x2, oa, ob, gm, wg, pa, pb, wo, gf, wr)


MOE_CHUNK = 256
MOE_TAIL_CHUNK = 128


def _moe_kernel(cnt_ref, hn_ref, rw_ref, wgu_ref, wd_ref, tri_ref, out_ref, acc, *, tile, chunks):
    i = pl.program_id(0)
    g = pl.program_id(1)

    @pl.when(g == 0)
    def _():
        acc[...] = jnp.zeros_like(acc)

    rw = rw_ref[...]
    lane = lax.broadcasted_iota(jnp.int32, (tile, LANES), 1)
    in_g = (lane >> 3) == g
    member = jnp.sum(jnp.where(in_g, rw, 0.0), axis=1, keepdims=True) > 0.0
    ind = jnp.broadcast_to(jnp.where(member, 1.0, 0.0), (tile, LANES))
    rank = _bdot(tri_ref[...], ind.astype(BF16))
    ind_row = ind.T[0:1, :]
    rank_row = rank.T[0:1, :]
    count = cnt_ref[i * N_GROUPS + g]
    pieces = _split3(rw)

    for start, chunk in chunks:
        @pl.when(start < count)
        def _():
            lane_c = lax.broadcasted_iota(jnp.int32, (chunk, LANES), 1)
            slot_r = (lax.broadcasted_iota(jnp.int32, (chunk, tile), 0) + start).astype(F32)
            slot_c = (lax.broadcasted_iota(jnp.int32, (tile, chunk), 1) + start).astype(F32)
            gather = jnp.where((rank_row == slot_r) & (ind_row > 0.0), 1.0, 0.0).astype(BF16)
            scatter = jnp.where((rank[:, 0:1] == slot_c) & member, 1.0, 0.0).astype(BF16)
            xg = _bdot(gather, hn_ref[...]).astype(BF16)
            wg = sum(_bdot(gather, pc) for pc in pieces)
            yg = jnp.zeros((chunk, D_MODEL), F32)
            for e in range(EXPERTS_PER_GROUP):
                h = _bdot(xg, wgu_ref[0, e])
                gate = h[:, :D_EXPERT]
                act = (gate * _sigmoid(gate) * h[:, D_EXPERT:]).astype(BF16)
                w_e = jnp.sum(jnp.where(lane_c == g * EXPERTS_PER_GROUP + e, wg, 0.0), axis=1, keepdims=True)
                yg = yg + _bdot(act, wd_ref[0, e]) * w_e
            acc[...] += _bdot(scatter, yg.astype(BF16))

    @pl.when(g == N_GROUPS - 1)
    def _():
        out_ref[...] = acc[...]


def _moe(hn, rw, wgu, wd, tile):
    n = hn.shape[0]
    chunks = [(0, min(MOE_CHUNK, tile))]
    while sum(chunks[-1]) < tile:
        chunks.append((sum(chunks[-1]), MOE_TAIL_CHUNK))
    n_tiles = n // tile
    grp_w = rw[:, :N_EXPERTS].reshape(n_tiles, tile, N_GROUPS, EXPERTS_PER_GROUP).sum(-1)
    counts = (grp_w > 0.0).sum(1).astype(jnp.int32).reshape(-1)
    t = jnp.arange(tile)
    tri = (t[None, :] < t[:, None]).astype(BF16)
    kern = functools.partial(_moe_kernel, tile=tile, chunks=tuple(chunks))
    row = lambda c: pl.BlockSpec((tile, c), lambda i, g, cnt: (i, 0))
    grid_spec = pltpu.PrefetchScalarGridSpec(
        num_scalar_prefetch=1,
        grid=(n_tiles, N_GROUPS),
        in_specs=[row(D_MODEL), row(LANES),
                  pl.BlockSpec((1, EXPERTS_PER_GROUP, D_MODEL, 2 * D_EXPERT), lambda i, g, cnt: (g, 0, 0, 0)),
                  pl.BlockSpec((1, EXPERTS_PER_GROUP, D_EXPERT, D_MODEL), lambda i, g, cnt: (g, 0, 0, 0)),
                  pl.BlockSpec((tile, tile), lambda i, g, cnt: (0, 0))],
        out_specs=row(D_MODEL),
        scratch_shapes=[pltpu.VMEM((tile, D_MODEL), F32)],
    )
    return pl.pallas_call(
        kern,
        grid_spec=grid_spec,
        out_shape=jax.ShapeDtypeStruct((n, D_MODEL), F32),
        compiler_params=_params("arbitrary", "arbitrary"),
        name="moe_grouped",
    )(counts, hn, rw, wgu, wd, tri)


def _final_norm_kernel(x_ref, m_ref, g_ref, y_ref):
    y_ref[...] = _rms(x_ref[...] + m_ref[...], g_ref[...])


def _final_norm(x1, moe_out, gn, tm):
    n = x1.shape[0]
    row = pl.BlockSpec((tm, D_MODEL), lambda i: (i, 0))
    return pl.pallas_call(
        _final_norm_kernel,
        grid=(n // tm,),
        in_specs=[row, row, pl.BlockSpec((1, D_MODEL), lambda i: (0, 0))],
        out_specs=row,
        out_shape=jax.ShapeDtypeStruct((n, D_MODEL), F32),
        compiler_params=_params("arbitrary"),
        name="final_norm",
    )(x1, moe_out, gn)


def _pad_rw_cols(m):
    z = jnp.zeros(m.shape[:-1] + (LANES - LORA_W,), m.dtype)
    c = 3 * W_MIX
    return jnp.concatenate([m[..., :c], m[..., c:c + LORA_W], z, m[..., c + LORA_W:c + 2 * LORA_W], z,
                            m[..., c + 2 * LORA_W:]], axis=-1)


def _unpad_rw_cols(m):
    c = 3 * W_MIX
    return jnp.concatenate([m[..., :c], m[..., c:c + LORA_W], m[..., c + LANES:c + LANES + LORA_W],
                            m[..., c + 2 * LANES:]], axis=-1)


def _pad_rows(m, rows):
    return jnp.concatenate([m, jnp.zeros((rows - m.shape[0],) + m.shape[1:], m.dtype)], axis=0)


def _prep_weights(l, norm_mix, w_in, b_f, mu_rw, w0, w2, a0, a2, g2, k_k, k_a, r_k, lnx_w, lnx_b,
                  p_a, p_b, w_o, norm_ffn, w_grp, w_exp, we_gate, we_up, we_down):
    w = w_in[l]
    w_fl = jnp.concatenate([w[:, 3 * W_MIX:FOX_COLS], jnp.zeros((D_MODEL, LANES - N_HEADS), F32)], axis=1)
    w_rw = _pad_rw_cols(w[:, FOX_COLS:FOX_COLS + RW_COLS])
    row = lambda vct: vct.reshape(1, -1)
    head_id = jnp.arange(W_MIX) // D_HEAD
    return dict(
        gm=row(norm_mix[l]),
        w_a=jnp.concatenate([w[:, :3 * W_MIX], w_fl, w_rw], axis=1).astype(BF16),
        w_kvt=w[:, W_MIX:3 * W_MIX].T.astype(BF16),
        bf=jnp.concatenate([b_f[l], jnp.zeros((LANES - N_HEADS,), F32)]).reshape(1, LANES),
        wg=w[:, FOX_COLS + RW_COLS:].astype(BF16),
        vecs=(row(_pad_rw_cols(mu_rw[l])), row(w0[l]), row(a0[l]), row(k_k[l]), row(k_a[l]),
              row(r_k[l].reshape(-1)), row(lnx_w[l]), row(lnx_b[l])),
        w2p=_pad_rows(w2[l], LANES).astype(BF16),
        a2p=_pad_rows(a2[l], LANES).astype(BF16),
        g2=g2[l].astype(BF16),
        e_heads=(head_id[:, None] == head_id[None, :]).astype(BF16),
        pa=p_a[l].astype(BF16), pb=p_b[l].astype(BF16), wo=w_o[l].astype(BF16),
        gf=row(norm_ffn[l]),
        wr=jnp.concatenate([w_exp[l], w_grp[l], jnp.zeros((D_MODEL, LANES - N_EXPERTS - N_GROUPS), F32)],
                           axis=1).astype(BF16),
        wgu=jnp.concatenate([we_gate[l], we_up[l]], axis=-1).astype(BF16).reshape(
            N_GROUPS, EXPERTS_PER_GROUP, D_MODEL, 2 * D_EXPERT),
        wd=we_down[l].astype(BF16).reshape(N_GROUPS, EXPERTS_PER_GROUP, D_EXPERT, D_MODEL),
    )


def _layer(x, past, shift0, wkv0, page_table, p, gn, tiles):
    b, t, _ = x.shape
    n = b * t
    tm, tq, tt, tm_moe = tiles
    x2 = x.reshape(n, D_MODEL)
    to3 = lambda m: m.reshape(b, t, m.shape[-1])
    if past is None:
        qb, kb, lf, rw, k_t, v_t, vt = _inproj(x2, p["gm"], p["w_a"], p["w_kvt"], p["bf"], tm,
                                               QK_SCALE * LOG2E, seq=t)
        qa, ka = _qk_aug(to3(lf), to3(qb), to3(kb), tq)
        o_a = _fox_prompt(qa, ka, vt, tq)
        t_pad = t
        u3 = to3(rw)
        k, v = (m.reshape(b, N_HEADS, D_HEAD, t).transpose(0, 3, 1, 2) for m in (k_t, v_t))
    else:
        qb, kb, lf, rw, k, v, vb = _inproj(x2, p["gm"], p["w_a"], p["w_kvt"], p["bf"], tm, QK_SCALE)
        pool_k, pool_v, pool_lf_t = past
        pad_keys = lambda m: jnp.concatenate([to3(m), jnp.zeros((b, PAGE - t, W_MIX), BF16)], axis=1)
        lfn_t = jnp.concatenate([to3(lf)[:, :, :N_HEADS].transpose(0, 2, 1),
                                 jnp.zeros((b, N_HEADS, PAGE - t), F32)], axis=2)
        o_a = _fox_sample(page_table, to3(qb).astype(F32), pad_keys(kb), pad_keys(vb), lfn_t,
                          pool_k, pool_v, pool_lf_t, t).astype(BF16)
        t_pad = tt
        u3 = jnp.concatenate([to3(rw), jnp.zeros((b, t_pad - t, RW_PAD), F32)], axis=1)
    o_b, wkv_t = _rwkv(u3, shift0, jnp.swapaxes(wkv0, -1, -2), p["vecs"], p["w2p"], p["a2p"], p["g2"],
                       p["e_heads"], tt, t)
    o_b = o_b[:, :t].reshape(n, W_MIX)
    wkv_new = jnp.swapaxes(wkv_t, -1, -2)
    x1, hn, route = _mix_route(x2, o_a.reshape(n, W_MIX), o_b, p["gm"], p["wg"], p["pa"], p["pb"], p["wo"],
                               p["gf"], p["wr"], tm)
    y = _final_norm(x1, _moe(hn, route, p["wgu"], p["wd"], tm_moe), gn, tm)
    shift_new = _unpad_rw_cols(to3(rw)[:, t - 1, :])
    return (y.reshape(b, t, D_MODEL), k.reshape(b, t, N_HEADS, D_HEAD), v.reshape(b, t, N_HEADS, D_HEAD),
            to3(lf)[:, :, :N_HEADS], wkv_new, shift_new)


def kernel(x_prompt, x_sample, cache_k, cache_v, cache_logf, page_table, state_wkv, state_shift, norm_mix, w_in, b_f, mu_rw, w0, w2, a0, a2, g2, k_k, k_a, r_k, lnx_w, lnx_b, p_a, p_b, w_o, norm_ffn, w_grp, w_exp, we_gate, we_up, we_down, norm_final):
    depth = w_in.shape[0]
    assert depth == 1, "final norm is fused into the layer's last kernel"
    bp, tp, _ = x_prompt.shape
    bs, ts, _ = x_sample.shape
    n_phys = cache_k.shape[1]
    gn = norm_final.reshape(1, D_MODEL)
    l = 0
    p = _prep_weights(l, norm_mix, w_in, b_f, mu_rw, w0, w2, a0, a2, g2, k_k, k_a, r_k, lnx_w, lnx_b,
                      p_a, p_b, w_o, norm_ffn, w_grp, w_exp, we_gate, we_up, we_down)
    tq = min(512, tp)
    prompt = _layer(x_prompt, None, jnp.zeros((bp, 1, RW_PAD), F32),
                    jnp.zeros((bp, N_HEADS, D_HEAD, D_HEAD), F32), None, p, gn,
                    (min(512, bp * tp), tq, min(256, tp), min(1024, bp * tp)))
    past = (cache_k[l].transpose(0, 2, 3, 1).reshape(n_phys, W_MIX, PAGE),
            cache_v[l].transpose(0, 2, 3, 1).reshape(n_phys, W_MIX, PAGE),
            cache_logf[l].transpose(0, 2, 1))
    sample = _layer(x_sample, past, _pad_rw_cols(state_shift[l])[:, None, :], state_wkv[l], page_table, p, gn,
                    (bs * ts, None, CHUNK, bs * ts))
    outs = []
    for y, k, v, lf, wkv, sh in (prompt, sample):
        outs.append((y, k[None], v[None], lf[None], wkv[None], sh[None]))
    (yp, kp, vp, lp, wp, sp), (ys, ks, vs, ls, ws, ss) = outs
    return (yp, ys, kp, vp, lp, wp, sp, ks, vs, ls, ws, ss)
```

```python
import functools

import numpy as np
import jax
import jax.numpy as jnp
from jax import lax
from jax.experimental import pallas as pl
from jax.experimental.pallas import tpu as pltpu

F32 = jnp.float32
BF16 = jnp.bfloat16

D_MODEL = 1024
N_HEADS = 8
D_HEAD = 64
W_MIX = N_HEADS * D_HEAD
LORA_W = 64
LORA_G = 128
RW_COLS = 3 * W_MIX + 2 * LORA_W + LORA_G
RW_PAD = 3 * W_MIX + 3 * 128
FOX_COLS = 3 * W_MIX + N_HEADS
N_EXPERTS = 32
EXPERTS_PER_GROUP = 8
N_GROUPS = 4
D_EXPERT = 256
PAGE = 128
MAX_PAGES_PER_STEP = 16
NORM_EPS = 1e-6
GN_EPS = 64e-5
QK_SCALE = D_HEAD ** -0.5
LANES = 128
CHUNK = 64
EXP_NEG_HALF = 0.6065306597126334
NEG = -1e30
VMEM_LIMIT = 56 * 1024 * 1024


def _dot(a, b):
    return jnp.dot(a.astype(BF16), b.astype(BF16), preferred_element_type=F32)


def _bdot(a, b):
    return jnp.dot(a, b, preferred_element_type=F32)


def _bdot_nt(a, b):
    return lax.dot_general(a, b, (((1,), (1,)), ((), ())), preferred_element_type=F32)


def _each(f, *lists):
    return [f(*xs) for xs in zip(*lists)]


def _split3(x):
    hi = x.astype(BF16)
    r1 = x - hi.astype(F32)
    mid = r1.astype(BF16)
    lo = (r1 - mid.astype(F32)).astype(BF16)
    return hi, mid, lo


def _dot_sel_lhs(sel, x):
    hi, mid, lo = _split3(x)
    d = lambda p: jnp.dot(sel, p, preferred_element_type=F32)
    return d(hi) + d(mid) + d(lo)


def _dot_sel_rhs(x, sel):
    hi, mid, lo = _split3(x)
    d = lambda p: jnp.dot(p, sel, preferred_element_type=F32)
    return d(hi) + d(mid) + d(lo)


def _sigmoid(x):
    return 0.5 * jnp.tanh(0.5 * x) + 0.5


def _log_sigmoid(x):
    return jnp.minimum(x, 0.0) - jnp.log(1.0 + jnp.exp(-jnp.abs(x)))


def _rms(x, g):
    return (x * lax.rsqrt(jnp.mean(x * x, axis=-1, keepdims=True) + NORM_EPS)) * g


def _params(*sem):
    return pltpu.CompilerParams(dimension_semantics=sem, vmem_limit_bytes=VMEM_LIMIT)


def _inproj_kernel(x_ref, g_ref, w_ref, wkvt_ref, bf_ref, q_ref, kb_ref, lf_ref, rw_ref, *kv_refs,
                   q_scale, seq_major_kv):
    xb = _rms(x_ref[...], g_ref[...]).astype(BF16)
    d = lambda lo, hi: jnp.dot(xb, w_ref[:, lo:hi], preferred_element_type=F32)
    q_ref[...] = (d(0, W_MIX) * q_scale).astype(BF16)
    k = d(W_MIX, 2 * W_MIX)
    kb_ref[...] = k.astype(BF16)
    lf_ref[...] = _log_sigmoid(d(3 * W_MIX, 3 * W_MIX + LANES) + bf_ref[...])
    rw_ref[...] = d(3 * W_MIX + LANES, 3 * W_MIX + LANES + RW_PAD)
    k_ref, v_ref, vb_ref = kv_refs
    if seq_major_kv:
        k_ref[0] = _bdot_nt(wkvt_ref[0:W_MIX, :], xb)
        vt = _bdot_nt(wkvt_ref[W_MIX:2 * W_MIX, :], xb)
        v_ref[0] = vt
        vb_ref[0] = vt.astype(BF16)
    else:
        v = d(2 * W_MIX, 3 * W_MIX)
        k_ref[...] = k
        v_ref[...] = v
        vb_ref[...] = v.astype(BF16)


def _inproj(x2, g, w_a, w_kvt, bf_pad, tm, q_scale, seq=None):
    n = x2.shape[0]
    wcols = w_a.shape[1]
    row = lambda c: pl.BlockSpec((tm, c), lambda i: (i, 0))
    full = lambda r, c: pl.BlockSpec((r, c), lambda i: (0, 0))
    out_specs = [row(W_MIX), row(W_MIX), row(LANES), row(RW_PAD)]
    out_shape = [jax.ShapeDtypeStruct((n, W_MIX), BF16), jax.ShapeDtypeStruct((n, W_MIX), BF16),
                 jax.ShapeDtypeStruct((n, LANES), F32), jax.ShapeDtypeStruct((n, RW_PAD), F32)]
    if seq is not None:
        per_seq = seq // tm
        kv_spec = pl.BlockSpec((1, W_MIX, tm), lambda i: (i // per_seq, 0, i % per_seq))
        kv_shape = lambda dt: jax.ShapeDtypeStruct((n // seq, W_MIX, seq), dt)
    else:
        kv_spec = row(W_MIX)
        kv_shape = lambda dt: jax.ShapeDtypeStruct((n, W_MIX), dt)
    out_specs += [kv_spec] * 3
    out_shape += [kv_shape(F32), kv_shape(F32), kv_shape(BF16)]
    return pl.pallas_call(
        functools.partial(_inproj_kernel, q_scale=q_scale, seq_major_kv=seq is not None),
        grid=(n // tm,),
        in_specs=[row(D_MODEL), full(1, D_MODEL), full(D_MODEL, wcols), full(2 * W_MIX, D_MODEL), full(1, LANES)],
        out_specs=out_specs,
        out_shape=out_shape,
        compiler_params=_params("arbitrary"),
        name="inproj",
    )(x2, g, w_a, w_kvt, bf_pad)


PAIR_W = 2 * D_HEAD
N_PAIRS = N_HEADS // 2
QK_W = PAIR_W + LANES
PAIRS_PER_STEP = 2
HEADS_PER_STEP = 2 * PAIRS_PER_STEP
KEY_BLK = 32
LOG2E = 1.4426950408889634


def _bias_tables():
    pq = np.zeros((3, LANES, N_PAIRS * LANES), np.float32)
    pk = np.zeros((3, LANES, N_PAIRS * LANES), np.float32)
    one_q = np.zeros((1, N_PAIRS * LANES), np.float32)
    one_k = np.zeros((1, N_PAIRS * LANES), np.float32)
    for h in range(N_HEADS):
        base = LANES * (h // 2)
        for piece in range(3):
            pq[piece, h, base + 3 * (h % 2) + piece] = 1.0
            pk[piece, h, base + 6 + 3 * (h % 2) + piece] = -1.0
    for g in range(N_PAIRS):
        one_q[0, LANES * g + 6:LANES * g + 12] = 1.0
        one_k[0, LANES * g + 0:LANES * g + 6] = 1.0
    return (jnp.asarray(pq, BF16), jnp.asarray(pk, BF16), jnp.asarray(one_q), jnp.asarray(one_k))


def _qk_aug_kernel(lf_ref, q_ref, k_ref, pq_ref, pk_ref, oq_ref, ok_ref, qa_ref, ka_ref, carry):
    @pl.when(pl.program_id(1) == 0)
    def _():
        carry[...] = jnp.zeros_like(carry)

    lf = lf_ref[0] * LOG2E
    tm = lf.shape[0]
    row = lax.broadcasted_iota(jnp.int32, (tm, tm), 0)
    col = lax.broadcasted_iota(jnp.int32, (tm, tm), 1)
    c = _dot_sel_lhs((col <= row).astype(BF16), lf) + carry[...]
    carry[...] = c[tm - 1:tm, :]
    pieces = _split3(c)
    aug_q = oq_ref[...] + sum(_bdot(pc, pq_ref[i]) for i, pc in enumerate(pieces))
    aug_k = ok_ref[...] + sum(_bdot(pc, pk_ref[i]) for i, pc in enumerate(pieces))
    q = q_ref[0]
    k = k_ref[0]
    for g in range(N_PAIRS):
        qa_ref[0, :, g * QK_W:g * QK_W + PAIR_W] = q[:, g * PAIR_W:(g + 1) * PAIR_W]
        qa_ref[0, :, g * QK_W + PAIR_W:(g + 1) * QK_W] = aug_q[:, g * LANES:(g + 1) * LANES].astype(BF16)
        ka_ref[0, :, g * QK_W:g * QK_W + PAIR_W] = k[:, g * PAIR_W:(g + 1) * PAIR_W]
        ka_ref[0, :, g * QK_W + PAIR_W:(g + 1) * QK_W] = aug_k[:, g * LANES:(g + 1) * LANES].astype(BF16)


def _qk_aug(lf3, qb, kb, tm):
    b, t, _ = lf3.shape
    pq, pk, one_q, one_k = _bias_tables()
    blk = lambda c: pl.BlockSpec((1, tm, c), lambda i, j: (i, j, 0))
    cst = lambda a: pl.BlockSpec(a.shape, lambda i, j: (0,) * a.ndim)
    return pl.pallas_call(
        _qk_aug_kernel,
        grid=(b, t // tm),
        in_specs=[blk(LANES), blk(W_MIX), blk(W_MIX), cst(pq), cst(pk), cst(one_q), cst(one_k)],
        out_specs=[blk(N_PAIRS * QK_W), blk(N_PAIRS * QK_W)],
        out_shape=[jax.ShapeDtypeStruct((b, t, N_PAIRS * QK_W), BF16)] * 2,
        scratch_shapes=[pltpu.VMEM((1, LANES), F32)],
        compiler_params=_params("arbitrary", "arbitrary"),
        name="logf_cumsum_qk_aug",
    )(lf3, qb, kb, pq, pk, one_q, one_k)


def _fox_prompt_kernel(qi_ref, kj_ref, q_ref, k_ref, vt_ref, o_ref, qm, m_s, l_s, acc, s_scr, p_scr,
                       *, tq, tk):
    step = pl.program_id(2)
    i = qi_ref[step]
    j = kj_ref[step]

    @pl.when(j == 0)
    def _():
        lane = lax.broadcasted_iota(jnp.int32, (tq, QK_W), 1)
        off = lane - PAIR_W
        for hh in range(HEADS_PER_STEP):
            pr, h = divmod(hh, 2)
            q = q_ref[0, :, pr * QK_W:(pr + 1) * QK_W].astype(F32)
            keep = ((lane >= h * D_HEAD) & (lane < (h + 1) * D_HEAD)) \
                | ((off >= 3 * h) & (off < 3 * h + 3)) | ((off >= 6 + 3 * h) & (off < 9 + 3 * h))
            qm[hh] = jnp.where(keep, q, 0.0).astype(BF16)
        m_s[...] = jnp.full_like(m_s, NEG)
        l_s[...] = jnp.zeros_like(l_s)
        acc[...] = jnp.zeros_like(acc)

    def block(diag):
        key = lax.broadcasted_iota(jnp.int32, (KEY_BLK, tq), 0)
        qry = lax.broadcasted_iota(jnp.int32, (KEY_BLK, tq), 1)
        n_kb = tk // KEY_BLK
        groups = lambda t: [t[8 * r:8 * (r + 1), :] for r in range(KEY_BLK // 8)]
        ones = jnp.ones((16, tk), BF16)

        def scores(h, kb):
            s = s_scr[h, kb * KEY_BLK:(kb + 1) * KEY_BLK, :]
            if diag:
                s = jnp.where(key + kb * KEY_BLK <= qry, s, NEG)
            return s

        for h in range(HEADS_PER_STEP):
            pr = h // 2
            s_scr[h] = _bdot_nt(k_ref[0, :, pr * QK_W:(pr + 1) * QK_W], qm[h])
        for h in range(HEADS_PER_STEP):
            m8 = jnp.full((8, tq), NEG, F32)
            for kb in range(n_kb):
                for g in groups(scores(h, kb)):
                    m8 = jnp.maximum(m8, g)
            m_prev = m_s[h]
            m_new = jnp.maximum(m_prev, jnp.max(m8, axis=0, keepdims=True))
            alpha = jnp.exp2(m_prev - m_new)
            for kb in range(n_kb):
                p_scr[h, kb * KEY_BLK:(kb + 1) * KEY_BLK, :] = jnp.exp2(scores(h, kb) - m_new).astype(BF16)
            m_s[h] = m_new
            vt = jnp.concatenate([vt_ref[0, h * D_HEAD:(h + 1) * D_HEAD, :], ones], axis=0)
            pv = _bdot(vt, p_scr[h])
            acc[h] = alpha * acc[h] + pv[:D_HEAD]
            l_s[h] = alpha * l_s[h] + pv[D_HEAD:D_HEAD + 1]

    @pl.when(j < i)
    def _():
        block(False)

    @pl.when(j == i)
    def _():
        block(True)
        o_t = jnp.concatenate([acc[h] / l_s[h] for h in range(HEADS_PER_STEP)], axis=0)
        o_ref[0] = o_t.T.astype(o_ref.dtype)


def _fox_prompt(qa, ka, vt, tq):
    b, _, t = vt.shape
    tk = tq
    nq = t // tq
    qi = np.array([i for i in range(nq) for _ in range(i + 1)], np.int32)
    kj = np.array([j for i in range(nq) for j in range(i + 1)], np.int32)
    kern = functools.partial(_fox_prompt_kernel, tq=tq, tk=tk)
    grid_spec = pltpu.PrefetchScalarGridSpec(
        num_scalar_prefetch=2,
        grid=(b, N_PAIRS // PAIRS_PER_STEP, len(qi)),
        in_specs=[
            pl.BlockSpec((1, tq, PAIRS_PER_STEP * QK_W), lambda bi, g, s, qi_, kj_: (bi, qi_[s], g)),
            pl.BlockSpec((1, tk, PAIRS_PER_STEP * QK_W), lambda bi, g, s, qi_, kj_: (bi, kj_[s], g)),
            pl.BlockSpec((1, PAIRS_PER_STEP * PAIR_W, tk), lambda bi, g, s, qi_, kj_: (bi, g, kj_[s])),
        ],
        out_specs=pl.BlockSpec((1, tq, PAIRS_PER_STEP * PAIR_W), lambda bi, g, s, qi_, kj_: (bi, qi_[s], g)),
        scratch_shapes=[pltpu.VMEM((HEADS_PER_STEP, tq, QK_W), BF16), pltpu.VMEM((HEADS_PER_STEP, 1, tq), F32),
                        pltpu.VMEM((HEADS_PER_STEP, 1, tq), F32), pltpu.VMEM((HEADS_PER_STEP, D_HEAD, tq), F32),
                        pltpu.VMEM((HEADS_PER_STEP, tk, tq), F32), pltpu.VMEM((HEADS_PER_STEP, tk, tq), BF16)],
    )
    return pl.pallas_call(
        kern,
        grid_spec=grid_spec,
        out_shape=jax.ShapeDtypeStruct((b, t, W_MIX), BF16),
        compiler_params=_params("arbitrary", "arbitrary", "arbitrary"),
        name="fox_prompt",
    )(jnp.asarray(qi), jnp.asarray(kj), qa, ka, vt)


def _fox_sample_kernel(pt_ref, q_ref, kn_ref, vn_ref, lfn_ref, *rest, n_new, n_steps, n_grp):
    del pt_ref
    kp_refs, vp_refs, lfp_refs = rest[:n_grp], rest[n_grp:2 * n_grp], rest[2 * n_grp:3 * n_grp]
    o_ref, qm, m_s, l_s, acc, carry, cq_s = rest[3 * n_grp:]
    j = pl.program_id(1)
    rows = n_new * N_HEADS
    lane8 = lax.broadcasted_iota(jnp.int32, (N_HEADS, W_MIX), 1)
    head8 = lax.broadcasted_iota(jnp.int32, (N_HEADS, W_MIX), 0)
    head_sel = (lane8 >> 6) == head8
    kr = lax.broadcasted_iota(jnp.int32, (PAGE, PAGE), 0)
    kc = lax.broadcasted_iota(jnp.int32, (PAGE, PAGE), 1)
    tile = lambda x: jnp.concatenate([x] * n_new, axis=0)

    def update(s, pv):
        m_prev = m_s[...]
        m_new = jnp.maximum(m_prev, jnp.max(s, axis=1, keepdims=True))
        alpha = jnp.exp(m_prev - m_new)
        p = jnp.exp(s - m_new)
        l_s[...] = alpha * l_s[...] + jnp.sum(p, axis=1, keepdims=True)
        acc[...] = alpha * acc[...] + pv(p.astype(BF16))
        m_s[...] = m_new

    @pl.when(j == 0)
    def _():
        q = q_ref[0].astype(F32)
        qrows = [jnp.where(head_sel, jnp.broadcast_to(q[t:t + 1, :], (N_HEADS, W_MIX)), 0.0)
                 for t in range(n_new)]
        qm[...] = jnp.concatenate(qrows, axis=0).astype(BF16)
        m_s[...] = jnp.full_like(m_s, NEG)
        l_s[...] = jnp.zeros_like(l_s)
        acc[...] = jnp.zeros_like(acc)
        carry[...] = jnp.zeros_like(carry)
        cn = _dot_sel_rhs(lfn_ref[0], (kr <= kc).astype(BF16))
        for t in range(n_new):
            cq_s[t * N_HEADS:(t + 1) * N_HEADS, :] = cn[:, t:t + 1]
        s = lax.dot_general(qm[...], kn_ref[0], (((1,), (1,)), ((), ())), preferred_element_type=F32)
        s = s + (cq_s[...] - tile(cn))
        tok = lax.broadcasted_iota(jnp.int32, (rows, PAGE), 0) >> 3
        key = lax.broadcasted_iota(jnp.int32, (rows, PAGE), 1)
        s = jnp.where(key <= tok, s, NEG)
        update(s, lambda p: _bdot(p, vn_ref[0]))

    @pl.when(j > 0)
    def _():
        lfs = [ref[0] for ref in lfp_refs]
        suffix = _dot_sel_rhs(jnp.concatenate(lfs, axis=0), (kr > kc).astype(BF16))
        later = carry[...]
        cq = cq_s[...]
        parts = []
        for g in range(n_grp):
            bias8 = suffix[g * N_HEADS:(g + 1) * N_HEADS, :] + later
            s_g = _bdot(qm[...], kp_refs[g][0].astype(BF16))
            parts.append(s_g + (cq + tile(bias8)))
            later = later + jnp.sum(lfs[g], axis=1, keepdims=True)
        carry[...] = later
        s = jnp.concatenate(parts, axis=1)

        def pv(p):
            out = _bdot_nt(p[:, 0:PAGE], vp_refs[0][0].astype(BF16))
            for g in range(1, n_grp):
                out = out + _bdot_nt(p[:, g * PAGE:(g + 1) * PAGE], vp_refs[g][0].astype(BF16))
            return out

        update(s, pv)

    @pl.when(j == n_steps - 1)
    def _():
        o = acc[...] / l_s[...]
        for t in range(n_new):
            ot = jnp.where(head_sel, o[t * N_HEADS:(t + 1) * N_HEADS, :], 0.0)
            o_ref[0, t:t + 1, :] = jnp.sum(ot, axis=0, keepdims=True).astype(o_ref.dtype)


def _fox_sample(page_table, qb, kn, vn, lfn_t, pool_k, pool_v, pool_lf_t, n_new):
    r, n_pages = page_table.shape
    rows = n_new * N_HEADS
    n_grp = max(g for g in range(1, MAX_PAGES_PER_STEP + 1) if n_pages % g == 0)
    n_steps = n_pages // n_grp + 1
    kern = functools.partial(_fox_sample_kernel, n_new=n_new, n_steps=n_steps, n_grp=n_grp)

    def page(g):
        return lambda ri, j, pt: (pt[ri, n_pages - 1 - ((jnp.maximum(j, 1) - 1) * n_grp + g)], 0, 0)

    req = lambda ri, j, pt: (ri, 0, 0)
    grid_spec = pltpu.PrefetchScalarGridSpec(
        num_scalar_prefetch=1,
        grid=(r, n_steps),
        in_specs=[
            pl.BlockSpec((1, n_new, W_MIX), req),
            pl.BlockSpec((1, PAGE, W_MIX), req),
            pl.BlockSpec((1, PAGE, W_MIX), req),
            pl.BlockSpec((1, N_HEADS, PAGE), req),
        ] + [pl.BlockSpec((1, W_MIX, PAGE), page(g)) for g in range(n_grp)]
          + [pl.BlockSpec((1, W_MIX, PAGE), page(g)) for g in range(n_grp)]
          + [pl.BlockSpec((1, N_HEADS, PAGE), page(g)) for g in range(n_grp)],
        out_specs=pl.BlockSpec((1, n_new, W_MIX), req),
        scratch_shapes=[pltpu.VMEM((rows, W_MIX), BF16), pltpu.VMEM((rows, 1), F32),
                        pltpu.VMEM((rows, 1), F32), pltpu.VMEM((rows, W_MIX), F32),
                        pltpu.VMEM((N_HEADS, 1), F32), pltpu.VMEM((rows, 1), F32)],
    )
    return pl.pallas_call(
        kern,
        grid_spec=grid_spec,
        out_shape=jax.ShapeDtypeStruct((r, n_new, W_MIX), F32),
        compiler_params=_params("arbitrary", "arbitrary"),
        name="fox_sample",
    )(page_table, qb, kn, vn, lfn_t, *([pool_k] * n_grp), *([pool_v] * n_grp), *([pool_lf_t] * n_grp))


HEADS_PER_ITER = 8


def _rwkv_kernel(u_ref, sh_ref, s0_ref, mu_ref, w0_ref, w2_ref, a0_ref, a2_ref, g2_ref, kk_ref, ka_ref,
                 rk_ref, lw_ref, lb_ref, e_ref, msk_ref, o_ref, sout_ref,
                 carry, st, kt_s, rt_s, kh_s, bh_s, k0_s, r0_s, khp_s, bhp_s, v_s, gt_s, o_s,
                 *, tt, t_real, t_total):
    ti = pl.program_id(1)
    nch = tt // CHUNK

    @pl.when(ti == 0)
    def _():
        carry[...] = sh_ref[0]
        st[...] = s0_ref[0]

    u = u_ref[0]
    row1 = lax.broadcasted_iota(jnp.int32, (tt, 1), 0)
    u_prev = jnp.where(row1 == 0, carry[...], pltpu.roll(u, 1, 0))
    carry[...] = u[tt - 1:tt, :]
    us = u + mu_ref[...] * (u_prev - u)
    r = us[:, 0:W_MIX]
    k = us[:, W_MIX:2 * W_MIX]
    v = us[:, 2 * W_MIX:3 * W_MIX]
    wl = us[:, 3 * W_MIX:3 * W_MIX + LANES]
    al = us[:, 3 * W_MIX + LANES:3 * W_MIX + 2 * LANES]
    gl = us[:, 3 * W_MIX + 2 * LANES:3 * W_MIX + 3 * LANES]
    z = w0_ref[...] + _dot(jnp.tanh(wl), w2_ref[...])
    logd = -EXP_NEG_HALF * _sigmoid(z)
    a = _sigmoid(a0_ref[...] + _dot(al, a2_ref[...]))
    g = _dot(_sigmoid(gl), g2_ref[...])
    e_heads = e_ref[...]
    kk = k * kk_ref[...]
    kk = kk * lax.rsqrt(jnp.maximum(_dot(kk * kk, e_heads), 1e-24))
    k_mod = k * (1.0 + (a - 1.0) * ka_ref[...])
    b = kk * a
    bonus = _dot(r * k_mod * rk_ref[...], e_heads) * v
    if t_real < t_total:
        valid = (ti * tt + row1) < t_real
        zero = lambda t: jnp.where(valid, t, 0.0)
        kk, b, k_mod, v, logd = zero(kk), zero(b), zero(k_mod), zero(v), zero(logd)

    cum = _dot_sel_lhs(msk_ref[1].astype(BF16), logd)
    if nch == 1:
        ref = _dot_sel_lhs(msk_ref[7].astype(BF16), logd)
        tot = _dot_sel_lhs(msk_ref[8].astype(BF16), logd)
    else:
        on_rows = lambda r: jnp.concatenate(
            [jnp.broadcast_to(cum[c * CHUNK + r:c * CHUNK + r + 1, :], (CHUNK, W_MIX)) for c in range(nch)], axis=0)
        ref = on_rows(CHUNK // 2 - 1)
        tot = on_rows(CHUNK - 1)
    cum_prev = cum - logd
    e_inv = jnp.exp(ref - cum)
    e_tot = jnp.exp(tot - cum)
    wide = dict(kt=kk * jnp.exp(cum_prev - ref), rt=r * jnp.exp(cum - ref), kh=k_mod * e_inv, bh=b * e_inv,
                k0=kk * jnp.exp(cum_prev), r0=r * jnp.exp(cum), khp=k_mod * e_tot, bhp=b * e_tot, v=v,
                gt=jnp.exp(tot))
    dst = dict(kt=kt_s, rt=rt_s, kh=kh_s, bh=bh_s, k0=k0_s, r0=r0_s, khp=khp_s, bhp=bhp_s, v=v_s, gt=gt_s)
    for name, val in wide.items():
        val = val.astype(dst[name].dtype)
        for h in range(N_HEADS):
            dst[name][h] = val[:, h * D_HEAD:(h + 1) * D_HEAD]

    def head_body(p, _):
        hs = [p * HEADS_PER_ITER + i for i in range(HEADS_PER_ITER)]
        ld = lambda s: [s[h] for h in hs]
        ldb = lambda s: [s[h].astype(BF16) for h in hs]
        kt, rt, kh, bh = ldb(kt_s), ldb(rt_s), ldb(kh_s), ldb(bh_s)
        strict = msk_ref[0] != 0.0
        incl = msk_ref[1] != 0.0
        a_kk = _each(lambda x, y: jnp.where(strict, _bdot_nt(x, y), 0.0).astype(BF16), kt, kh)
        a_kb = _each(lambda x, y: jnp.where(strict, _bdot_nt(x, y), 0.0), kt, bh)
        a_rk = _each(lambda x, y: jnp.where(incl, _bdot_nt(x, y), 0.0).astype(BF16), rt, kh)
        a_rb = _each(lambda x, y: jnp.where(incl, _bdot_nt(x, y), 0.0).astype(BF16), rt, bh)
        x = _each(lambda t: (-(t * msk_ref[2])).astype(BF16), a_kb)
        x2 = _each(lambda t: _bdot(t, t).astype(BF16), x)
        pm = _each(lambda t: msk_ref[6] + t.astype(F32), x)
        pm = _each(lambda q, t: q + _bdot(q.astype(BF16), t), pm, x2)
        x4 = _each(lambda t: _bdot(t, t).astype(BF16), x2)
        pm = _each(lambda q, t: q + _bdot(q.astype(BF16), t), pm, x4)
        for lvl in (3, 4, 5):
            pb = _each(lambda q: q.astype(BF16), pm)
            a_off = _each(lambda t: (t * msk_ref[lvl]).astype(BF16), a_kb)
            pa = _each(lambda q, t: _bdot(q, t).astype(BF16), pb, a_off)
            pm = _each(lambda q, s, qb: q - _bdot(s, qb), pm, pa, pb)
        t_inv = _each(lambda q: q.astype(BF16), pm)
        vb = ldb(v_s)
        w1 = _each(lambda t, y: _bdot(t, y), t_inv, ldb(k0_s))
        av = _each(lambda t, y: _bdot(t, y).astype(BF16), a_kk, vb)
        u1 = _each(lambda t, y: _bdot(t, y), t_inv, av)
        w1b = _each(lambda t: t.astype(BF16), w1)
        u1b = _each(lambda t: t.astype(BF16), u1)
        q_t = _each(lambda r0, t, y: (r0 - _bdot(t, y)).astype(BF16), ld(r0_s), a_rb, w1b)
        o1 = _each(lambda t, y, t2, y2: _bdot(t, y) - _bdot(t2, y2), a_rk, vb, a_rb, u1b)
        khp, bhp, gt = ld(khp_s), ld(bhp_s), ld(gt_s)
        for c in range(nch):
            rows = slice(c * CHUNK, (c + 1) * CHUNK)
            bt = _each(lambda t: t[rows].T.astype(BF16), bhp)
            kt_ = _each(lambda t: t[rows].T.astype(BF16), khp)
            gcol = _each(lambda t: t[rows].T[:, 0:1], gt)
            m_low = _each(lambda t, y: _bdot(t, y[rows]).astype(BF16), bt, w1b)
            n1 = _each(lambda t, y, t2, y2: _bdot(t, y[rows]) - _bdot(t2, y2[rows]), kt_, vb, bt, u1b)
            for i, h in enumerate(hs):
                s_t = st[h]
                s_hi = s_t.astype(BF16)
                s_lo = (s_t - s_hi.astype(F32)).astype(BF16)
                qc = q_t[i][rows]
                o_s[h, rows, :] = _bdot(qc, s_hi) + _bdot(qc, s_lo) + o1[i][rows]
                st[h] = gcol[i] * s_t - (_bdot(m_low[i], s_hi) + _bdot(m_low[i], s_lo)) + n1[i]
        return 0

    lax.fori_loop(0, N_HEADS // HEADS_PER_ITER, head_body, 0)

    o = jnp.concatenate([o_s[h] for h in range(N_HEADS)], axis=1)
    mean = _dot(o, e_heads) * (1.0 / D_HEAD)
    xc = o - mean
    var = _dot(xc * xc, e_heads) * (1.0 / D_HEAD)
    on = xc * lax.rsqrt(var + GN_EPS) * lw_ref[...] + lb_ref[...]
    o_ref[0] = ((on + bonus) * g).astype(o_ref.dtype)
    sout_ref[0] = st[...]


def _chunk_masks(tt):
    i = jnp.arange(tt)[:, None]
    j = jnp.arange(tt)[None, :]
    blk = lambda s: (i >> s) == (j >> s)
    same = blk(6)
    ms = [same & (j < i), same & (j <= i), blk(3), blk(4) & ~blk(3), blk(5) & ~blk(4), same & ~blk(5),
          i == j, same & ((j & (CHUNK - 1)) < CHUNK // 2), same]
    return jnp.stack(ms).astype(F32)


def _rwkv(u3, shift0, s0_t, vecs, w2p, a2p, g2, e_heads, tt, t_real):
    b, t, _ = u3.shape
    mu, w0, a0, k_k, k_a, r_k, lnw, lnb = vecs
    masks = _chunk_masks(tt)
    kern = functools.partial(_rwkv_kernel, tt=tt, t_real=t_real, t_total=t)
    vec = lambda c: pl.BlockSpec((1, c), lambda bi, ti: (0, 0))
    mat = lambda r_, c: pl.BlockSpec((r_, c), lambda bi, ti: (0, 0))
    wide = lambda dt: pltpu.VMEM((N_HEADS, tt, D_HEAD), dt)
    return pl.pallas_call(
        kern,
        grid=(b, t // tt),
        in_specs=[
            pl.BlockSpec((1, tt, RW_PAD), lambda bi, ti: (bi, ti, 0)),
            pl.BlockSpec((1, 1, RW_PAD), lambda bi, ti: (bi, 0, 0)),
            pl.BlockSpec((1, N_HEADS, D_HEAD, D_HEAD), lambda bi, ti: (bi, 0, 0, 0)),
            vec(RW_PAD), vec(W_MIX), mat(LANES, W_MIX), vec(W_MIX), mat(LANES, W_MIX), mat(LANES, W_MIX),
            vec(W_MIX), vec(W_MIX), vec(W_MIX), vec(W_MIX), vec(W_MIX), mat(W_MIX, W_MIX),
            pl.BlockSpec(masks.shape, lambda bi, ti: (0, 0, 0)),
        ],
        out_specs=[pl.BlockSpec((1, tt, W_MIX), lambda bi, ti: (bi, ti, 0)),
                   pl.BlockSpec((1, N_HEADS, D_HEAD, D_HEAD), lambda bi, ti: (bi, 0, 0, 0))],
        out_shape=[jax.ShapeDtypeStruct((b, t, W_MIX), BF16),
                   jax.ShapeDtypeStruct((b, N_HEADS, D_HEAD, D_HEAD), F32)],
        scratch_shapes=[pltpu.VMEM((1, RW_PAD), F32), pltpu.VMEM((N_HEADS, D_HEAD, D_HEAD), F32)]
                       + [wide(BF16 if name in ("kt", "rt", "kh", "bh", "k0", "v") else F32)
                          for name in ("kt", "rt", "kh", "bh", "k0", "r0", "khp", "bhp", "v", "gt", "o")],
        compiler_params=_params("arbitrary", "arbitrary"),
        name="rwkv7_chunked",
    )(u3, shift0, s0_t, mu, w0, w2p, a0, a2p, g2, k_k, k_a, r_k, lnw, lnb, e_heads, masks)


def _mix_route_kernel(x_ref, oa_ref, ob_ref, gm_ref, wg_ref, pa_ref, pb_ref, wo_ref, gf_ref, wr_ref,
                      x1_ref, hn_ref, rw_ref):
    x = x_ref[...]
    xb = _rms(x, gm_ref[...]).astype(BF16)
    gates = _sigmoid(jnp.dot(xb, wg_ref[...], preferred_element_type=F32))
    mixed = (gates[:, :D_MODEL] * jnp.dot(oa_ref[...], pa_ref[...], preferred_element_type=F32)
             + gates[:, D_MODEL:] * jnp.dot(ob_ref[...], pb_ref[...], preferred_element_type=F32))
    x1 = x + _dot(mixed, wo_ref[...])
    x1_ref[...] = x1
    hb = _rms(x1, gf_ref[...]).astype(BF16)
    hn_ref[...] = hb
    logits = jnp.dot(hb, wr_ref[...], preferred_element_type=F32)
    tm = logits.shape[0]
    lane = lax.broadcasted_iota(jnp.int32, (tm, LANES), 1)
    lane_f = lane.astype(F32)
    first = lambda hit: jnp.min(jnp.where(hit, lane_f, 1e9), axis=1, keepdims=True)
    is_grp = (lane >= N_EXPERTS) & (lane < N_EXPERTS + N_GROUPS)
    gl = jnp.where(is_grp, logits, NEG)
    gmax = jnp.max(gl, axis=1, keepdims=True)
    g_idx = first(gl == gmax) - float(N_EXPERTS)
    p_grp = 1.0 / jnp.sum(jnp.where(is_grp, jnp.exp(gl - gmax), 0.0), axis=1, keepdims=True)
    in_grp = (lane < N_EXPERTS) & ((lane >> 3).astype(F32) == g_idx)
    el = jnp.where(in_grp, logits, NEG)
    v1 = jnp.max(el, axis=1, keepdims=True)
    i1 = first(el == v1)
    el2 = jnp.where(lane_f == i1, NEG, el)
    v2 = jnp.max(el2, axis=1, keepdims=True)
    i2 = first(el2 == v2)
    e2 = jnp.exp(v2 - v1)
    w1 = p_grp / (1.0 + e2)
    rw_ref[...] = jnp.where(lane_f == i1, w1, 0.0) + jnp.where(lane_f == i2, w1 * e2, 0.0)


def _mix_route(x2, oa, ob, gm, wg, pa, pb, wo, gf, wr, tm):
    n = x2.shape[0]
    row = lambda c: pl.BlockSpec((tm, c), lambda i: (i, 0))
    full = lambda r, c: pl.BlockSpec((r, c), lambda i: (0, 0))
    return pl.pallas_call(
        _mix_route_kernel,
        grid=(n // tm,),
        in_specs=[row(D_MODEL), row(W_MIX), row(W_MIX), full(1, D_MODEL), full(D_MODEL, 2 * D_MODEL),
                  full(W_MIX, D_MODEL), full(W_MIX, D_MODEL), full(D_MODEL, D_MODEL), full(1, D_MODEL),
                  full(D_MODEL, LANES)],
        out_specs=[row(D_MODEL), row(D_MODEL), row(LANES)],
        out_shape=[jax.ShapeDtypeStruct((n, D_MODEL), F32), jax.ShapeDtypeStruct((n, D_MODEL), BF16),
                   jax.ShapeDtypeStruct((n, LANES), F32)],
        compiler_params=_params("arbitrary"),
        name="mix_route",
    )(x2, oa, ob, gm, wg, pa, pb, wo, gf, wr)


MOE_CHUNK = 256
MOE_TAIL_CHUNK = 128


def _moe_kernel(cnt_ref, hn_ref, rw_ref, wgu_ref, wd_ref, tri_ref, out_ref, acc, *, tile, chunks):
    i = pl.program_id(0)
    g = pl.program_id(1)

    @pl.when(g == 0)
    def _():
        acc[...] = jnp.zeros_like(acc)

    rw = rw_ref[...]
    lane = lax.broadcasted_iota(jnp.int32, (tile, LANES), 1)
    in_g = (lane >> 3) == g
    member = jnp.sum(jnp.where(in_g, rw, 0.0), axis=1, keepdims=True) > 0.0
    ind = jnp.broadcast_to(jnp.where(member, 1.0, 0.0), (tile, LANES))
    rank = _bdot(tri_ref[...], ind.astype(BF16))
    ind_row = ind.T[0:1, :]
    rank_row = rank.T[0:1, :]
    count = cnt_ref[i * N_GROUPS + g]
    pieces = _split3(rw)

    for start, chunk in chunks:
        @pl.when(start < count)
        def _():
            lane_c = lax.broadcasted_iota(jnp.int32, (chunk, LANES), 1)
            slot_r = (lax.broadcasted_iota(jnp.int32, (chunk, tile), 0) + start).astype(F32)
            slot_c = (lax.broadcasted_iota(jnp.int32, (tile, chunk), 1) + start).astype(F32)
            gather = jnp.where((rank_row == slot_r) & (ind_row > 0.0), 1.0, 0.0).astype(BF16)
            scatter = jnp.where((rank[:, 0:1] == slot_c) & member, 1.0, 0.0).astype(BF16)
            xg = _bdot(gather, hn_ref[...]).astype(BF16)
            wg = sum(_bdot(gather, pc) for pc in pieces)
            yg = jnp.zeros((chunk, D_MODEL), F32)
            for e in range(EXPERTS_PER_GROUP):
                h = _bdot(xg, wgu_ref[0, e])
                gate = h[:, :D_EXPERT]
                act = (gate * _sigmoid(gate) * h[:, D_EXPERT:]).astype(BF16)
                w_e = jnp.sum(jnp.where(lane_c == g * EXPERTS_PER_GROUP + e, wg, 0.0), axis=1, keepdims=True)
                yg = yg + _bdot(act, wd_ref[0, e]) * w_e
            acc[...] += _bdot(scatter, yg.astype(BF16))

    @pl.when(g == N_GROUPS - 1)
    def _():
        out_ref[...] = acc[...]


def _moe(hn, rw, wgu, wd, tile):
    n = hn.shape[0]
    chunks = [(0, min(MOE_CHUNK, tile))]
    while sum(chunks[-1]) < tile:
        chunks.append((sum(chunks[-1]), MOE_TAIL_CHUNK))
    n_tiles = n // tile
    grp_w = rw[:, :N_EXPERTS].reshape(n_tiles, tile, N_GROUPS, EXPERTS_PER_GROUP).sum(-1)
    counts = (grp_w > 0.0).sum(1).astype(jnp.int32).reshape(-1)
    t = jnp.arange(tile)
    tri = (t[None, :] < t[:, None]).astype(BF16)
    kern = functools.partial(_moe_kernel, tile=tile, chunks=tuple(chunks))
    row = lambda c: pl.BlockSpec((tile, c), lambda i, g, cnt: (i, 0))
    grid_spec = pltpu.PrefetchScalarGridSpec(
        num_scalar_prefetch=1,
        grid=(n_tiles, N_GROUPS),
        in_specs=[row(D_MODEL), row(LANES),
                  pl.BlockSpec((1, EXPERTS_PER_GROUP, D_MODEL, 2 * D_EXPERT), lambda i, g, cnt: (g, 0, 0, 0)),
                  pl.BlockSpec((1, EXPERTS_PER_GROUP, D_EXPERT, D_MODEL), lambda i, g, cnt: (g, 0, 0, 0)),
                  pl.BlockSpec((tile, tile), lambda i, g, cnt: (0, 0))],
        out_specs=row(D_MODEL),
        scratch_shapes=[pltpu.VMEM((tile, D_MODEL), F32)],
    )
    return pl.pallas_call(
        kern,
        grid_spec=grid_spec,
        out_shape=jax.ShapeDtypeStruct((n, D_MODEL), F32),
        compiler_params=_params("arbitrary", "arbitrary"),
        name="moe_grouped",
    )(counts, hn, rw, wgu, wd, tri)


def _final_norm_kernel(x_ref, m_ref, g_ref, y_ref):
    y_ref[...] = _rms(x_ref[...] + m_ref[...], g_ref[...])


def _final_norm(x1, moe_out, gn, tm):
    n = x1.shape[0]
    row = pl.BlockSpec((tm, D_MODEL), lambda i: (i, 0))
    return pl.pallas_call(
        _final_norm_kernel,
        grid=(n // tm,),
        in_specs=[row, row, pl.BlockSpec((1, D_MODEL), lambda i: (0, 0))],
        out_specs=row,
        out_shape=jax.ShapeDtypeStruct((n, D_MODEL), F32),
        compiler_params=_params("arbitrary"),
        name="final_norm",
    )(x1, moe_out, gn)


def _pad_rw_cols(m):
    z = jnp.zeros(m.shape[:-1] + (LANES - LORA_W,), m.dtype)
    c = 3 * W_MIX
    return jnp.concatenate([m[..., :c], m[..., c:c + LORA_W], z, m[..., c + LORA_W:c + 2 * LORA_W], z,
                            m[..., c + 2 * LORA_W:]], axis=-1)


def _unpad_rw_cols(m):
    c = 3 * W_MIX
    return jnp.concatenate([m[..., :c], m[..., c:c + LORA_W], m[..., c + LANES:c + LANES + LORA_W],
                            m[..., c + 2 * LANES:]], axis=-1)


def _pad_rows(m, rows):
    return jnp.concatenate([m, jnp.zeros((rows - m.shape[0],) + m.shape[1:], m.dtype)], axis=0)


def _prep_weights(l, norm_mix, w_in, b_f, mu_rw, w0, w2, a0, a2, g2, k_k, k_a, r_k, lnx_w, lnx_b,
                  p_a, p_b, w_o, norm_ffn, w_grp, w_exp, we_gate, we_up, we_down):
    w = w_in[l]
    w_fl = jnp.concatenate([w[:, 3 * W_MIX:FOX_COLS], jnp.zeros((D_MODEL, LANES - N_HEADS), F32)], axis=1)
    w_rw = _pad_rw_cols(w[:, FOX_COLS:FOX_COLS + RW_COLS])
    row = lambda vct: vct.reshape(1, -1)
    head_id = jnp.arange(W_MIX) // D_HEAD
    return dict(
        gm=row(norm_mix[l]),
        w_a=jnp.concatenate([w[:, :3 * W_MIX], w_fl, w_rw], axis=1).astype(BF16),
        w_kvt=w[:, W_MIX:3 * W_MIX].T.astype(BF16),
        bf=jnp.concatenate([b_f[l], jnp.zeros((LANES - N_HEADS,), F32)]).reshape(1, LANES),
        wg=w[:, FOX_COLS + RW_COLS:].astype(BF16),
        vecs=(row(_pad_rw_cols(mu_rw[l])), row(w0[l]), row(a0[l]), row(k_k[l]), row(k_a[l]),
              row(r_k[l].reshape(-1)), row(lnx_w[l]), row(lnx_b[l])),
        w2p=_pad_rows(w2[l], LANES).astype(BF16),
        a2p=_pad_rows(a2[l], LANES).astype(BF16),
        g2=g2[l].astype(BF16),
        e_heads=(head_id[:, None] == head_id[None, :]).astype(BF16),
        pa=p_a[l].astype(BF16), pb=p_b[l].astype(BF16), wo=w_o[l].astype(BF16),
        gf=row(norm_ffn[l]),
        wr=jnp.concatenate([w_exp[l], w_grp[l], jnp.zeros((D_MODEL, LANES - N_EXPERTS - N_GROUPS), F32)],
                           axis=1).astype(BF16),
        wgu=jnp.concatenate([we_gate[l], we_up[l]], axis=-1).astype(BF16).reshape(
            N_GROUPS, EXPERTS_PER_GROUP, D_MODEL, 2 * D_EXPERT),
        wd=we_down[l].astype(BF16).reshape(N_GROUPS, EXPERTS_PER_GROUP, D_EXPERT, D_MODEL),
    )


def _layer(x, past, shift0, wkv0, page_table, p, gn, tiles):
    b, t, _ = x.shape
    n = b * t
    tm, tq, tt, tm_moe = tiles
    x2 = x.reshape(n, D_MODEL)
    to3 = lambda m: m.reshape(b, t, m.shape[-1])
    if past is None:
        qb, kb, lf, rw, k_t, v_t, vt = _inproj(x2, p["gm"], p["w_a"], p["w_kvt"], p["bf"], tm,
                                               QK_SCALE * LOG2E, seq=t)
        qa, ka = _qk_aug(to3(lf), to3(qb), to3(kb), tm)
        o_a = _fox_prompt(qa, ka, vt, tq)
        t_pad = t
        u3 = to3(rw)
        k, v = (m.reshape(b, N_HEADS, D_HEAD, t).transpose(0, 3, 1, 2) for m in (k_t, v_t))
    else:
        qb, kb, lf, rw, k, v, vb = _inproj(x2, p["gm"], p["w_a"], p["w_kvt"], p["bf"], tm, QK_SCALE)
        pool_k, pool_v, pool_lf_t = past
        pad_keys = lambda m: jnp.concatenate([to3(m), jnp.zeros((b, PAGE - t, W_MIX), BF16)], axis=1)
        lfn_t = jnp.concatenate([to3(lf)[:, :, :N_HEADS].transpose(0, 2, 1),
                                 jnp.zeros((b, N_HEADS, PAGE - t), F32)], axis=2)
        o_a = _fox_sample(page_table, to3(qb).astype(F32), pad_keys(kb), pad_keys(vb), lfn_t,
                          pool_k, pool_v, pool_lf_t, t).astype(BF16)
        t_pad = tt
        u3 = jnp.concatenate([to3(rw), jnp.zeros((b, t_pad - t, RW_PAD), F32)], axis=1)
    o_b, wkv_t = _rwkv(u3, shift0, jnp.swapaxes(wkv0, -1, -2), p["vecs"], p["w2p"], p["a2p"], p["g2"],
                       p["e_heads"], tt, t)
    o_b = o_b[:, :t].reshape(n, W_MIX)
    wkv_new = jnp.swapaxes(wkv_t, -1, -2)
    x1, hn, route = _mix_route(x2, o_a.reshape(n, W_MIX), o_b, p["gm"], p["wg"], p["pa"], p["pb"], p["wo"],
                               p["gf"], p["wr"], tm)
    y = _final_norm(x1, _moe(hn, route, p["wgu"], p["wd"], tm_moe), gn, tm)
    shift_new = _unpad_rw_cols(to3(rw)[:, t - 1, :])
    return (y.reshape(b, t, D_MODEL), k.reshape(b, t, N_HEADS, D_HEAD), v.reshape(b, t, N_HEADS, D_HEAD),
            to3(lf)[:, :, :N_HEADS], wkv_new, shift_new)


def kernel(x_prompt, x_sample, cache_k, cache_v, cache_logf, page_table, state_wkv, state_shift, norm_mix, w_in, b_f, mu_rw, w0, w2, a0, a2, g2, k_k, k_a, r_k, lnx_w, lnx_b, p_a, p_b, w_o, norm_ffn, w_grp, w_exp, we_gate, we_up, we_down, norm_final):
    depth = w_in.shape[0]
    assert depth == 1, "final norm is fused into the layer's last kernel"
    bp, tp, _ = x_prompt.shape
    bs, ts, _ = x_sample.shape
    n_phys = cache_k.shape[1]
    gn = norm_final.reshape(1, D_MODEL)
    l = 0
    p = _prep_weights(l, norm_mix, w_in, b_f, mu_rw, w0, w2, a0, a2, g2, k_k, k_a, r_k, lnx_w, lnx_b,
                      p_a, p_b, w_o, norm_ffn, w_grp, w_exp, we_gate, we_up, we_down)
    tq = min(1024, tp)
    prompt = _layer(x_prompt, None, jnp.zeros((bp, 1, RW_PAD), F32),
                    jnp.zeros((bp, N_HEADS, D_HEAD, D_HEAD), F32), None, p, gn,
                    (min(512, bp * tp), tq, min(256, tp), min(1024, bp * tp)))
    past = (cache_k[l].transpose(0, 2, 3, 1).reshape(n_phys, W_MIX, PAGE),
            cache_v[l].transpose(0, 2, 3, 1).reshape(n_phys, W_MIX, PAGE),
            cache_logf[l].transpose(0, 2, 1))
    sample = _layer(x_sample, past, _pad_rw_cols(state_shift[l])[:, None, :], state_wkv[l], page_table, p, gn,
                    (bs * ts, None, CHUNK, bs * ts))
    outs = []
    for y, k, v, lf, wkv, sh in (prompt, sample):
        outs.append((y, k[None], v[None], lf[None], wkv[None], sh[None]))
    (yp, kp, vp, lp, wp, sp), (ys, ks, vs, ls, ws, ss) = outs
    return (yp, ys, kp, vp, lp, wp, sp, ks, vs, ls, ws, ss)
```

```python
import functools

import numpy as np
import jax
import jax.numpy as jnp
from jax import lax
from jax.experimental import pallas as pl
from jax.experimental.pallas import tpu as pltpu

F32 = jnp.float32
BF16 = jnp.bfloat16

D_MODEL = 1024
N_HEADS = 8
D_HEAD = 64
W_MIX = N_HEADS * D_HEAD
LORA_W = 64
LORA_G = 128
RW_COLS = 3 * W_MIX + 2 * LORA_W + LORA_G
RW_PAD = 3 * W_MIX + 3 * 128
FOX_COLS = 3 * W_MIX + N_HEADS
N_EXPERTS = 32
EXPERTS_PER_GROUP = 8
N_GROUPS = 4
D_EXPERT = 256
PAGE = 128
MAX_PAGES_PER_STEP = 16
NORM_EPS = 1e-6
GN_EPS = 64e-5
QK_SCALE = D_HEAD ** -0.5
LANES = 128
CHUNK = 64
EXP_NEG_HALF = 0.6065306597126334
NEG = -1e30
VMEM_LIMIT = 56 * 1024 * 1024


def _dot(a, b):
    return jnp.dot(a.astype(BF16), b.astype(BF16), preferred_element_type=F32)


def _bdot(a, b):
    return jnp.dot(a, b, preferred_element_type=F32)


def _bdot_nt(a, b):
    return lax.dot_general(a, b, (((1,), (1,)), ((), ())), preferred_element_type=F32)


def _each(f, *lists):
    return [f(*xs) for xs in zip(*lists)]


def _split3(x):
    hi = x.astype(BF16)
    r1 = x - hi.astype(F32)
    mid = r1.astype(BF16)
    lo = (r1 - mid.astype(F32)).astype(BF16)
    return hi, mid, lo


def _dot_sel_lhs(sel, x):
    hi, mid, lo = _split3(x)
    d = lambda p: jnp.dot(sel, p, preferred_element_type=F32)
    return d(hi) + d(mid) + d(lo)


def _dot_sel_rhs(x, sel):
    hi, mid, lo = _split3(x)
    d = lambda p: jnp.dot(p, sel, preferred_element_type=F32)
    return d(hi) + d(mid) + d(lo)


def _sigmoid(x):
    return 0.5 * jnp.tanh(0.5 * x) + 0.5


def _log_sigmoid(x):
    return jnp.minimum(x, 0.0) - jnp.log(1.0 + jnp.exp(-jnp.abs(x)))


def _rms(x, g):
    return (x * lax.rsqrt(jnp.mean(x * x, axis=-1, keepdims=True) + NORM_EPS)) * g


def _params(*sem):
    return pltpu.CompilerParams(dimension_semantics=sem, vmem_limit_bytes=VMEM_LIMIT)


def _inproj_kernel(x_ref, g_ref, w_ref, wkvt_ref, bf_ref, q_ref, kb_ref, lf_ref, rw_ref, *kv_refs,
                   q_scale, seq_major_kv):
    xb = _rms(x_ref[...], g_ref[...]).astype(BF16)
    d = lambda lo, hi: jnp.dot(xb, w_ref[:, lo:hi], preferred_element_type=F32)
    q_ref[...] = (d(0, W_MIX) * q_scale).astype(BF16)
    k = d(W_MIX, 2 * W_MIX)
    kb_ref[...] = k.astype(BF16)
    lf_ref[...] = _log_sigmoid(d(3 * W_MIX, 3 * W_MIX + LANES) + bf_ref[...])
    rw_ref[...] = d(3 * W_MIX + LANES, 3 * W_MIX + LANES + RW_PAD)
    k_ref, v_ref, vb_ref = kv_refs
    if seq_major_kv:
        k_ref[0] = _bdot_nt(wkvt_ref[0:W_MIX, :], xb)
        vt = _bdot_nt(wkvt_ref[W_MIX:2 * W_MIX, :], xb)
        v_ref[0] = vt
        vb_ref[0] = vt.astype(BF16)
    else:
        v = d(2 * W_MIX, 3 * W_MIX)
        k_ref[...] = k
        v_ref[...] = v
        vb_ref[...] = v.astype(BF16)


def _inproj(x2, g, w_a, w_kvt, bf_pad, tm, q_scale, seq=None):
    n = x2.shape[0]
    wcols = w_a.shape[1]
    row = lambda c: pl.BlockSpec((tm, c), lambda i: (i, 0))
    full = lambda r, c: pl.BlockSpec((r, c), lambda i: (0, 0))
    out_specs = [row(W_MIX), row(W_MIX), row(LANES), row(RW_PAD)]
    out_shape = [jax.ShapeDtypeStruct((n, W_MIX), BF16), jax.ShapeDtypeStruct((n, W_MIX), BF16),
                 jax.ShapeDtypeStruct((n, LANES), F32), jax.ShapeDtypeStruct((n, RW_PAD), F32)]
    if seq is not None:
        per_seq = seq // tm
        kv_spec = pl.BlockSpec((1, W_MIX, tm), lambda i: (i // per_seq, 0, i % per_seq))
        kv_shape = lambda dt: jax.ShapeDtypeStruct((n // seq, W_MIX, seq), dt)
    else:
        kv_spec = row(W_MIX)
        kv_shape = lambda dt: jax.ShapeDtypeStruct((n, W_MIX), dt)
    out_specs += [kv_spec] * 3
    out_shape += [kv_shape(F32), kv_shape(F32), kv_shape(BF16)]
    return pl.pallas_call(
        functools.partial(_inproj_kernel, q_scale=q_scale, seq_major_kv=seq is not None),
        grid=(n // tm,),
        in_specs=[row(D_MODEL), full(1, D_MODEL), full(D_MODEL, wcols), full(2 * W_MIX, D_MODEL), full(1, LANES)],
        out_specs=out_specs,
        out_shape=out_shape,
        compiler_params=_params("arbitrary"),
        name="inproj",
    )(x2, g, w_a, w_kvt, bf_pad)


PAIR_W = 2 * D_HEAD
N_PAIRS = N_HEADS // 2
QK_W = PAIR_W + LANES
PAIRS_PER_STEP = 2
HEADS_PER_STEP = 2 * PAIRS_PER_STEP
KEY_BLK = 32
LOG2E = 1.4426950408889634


def _bias_tables():
    pq = np.zeros((3, LANES, N_PAIRS * LANES), np.float32)
    pk = np.zeros((3, LANES, N_PAIRS * LANES), np.float32)
    one_q = np.zeros((1, N_PAIRS * LANES), np.float32)
    one_k = np.zeros((1, N_PAIRS * LANES), np.float32)
    for h in range(N_HEADS):
        base = LANES * (h // 2)
        for piece in range(3):
            pq[piece, h, base + 3 * (h % 2) + piece] = 1.0
            pk[piece, h, base + 6 + 3 * (h % 2) + piece] = -1.0
    for g in range(N_PAIRS):
        one_q[0, LANES * g + 6:LANES * g + 12] = 1.0
        one_k[0, LANES * g + 0:LANES * g + 6] = 1.0
    return (jnp.asarray(pq, BF16), jnp.asarray(pk, BF16), jnp.asarray(one_q), jnp.asarray(one_k))


def _qk_aug_kernel(lf_ref, q_ref, k_ref, pq_ref, pk_ref, oq_ref, ok_ref, qa_ref, ka_ref, carry):
    @pl.when(pl.program_id(1) == 0)
    def _():
        carry[...] = jnp.zeros_like(carry)

    lf = lf_ref[0] * LOG2E
    tm = lf.shape[0]
    row = lax.broadcasted_iota(jnp.int32, (tm, tm), 0)
    col = lax.broadcasted_iota(jnp.int32, (tm, tm), 1)
    c = _dot_sel_lhs((col <= row).astype(BF16), lf) + carry[...]
    carry[...] = c[tm - 1:tm, :]
    pieces = _split3(c)
    aug_q = oq_ref[...] + sum(_bdot(pc, pq_ref[i]) for i, pc in enumerate(pieces))
    aug_k = ok_ref[...] + sum(_bdot(pc, pk_ref[i]) for i, pc in enumerate(pieces))
    q = q_ref[0]
    k = k_ref[0]
    for g in range(N_PAIRS):
        qa_ref[0, :, g * QK_W:g * QK_W + PAIR_W] = q[:, g * PAIR_W:(g + 1) * PAIR_W]
        qa_ref[0, :, g * QK_W + PAIR_W:(g + 1) * QK_W] = aug_q[:, g * LANES:(g + 1) * LANES].astype(BF16)
        ka_ref[0, :, g * QK_W:g * QK_W + PAIR_W] = k[:, g * PAIR_W:(g + 1) * PAIR_W]
        ka_ref[0, :, g * QK_W + PAIR_W:(g + 1) * QK_W] = aug_k[:, g * LANES:(g + 1) * LANES].astype(BF16)


def _qk_aug(lf3, qb, kb, tm):
    b, t, _ = lf3.shape
    pq, pk, one_q, one_k = _bias_tables()
    blk = lambda c: pl.BlockSpec((1, tm, c), lambda i, j: (i, j, 0))
    cst = lambda a: pl.BlockSpec(a.shape, lambda i, j: (0,) * a.ndim)
    return pl.pallas_call(
        _qk_aug_kernel,
        grid=(b, t // tm),
        in_specs=[blk(LANES), blk(W_MIX), blk(W_MIX), cst(pq), cst(pk), cst(one_q), cst(one_k)],
        out_specs=[blk(N_PAIRS * QK_W), blk(N_PAIRS * QK_W)],
        out_shape=[jax.ShapeDtypeStruct((b, t, N_PAIRS * QK_W), BF16)] * 2,
        scratch_shapes=[pltpu.VMEM((1, LANES), F32)],
        compiler_params=_params("arbitrary", "arbitrary"),
        name="logf_cumsum_qk_aug",
    )(lf3, qb, kb, pq, pk, one_q, one_k)


def _fox_prompt_kernel(qi_ref, kj_ref, q_ref, k_ref, vt_ref, o_ref, qm, m_s, l_s, acc, s_scr, p_scr,
                       *, tq, tk):
    step = pl.program_id(2)
    i = qi_ref[step]
    j = kj_ref[step]

    @pl.when(j == 0)
    def _():
        lane = lax.broadcasted_iota(jnp.int32, (tq, QK_W), 1)
        off = lane - PAIR_W
        for hh in range(HEADS_PER_STEP):
            pr, h = divmod(hh, 2)
            q = q_ref[0, :, pr * QK_W:(pr + 1) * QK_W].astype(F32)
            keep = ((lane >= h * D_HEAD) & (lane < (h + 1) * D_HEAD)) \
                | ((off >= 3 * h) & (off < 3 * h + 3)) | ((off >= 6 + 3 * h) & (off < 9 + 3 * h))
            qm[hh] = jnp.where(keep, q, 0.0).astype(BF16)
        m_s[...] = jnp.full_like(m_s, NEG)
        l_s[...] = jnp.zeros_like(l_s)
        acc[...] = jnp.zeros_like(acc)

    def block(diag):
        key = lax.broadcasted_iota(jnp.int32, (KEY_BLK, tq), 0)
        qry = lax.broadcasted_iota(jnp.int32, (KEY_BLK, tq), 1)
        n_kb = tk // KEY_BLK
        groups = lambda t: [t[8 * r:8 * (r + 1), :] for r in range(KEY_BLK // 8)]
        ones = jnp.ones((16, tk), BF16)

        def scores(h, kb):
            s = s_scr[h, kb * KEY_BLK:(kb + 1) * KEY_BLK, :]
            if diag:
                s = jnp.where(key + kb * KEY_BLK <= qry, s, NEG)
            return s

        for h in range(HEADS_PER_STEP):
            pr = h // 2
            s_scr[h] = _bdot_nt(k_ref[0, :, pr * QK_W:(pr + 1) * QK_W], qm[h])
        for h in range(HEADS_PER_STEP):
            m8 = jnp.full((8, tq), NEG, F32)
            for kb in range(n_kb):
                for g in groups(scores(h, kb)):
                    m8 = jnp.maximum(m8, g)
            m_prev = m_s[h]
            m_new = jnp.maximum(m_prev, jnp.max(m8, axis=0, keepdims=True))
            alpha = jnp.exp2(m_prev - m_new)
            for kb in range(n_kb):
                p_scr[h, kb * KEY_BLK:(kb + 1) * KEY_BLK, :] = jnp.exp2(scores(h, kb) - m_new).astype(BF16)
            m_s[h] = m_new
            vt = jnp.concatenate([vt_ref[0, h * D_HEAD:(h + 1) * D_HEAD, :], ones], axis=0)
            pv = _bdot(vt, p_scr[h])
            acc[h] = alpha * acc[h] + pv[:D_HEAD]
            l_s[h] = alpha * l_s[h] + pv[D_HEAD:D_HEAD + 1]

    @pl.when(j < i)
    def _():
        block(False)

    @pl.when(j == i)
    def _():
        block(True)
        o_t = jnp.concatenate([acc[h] / l_s[h] for h in range(HEADS_PER_STEP)], axis=0)
        o_ref[0] = o_t.T.astype(o_ref.dtype)


def _fox_prompt(qa, ka, vt, tq):
    b, _, t = vt.shape
    tk = tq
    nq = t // tq
    qi = np.array([i for i in range(nq) for _ in range(i + 1)], np.int32)
    kj = np.array([j for i in range(nq) for j in range(i + 1)], np.int32)
    kern = functools.partial(_fox_prompt_kernel, tq=tq, tk=tk)
    grid_spec = pltpu.PrefetchScalarGridSpec(
        num_scalar_prefetch=2,
        grid=(b, N_PAIRS // PAIRS_PER_STEP, len(qi)),
        in_specs=[
            pl.BlockSpec((1, tq, PAIRS_PER_STEP * QK_W), lambda bi, g, s, qi_, kj_: (bi, qi_[s], g)),
            pl.BlockSpec((1, tk, PAIRS_PER_STEP * QK_W), lambda bi, g, s, qi_, kj_: (bi, kj_[s], g)),
            pl.BlockSpec((1, PAIRS_PER_STEP * PAIR_W, tk), lambda bi, g, s, qi_, kj_: (bi, g, kj_[s])),
        ],
        out_specs=pl.BlockSpec((1, tq, PAIRS_PER_STEP * PAIR_W), lambda bi, g, s, qi_, kj_: (bi, qi_[s], g)),
        scratch_shapes=[pltpu.VMEM((HEADS_PER_STEP, tq, QK_W), BF16), pltpu.VMEM((HEADS_PER_STEP, 1, tq), F32),
                        pltpu.VMEM((HEADS_PER_STEP, 1, tq), F32), pltpu.VMEM((HEADS_PER_STEP, D_HEAD, tq), F32),
                        pltpu.VMEM((HEADS_PER_STEP, tk, tq), F32), pltpu.VMEM((HEADS_PER_STEP, tk, tq), BF16)],
    )
    return pl.pallas_call(
        kern,
        grid_spec=grid_spec,
        out_shape=jax.ShapeDtypeStruct((b, t, W_MIX), BF16),
        compiler_params=_params("arbitrary", "arbitrary", "arbitrary"),
        name="fox_prompt",
    )(jnp.asarray(qi), jnp.asarray(kj), qa, ka, vt)


def _fox_sample_kernel(pt_ref, q_ref, kn_ref, vn_ref, lfn_ref, *rest, n_new, n_steps, n_grp):
    del pt_ref
    kp_refs, vp_refs, lfp_refs = rest[:n_grp], rest[n_grp:2 * n_grp], rest[2 * n_grp:3 * n_grp]
    o_ref, qm, m_s, l_s, acc, carry, cq_s = rest[3 * n_grp:]
    j = pl.program_id(1)
    rows = n_new * N_HEADS
    lane8 = lax.broadcasted_iota(jnp.int32, (N_HEADS, W_MIX), 1)
    head8 = lax.broadcasted_iota(jnp.int32, (N_HEADS, W_MIX), 0)
    head_sel = (lane8 >> 6) == head8
    kr = lax.broadcasted_iota(jnp.int32, (PAGE, PAGE), 0)
    kc = lax.broadcasted_iota(jnp.int32, (PAGE, PAGE), 1)
    tile = lambda x: jnp.concatenate([x] * n_new, axis=0)

    def update(s, pv):
        m_prev = m_s[...]
        m_new = jnp.maximum(m_prev, jnp.max(s, axis=1, keepdims=True))
        alpha = jnp.exp(m_prev - m_new)
        p = jnp.exp(s - m_new)
        l_s[...] = alpha * l_s[...] + jnp.sum(p, axis=1, keepdims=True)
        acc[...] = alpha * acc[...] + pv(p.astype(BF16))
        m_s[...] = m_new

    @pl.when(j == 0)
    def _():
        q = q_ref[0].astype(F32)
        qrows = [jnp.where(head_sel, jnp.broadcast_to(q[t:t + 1, :], (N_HEADS, W_MIX)), 0.0)
                 for t in range(n_new)]
        qm[...] = jnp.concatenate(qrows, axis=0).astype(BF16)
        m_s[...] = jnp.full_like(m_s, NEG)
        l_s[...] = jnp.zeros_like(l_s)
        acc[...] = jnp.zeros_like(acc)
        carry[...] = jnp.zeros_like(carry)
        cn = _dot_sel_rhs(lfn_ref[0], (kr <= kc).astype(BF16))
        for t in range(n_new):
            cq_s[t * N_HEADS:(t + 1) * N_HEADS, :] = cn[:, t:t + 1]
        s = lax.dot_general(qm[...], kn_ref[0], (((1,), (1,)), ((), ())), preferred_element_type=F32)
        s = s + (cq_s[...] - tile(cn))
        tok = lax.broadcasted_iota(jnp.int32, (rows, PAGE), 0) >> 3
        key = lax.broadcasted_iota(jnp.int32, (rows, PAGE), 1)
        s = jnp.where(key <= tok, s, NEG)
        update(s, lambda p: _bdot(p, vn_ref[0]))

    @pl.when(j > 0)
    def _():
        lfs = [ref[0] for ref in lfp_refs]
        suffix = _dot_sel_rhs(jnp.concatenate(lfs, axis=0), (kr > kc).astype(BF16))
        later = carry[...]
        cq = cq_s[...]
        parts = []
        for g in range(n_grp):
            bias8 = suffix[g * N_HEADS:(g + 1) * N_HEADS, :] + later
            s_g = _bdot(qm[...], kp_refs[g][0].astype(BF16))
            parts.append(s_g + (cq + tile(bias8)))
            later = later + jnp.sum(lfs[g], axis=1, keepdims=True)
        carry[...] = later
        s = jnp.concatenate(parts, axis=1)

        def pv(p):
            out = _bdot_nt(p[:, 0:PAGE], vp_refs[0][0].astype(BF16))
            for g in range(1, n_grp):
                out = out + _bdot_nt(p[:, g * PAGE:(g + 1) * PAGE], vp_refs[g][0].astype(BF16))
            return out

        update(s, pv)

    @pl.when(j == n_steps - 1)
    def _():
        o = acc[...] / l_s[...]
        for t in range(n_new):
            ot = jnp.where(head_sel, o[t * N_HEADS:(t + 1) * N_HEADS, :], 0.0)
            o_ref[0, t:t + 1, :] = jnp.sum(ot, axis=0, keepdims=True).astype(o_ref.dtype)


def _fox_sample(page_table, qb, kn, vn, lfn_t, pool_k, pool_v, pool_lf_t, n_new):
    r, n_pages = page_table.shape
    rows = n_new * N_HEADS
    n_grp = max(g for g in range(1, MAX_PAGES_PER_STEP + 1) if n_pages % g == 0)
    n_steps = n_pages // n_grp + 1
    kern = functools.partial(_fox_sample_kernel, n_new=n_new, n_steps=n_steps, n_grp=n_grp)

    def page(g):
        return lambda ri, j, pt: (pt[ri, n_pages - 1 - ((jnp.maximum(j, 1) - 1) * n_grp + g)], 0, 0)

    req = lambda ri, j, pt: (ri, 0, 0)
    grid_spec = pltpu.PrefetchScalarGridSpec(
        num_scalar_prefetch=1,
        grid=(r, n_steps),
        in_specs=[
            pl.BlockSpec((1, n_new, W_MIX), req),
            pl.BlockSpec((1, PAGE, W_MIX), req),
            pl.BlockSpec((1, PAGE, W_MIX), req),
            pl.BlockSpec((1, N_HEADS, PAGE), req),
        ] + [pl.BlockSpec((1, W_MIX, PAGE), page(g)) for g in range(n_grp)]
          + [pl.BlockSpec((1, W_MIX, PAGE), page(g)) for g in range(n_grp)]
          + [pl.BlockSpec((1, N_HEADS, PAGE), page(g)) for g in range(n_grp)],
        out_specs=pl.BlockSpec((1, n_new, W_MIX), req),
        scratch_shapes=[pltpu.VMEM((rows, W_MIX), BF16), pltpu.VMEM((rows, 1), F32),
                        pltpu.VMEM((rows, 1), F32), pltpu.VMEM((rows, W_MIX), F32),
                        pltpu.VMEM((N_HEADS, 1), F32), pltpu.VMEM((rows, 1), F32)],
    )
    return pl.pallas_call(
        kern,
        grid_spec=grid_spec,
        out_shape=jax.ShapeDtypeStruct((r, n_new, W_MIX), F32),
        compiler_params=_params("arbitrary", "arbitrary"),
        name="fox_sample",
    )(page_table, qb, kn, vn, lfn_t, *([pool_k] * n_grp), *([pool_v] * n_grp), *([pool_lf_t] * n_grp))


HEADS_PER_ITER = 8


def _rwkv_kernel(u_ref, sh_ref, s0_ref, mu_ref, w0_ref, w2_ref, a0_ref, a2_ref, g2_ref, kk_ref, ka_ref,
                 rk_ref, lw_ref, lb_ref, e_ref, msk_ref, mskb_ref, o_ref, sout_ref,
                 carry, st, kt_s, rt_s, kh_s, bh_s, k0_s, r0_s, khp_s, bhp_s, v_s, gt_s, o_s,
                 *, tt, t_real, t_total):
    ti = pl.program_id(1)
    nch = tt // CHUNK

    @pl.when(ti == 0)
    def _():
        carry[...] = sh_ref[0]
        st[...] = s0_ref[0]

    u = u_ref[0]
    row1 = lax.broadcasted_iota(jnp.int32, (tt, 1), 0)
    u_prev = jnp.where(row1 == 0, carry[...], pltpu.roll(u, 1, 0))
    carry[...] = u[tt - 1:tt, :]
    us = u + mu_ref[...] * (u_prev - u)
    r = us[:, 0:W_MIX]
    k = us[:, W_MIX:2 * W_MIX]
    v = us[:, 2 * W_MIX:3 * W_MIX]
    wl = us[:, 3 * W_MIX:3 * W_MIX + LANES]
    al = us[:, 3 * W_MIX + LANES:3 * W_MIX + 2 * LANES]
    gl = us[:, 3 * W_MIX + 2 * LANES:3 * W_MIX + 3 * LANES]
    z = w0_ref[...] + _dot(jnp.tanh(wl), w2_ref[...])
    logd = -EXP_NEG_HALF * _sigmoid(z)
    a = _sigmoid(a0_ref[...] + _dot(al, a2_ref[...]))
    g = _dot(_sigmoid(gl), g2_ref[...])
    e_heads = e_ref[...]
    kk = k * kk_ref[...]
    kk = kk * lax.rsqrt(jnp.maximum(_dot(kk * kk, e_heads), 1e-24))
    k_mod = k * (1.0 + (a - 1.0) * ka_ref[...])
    b = kk * a
    bonus = _dot(r * k_mod * rk_ref[...], e_heads) * v
    if t_real < t_total:
        valid = (ti * tt + row1) < t_real
        zero = lambda t: jnp.where(valid, t, 0.0)
        kk, b, k_mod, v, logd = zero(kk), zero(b), zero(k_mod), zero(v), zero(logd)

    cum = _dot_sel_lhs(msk_ref[1].astype(BF16), logd)
    if nch == 1:
        ref = _dot_sel_lhs(msk_ref[3].astype(BF16), logd)
        tot = _dot_sel_lhs(msk_ref[4].astype(BF16), logd)
    else:
        on_rows = lambda r: jnp.concatenate(
            [jnp.broadcast_to(cum[c * CHUNK + r:c * CHUNK + r + 1, :], (CHUNK, W_MIX)) for c in range(nch)], axis=0)
        ref = on_rows(CHUNK // 2 - 1)
        tot = on_rows(CHUNK - 1)
    cum_prev = cum - logd
    e_inv = jnp.exp(ref - cum)
    e_tot = jnp.exp(tot - cum)
    wide = dict(kt=kk * jnp.exp(cum_prev - ref), rt=r * jnp.exp(cum - ref), kh=k_mod * e_inv, bh=b * e_inv,
                k0=kk * jnp.exp(cum_prev), r0=r * jnp.exp(cum), khp=k_mod * e_tot, bhp=b * e_tot, v=v,
                gt=jnp.exp(tot))
    dst = dict(kt=kt_s, rt=rt_s, kh=kh_s, bh=bh_s, k0=k0_s, r0=r0_s, khp=khp_s, bhp=bhp_s, v=v_s, gt=gt_s)
    for name, val in wide.items():
        val = val.astype(dst[name].dtype)
        for h in range(N_HEADS):
            dst[name][h] = val[:, h * D_HEAD:(h + 1) * D_HEAD]

    def head_body(p, _):
        hs = [p * HEADS_PER_ITER + i for i in range(HEADS_PER_ITER)]
        ld = lambda s: [s[h] for h in hs]
        ldb = lambda s: [s[h].astype(BF16) for h in hs]
        kt, rt, kh, bh = ldb(kt_s), ldb(rt_s), ldb(kh_s), ldb(bh_s)
        strict = msk_ref[0] != 0.0
        incl = msk_ref[1] != 0.0
        a_kk = _each(lambda x, y: jnp.where(strict, _bdot_nt(x, y), 0.0).astype(BF16), kt, kh)
        a_kb = _each(lambda x, y: jnp.where(strict, _bdot_nt(x, y), 0.0).astype(BF16), kt, bh)
        a_rk = _each(lambda x, y: jnp.where(incl, _bdot_nt(x, y), 0.0).astype(BF16), rt, kh)
        a_rb = _each(lambda x, y: jnp.where(incl, _bdot_nt(x, y), 0.0).astype(BF16), rt, bh)
        x = _each(lambda t: -(t * mskb_ref[0]), a_kb)
        x2 = _each(lambda t: _bdot(t, t).astype(BF16), x)
        pm = _each(lambda t: msk_ref[2] + t.astype(F32), x)
        pm = _each(lambda q, t: q + _bdot(q.astype(BF16), t), pm, x2)
        x4 = _each(lambda t: _bdot(t, t).astype(BF16), x2)
        pm = _each(lambda q, t: (q + _bdot(q.astype(BF16), t)).astype(BF16), pm, x4)
        for lvl in (1, 2, 3):
            a_off = _each(lambda t: t * mskb_ref[lvl], a_kb)
            pa = _each(lambda q, t: _bdot(q, t).astype(BF16), pm, a_off)
            pm = _each(lambda q, s: q - _bdot(s, q).astype(BF16), pm, pa)
        t_inv = pm
        vb = ldb(v_s)
        w1 = _each(lambda t, y: _bdot(t, y), t_inv, ldb(k0_s))
        av = _each(lambda t, y: _bdot(t, y).astype(BF16), a_kk, vb)
        u1 = _each(lambda t, y: _bdot(t, y), t_inv, av)
        w1b = _each(lambda t: t.astype(BF16), w1)
        u1b = _each(lambda t: t.astype(BF16), u1)
        q_t = _each(lambda r0, t, y: (r0 - _bdot(t, y)).astype(BF16), ld(r0_s), a_rb, w1b)
        o1 = _each(lambda t, y, t2, y2: _bdot(t, y) - _bdot(t2, y2), a_rk, vb, a_rb, u1b)
        khp, bhp, gt = ld(khp_s), ld(bhp_s), ld(gt_s)
        for c in range(nch):
            rows = slice(c * CHUNK, (c + 1) * CHUNK)
            bt = _each(lambda t: t[rows].T.astype(BF16), bhp)
            kt_ = _each(lambda t: t[rows].T.astype(BF16), khp)
            gcol = _each(lambda t: t[rows].T[:, 0:1], gt)
            m_low = _each(lambda t, y: _bdot(t, y[rows]).astype(BF16), bt, w1b)
            n1 = _each(lambda t, y, t2, y2: _bdot(t, y[rows]) - _bdot(t2, y2[rows]), kt_, vb, bt, u1b)
            for i, h in enumerate(hs):
                s_t = st[h]
                s_hi = s_t.astype(BF16)
                s_lo = (s_t - s_hi.astype(F32)).astype(BF16)
                qc = q_t[i][rows]
                o_s[h, rows, :] = _bdot(qc, s_hi) + _bdot(qc, s_lo) + o1[i][rows]
                st[h] = gcol[i] * s_t - (_bdot(m_low[i], s_hi) + _bdot(m_low[i], s_lo)) + n1[i]
        return 0

    lax.fori_loop(0, N_HEADS // HEADS_PER_ITER, head_body, 0)

    o = jnp.concatenate([o_s[h] for h in range(N_HEADS)], axis=1)
    mean = _dot(o, e_heads) * (1.0 / D_HEAD)
    xc = o - mean
    var = _dot(xc * xc, e_heads) * (1.0 / D_HEAD)
    on = xc * lax.rsqrt(var + GN_EPS) * lw_ref[...] + lb_ref[...]
    o_ref[0] = ((on + bonus) * g).astype(o_ref.dtype)
    sout_ref[0] = st[...]


def _chunk_masks(tt):
    i = jnp.arange(tt)[:, None]
    j = jnp.arange(tt)[None, :]
    blk = lambda s: (i >> s) == (j >> s)
    same = blk(6)
    ms = [same & (j < i), same & (j <= i), i == j, same & ((j & (CHUNK - 1)) < CHUNK // 2), same]
    mb = [blk(3), blk(4) & ~blk(3), blk(5) & ~blk(4), same & ~blk(5)]
    return jnp.stack(ms).astype(F32), jnp.stack(mb).astype(BF16)


def _rwkv(u3, shift0, s0_t, vecs, w2p, a2p, g2, e_heads, tt, t_real):
    b, t, _ = u3.shape
    mu, w0, a0, k_k, k_a, r_k, lnw, lnb = vecs
    masks, masks_b = _chunk_masks(tt)
    kern = functools.partial(_rwkv_kernel, tt=tt, t_real=t_real, t_total=t)
    vec = lambda c: pl.BlockSpec((1, c), lambda bi, ti: (0, 0))
    mat = lambda r_, c: pl.BlockSpec((r_, c), lambda bi, ti: (0, 0))
    wide = lambda dt: pltpu.VMEM((N_HEADS, tt, D_HEAD), dt)
    return pl.pallas_call(
        kern,
        grid=(b, t // tt),
        in_specs=[
            pl.BlockSpec((1, tt, RW_PAD), lambda bi, ti: (bi, ti, 0)),
            pl.BlockSpec((1, 1, RW_PAD), lambda bi, ti: (bi, 0, 0)),
            pl.BlockSpec((1, N_HEADS, D_HEAD, D_HEAD), lambda bi, ti: (bi, 0, 0, 0)),
            vec(RW_PAD), vec(W_MIX), mat(LANES, W_MIX), vec(W_MIX), mat(LANES, W_MIX), mat(LANES, W_MIX),
            vec(W_MIX), vec(W_MIX), vec(W_MIX), vec(W_MIX), vec(W_MIX), mat(W_MIX, W_MIX),
            pl.BlockSpec(masks.shape, lambda bi, ti: (0, 0, 0)),
            pl.BlockSpec(masks_b.shape, lambda bi, ti: (0, 0, 0)),
        ],
        out_specs=[pl.BlockSpec((1, tt, W_MIX), lambda bi, ti: (bi, ti, 0)),
                   pl.BlockSpec((1, N_HEADS, D_HEAD, D_HEAD), lambda bi, ti: (bi, 0, 0, 0))],
        out_shape=[jax.ShapeDtypeStruct((b, t, W_MIX), BF16),
                   jax.ShapeDtypeStruct((b, N_HEADS, D_HEAD, D_HEAD), F32)],
        scratch_shapes=[pltpu.VMEM((1, RW_PAD), F32), pltpu.VMEM((N_HEADS, D_HEAD, D_HEAD), F32)]
                       + [wide(BF16 if name in ("kt", "rt", "kh", "bh", "k0", "v") else F32)
                          for name in ("kt", "rt", "kh", "bh", "k0", "r0", "khp", "bhp", "v", "gt", "o")],
        compiler_params=_params("arbitrary", "arbitrary"),
        name="rwkv7_chunked",
    )(u3, shift0, s0_t, mu, w0, w2p, a0, a2p, g2, k_k, k_a, r_k, lnw, lnb, e_heads, masks, masks_b)


def _mix_route_kernel(x_ref, oa_ref, ob_ref, gm_ref, wg_ref, pa_ref, pb_ref, wo_ref, gf_ref, wr_ref,
                      x1_ref, hn_ref, rw_ref):
    x = x_ref[...]
    xb = _rms(x, gm_ref[...]).astype(BF16)
    gates = _sigmoid(jnp.dot(xb, wg_ref[...], preferred_element_type=F32))
    mixed = (gates[:, :D_MODEL] * jnp.dot(oa_ref[...], pa_ref[...], preferred_element_type=F32)
             + gates[:, D_MODEL:] * jnp.dot(ob_ref[...], pb_ref[...], preferred_element_type=F32))
    x1 = x + _dot(mixed, wo_ref[...])
    x1_ref[...] = x1
    hb = _rms(x1, gf_ref[...]).astype(BF16)
    hn_ref[...] = hb
    logits = jnp.dot(hb, wr_ref[...], preferred_element_type=F32)
    tm = logits.shape[0]
    lane = lax.broadcasted_iota(jnp.int32, (tm, LANES), 1)
    lane_f = lane.astype(F32)
    first = lambda hit: jnp.min(jnp.where(hit, lane_f, 1e9), axis=1, keepdims=True)
    is_grp = (lane >= N_EXPERTS) & (lane < N_EXPERTS + N_GROUPS)
    gl = jnp.where(is_grp, logits, NEG)
    gmax = jnp.max(gl, axis=1, keepdims=True)
    g_idx = first(gl == gmax) - float(N_EXPERTS)
    p_grp = 1.0 / jnp.sum(jnp.where(is_grp, jnp.exp(gl - gmax), 0.0), axis=1, keepdims=True)
    in_grp = (lane < N_EXPERTS) & ((lane >> 3).astype(F32) == g_idx)
    el = jnp.where(in_grp, logits, NEG)
    v1 = jnp.max(el, axis=1, keepdims=True)
    i1 = first(el == v1)
    el2 = jnp.where(lane_f == i1, NEG, el)
    v2 = jnp.max(el2, axis=1, keepdims=True)
    i2 = first(el2 == v2)
    e2 = jnp.exp(v2 - v1)
    w1 = p_grp / (1.0 + e2)
    rw_ref[...] = jnp.where(lane_f == i1, w1, 0.0) + jnp.where(lane_f == i2, w1 * e2, 0.0)


def _mix_route(x2, oa, ob, gm, wg, pa, pb, wo, gf, wr, tm):
    n = x2.shape[0]
    row = lambda c: pl.BlockSpec((tm, c), lambda i: (i, 0))
    full = lambda r, c: pl.BlockSpec((r, c), lambda i: (0, 0))
    return pl.pallas_call(
        _mix_route_kernel,
        grid=(n // tm,),
        in_specs=[row(D_MODEL), row(W_MIX), row(W_MIX), full(1, D_MODEL), full(D_MODEL, 2 * D_MODEL),
                  full(W_MIX, D_MODEL), full(W_MIX, D_MODEL), full(D_MODEL, D_MODEL), full(1, D_MODEL),
                  full(D_MODEL, LANES)],
        out_specs=[row(D_MODEL), row(D_MODEL), row(LANES)],
        out_shape=[jax.ShapeDtypeStruct((n, D_MODEL), F32), jax.ShapeDtypeStruct((n, D_MODEL), BF16),
                   jax.ShapeDtypeStruct((n, LANES), F32)],
        compiler_params=_params("arbitrary"),
        name="mix_route",
    )(x2, oa, ob, gm, wg, pa, pb, wo, gf, wr)


MOE_CHUNK = 256
MOE_TAIL_CHUNK = 128


def _moe_kernel(cnt_ref, hn_ref, rw_ref, wgu_ref, wd_ref, tri_ref, out_ref, acc, *, tile, chunks):
    i = pl.program_id(0)
    g = pl.program_id(1)

    @pl.when(g == 0)
    def _():
        acc[...] = jnp.zeros_like(acc)

    rw = rw_ref[...]
    lane = lax.broadcasted_iota(jnp.int32, (tile, LANES), 1)
    in_g = (lane >> 3) == g
    member = jnp.sum(jnp.where(in_g, rw, 0.0), axis=1, keepdims=True) > 0.0
    ind = jnp.broadcast_to(jnp.where(member, 1.0, 0.0), (tile, LANES))
    rank = _bdot(tri_ref[...], ind.astype(BF16))
    ind_row = ind.T[0:1, :]
    rank_row = rank.T[0:1, :]
    count = cnt_ref[i * N_GROUPS + g]
    pieces = _split3(rw)[:2]

    for start, chunk in chunks:
        @pl.when(start < count)
        def _():
            lane_c = lax.broadcasted_iota(jnp.int32, (chunk, LANES), 1)
            slot_r = (lax.broadcasted_iota(jnp.int32, (chunk, tile), 0) + start).astype(F32)
            slot_c = (lax.broadcasted_iota(jnp.int32, (tile, chunk), 1) + start).astype(F32)
            gather = jnp.where((rank_row == slot_r) & (ind_row > 0.0), 1.0, 0.0).astype(BF16)
            scatter = jnp.where((rank[:, 0:1] == slot_c) & member, 1.0, 0.0).astype(BF16)
            xg = _bdot(gather, hn_ref[...]).astype(BF16)
            wg = sum(_bdot(gather, pc) for pc in pieces)
            acts = []
            for e in range(EXPERTS_PER_GROUP):
                h = _bdot(xg, wgu_ref[0, e])
                gate = h[:, :D_EXPERT]
                w_e = jnp.sum(jnp.where(lane_c == g * EXPERTS_PER_GROUP + e, wg, 0.0), axis=1, keepdims=True)
                acts.append((gate * _sigmoid(gate) * h[:, D_EXPERT:] * w_e).astype(BF16))
            yg = _bdot(jnp.concatenate(acts, axis=1), wd_ref[0])
            acc[...] += _bdot(scatter, yg.astype(BF16))

    @pl.when(g == N_GROUPS - 1)
    def _():
        out_ref[...] = acc[...]


def _moe(hn, rw, wgu, wd, tile):
    n = hn.shape[0]
    chunks = [(0, min(MOE_CHUNK, tile))]
    while sum(chunks[-1]) < tile:
        chunks.append((sum(chunks[-1]), MOE_TAIL_CHUNK))
    n_tiles = n // tile
    grp_w = rw[:, :N_EXPERTS].reshape(n_tiles, tile, N_GROUPS, EXPERTS_PER_GROUP).sum(-1)
    counts = (grp_w > 0.0).sum(1).astype(jnp.int32).reshape(-1)
    t = jnp.arange(tile)
    tri = (t[None, :] < t[:, None]).astype(BF16)
    kern = functools.partial(_moe_kernel, tile=tile, chunks=tuple(chunks))
    row = lambda c: pl.BlockSpec((tile, c), lambda i, g, cnt: (i, 0))
    grid_spec = pltpu.PrefetchScalarGridSpec(
        num_scalar_prefetch=1,
        grid=(n_tiles, N_GROUPS),
        in_specs=[row(D_MODEL), row(LANES),
                  pl.BlockSpec((1, EXPERTS_PER_GROUP, D_MODEL, 2 * D_EXPERT), lambda i, g, cnt: (g, 0, 0, 0)),
                  pl.BlockSpec((1, EXPERTS_PER_GROUP * D_EXPERT, D_MODEL), lambda i, g, cnt: (g, 0, 0)),
                  pl.BlockSpec((tile, tile), lambda i, g, cnt: (0, 0))],
        out_specs=row(D_MODEL),
        scratch_shapes=[pltpu.VMEM((tile, D_MODEL), F32)],
    )
    return pl.pallas_call(
        kern,
        grid_spec=grid_spec,
        out_shape=jax.ShapeDtypeStruct((n, D_MODEL), F32),
        compiler_params=_params("arbitrary", "arbitrary"),
        name="moe_grouped",
    )(counts, hn, rw, wgu, wd, tri)


def _final_norm_kernel(x_ref, m_ref, g_ref, y_ref):
    y_ref[...] = _rms(x_ref[...] + m_ref[...], g_ref[...])


def _final_norm(x1, moe_out, gn, tm):
    n = x1.shape[0]
    row = pl.BlockSpec((tm, D_MODEL), lambda i: (i, 0))
    return pl.pallas_call(
        _final_norm_kernel,
        grid=(n // tm,),
        in_specs=[row, row, pl.BlockSpec((1, D_MODEL), lambda i: (0, 0))],
        out_specs=row,
        out_shape=jax.ShapeDtypeStruct((n, D_MODEL), F32),
        compiler_params=_params("arbitrary"),
        name="final_norm",
    )(x1, moe_out, gn)


def _pad_rw_cols(m):
    z = jnp.zeros(m.shape[:-1] + (LANES - LORA_W,), m.dtype)
    c = 3 * W_MIX
    return jnp.concatenate([m[..., :c], m[..., c:c + LORA_W], z, m[..., c + LORA_W:c + 2 * LORA_W], z,
                            m[..., c + 2 * LORA_W:]], axis=-1)


def _unpad_rw_cols(m):
    c = 3 * W_MIX
    return jnp.concatenate([m[..., :c], m[..., c:c + LORA_W], m[..., c + LANES:c + LANES + LORA_W],
                            m[..., c + 2 * LANES:]], axis=-1)


def _pad_rows(m, rows):
    return jnp.concatenate([m, jnp.zeros((rows - m.shape[0],) + m.shape[1:], m.dtype)], axis=0)


def _prep_weights(l, norm_mix, w_in, b_f, mu_rw, w0, w2, a0, a2, g2, k_k, k_a, r_k, lnx_w, lnx_b,
                  p_a, p_b, w_o, norm_ffn, w_grp, w_exp, we_gate, we_up, we_down):
    w = w_in[l]
    w_fl = jnp.concatenate([w[:, 3 * W_MIX:FOX_COLS], jnp.zeros((D_MODEL, LANES - N_HEADS), F32)], axis=1)
    w_rw = _pad_rw_cols(w[:, FOX_COLS:FOX_COLS + RW_COLS])
    row = lambda vct: vct.reshape(1, -1)
    head_id = jnp.arange(W_MIX) // D_HEAD
    return dict(
        gm=row(norm_mix[l]),
        w_a=jnp.concatenate([w[:, :3 * W_MIX], w_fl, w_rw], axis=1).astype(BF16),
        w_kvt=w[:, W_MIX:3 * W_MIX].T.astype(BF16),
        bf=jnp.concatenate([b_f[l], jnp.zeros((LANES - N_HEADS,), F32)]).reshape(1, LANES),
        wg=w[:, FOX_COLS + RW_COLS:].astype(BF16),
        vecs=(row(_pad_rw_cols(mu_rw[l])), row(w0[l]), row(a0[l]), row(k_k[l]), row(k_a[l]),
              row(r_k[l].reshape(-1)), row(lnx_w[l]), row(lnx_b[l])),
        w2p=_pad_rows(w2[l], LANES).astype(BF16),
        a2p=_pad_rows(a2[l], LANES).astype(BF16),
        g2=g2[l].astype(BF16),
        e_heads=(head_id[:, None] == head_id[None, :]).astype(BF16),
        pa=p_a[l].astype(BF16), pb=p_b[l].astype(BF16), wo=w_o[l].astype(BF16),
        gf=row(norm_ffn[l]),
        wr=jnp.concatenate([w_exp[l], w_grp[l], jnp.zeros((D_MODEL, LANES - N_EXPERTS - N_GROUPS), F32)],
                           axis=1).astype(BF16),
        wgu=jnp.concatenate([we_gate[l], we_up[l]], axis=-1).astype(BF16).reshape(
            N_GROUPS, EXPERTS_PER_GROUP, D_MODEL, 2 * D_EXPERT),
        wd=we_down[l].astype(BF16).reshape(N_GROUPS, EXPERTS_PER_GROUP * D_EXPERT, D_MODEL),
    )


def _layer(x, past, shift0, wkv0, page_table, p, gn, tiles):
    b, t, _ = x.shape
    n = b * t
    tm, tq, tt, tm_moe = tiles
    x2 = x.reshape(n, D_MODEL)
    to3 = lambda m: m.reshape(b, t, m.shape[-1])
    if past is None:
        qb, kb, lf, rw, k_t, v_t, vt = _inproj(x2, p["gm"], p["w_a"], p["w_kvt"], p["bf"], tm,
                                               QK_SCALE * LOG2E, seq=t)
        qa, ka = _qk_aug(to3(lf), to3(qb), to3(kb), tm)
        o_a = _fox_prompt(qa, ka, vt, tq)
        t_pad = t
        u3 = to3(rw)
        k, v = (m.reshape(b, N_HEADS, D_HEAD, t).transpose(0, 3, 1, 2) for m in (k_t, v_t))
    else:
        qb, kb, lf, rw, k, v, vb = _inproj(x2, p["gm"], p["w_a"], p["w_kvt"], p["bf"], tm, QK_SCALE)
        pool_k, pool_v, pool_lf_t = past
        pad_keys = lambda m: jnp.concatenate([to3(m), jnp.zeros((b, PAGE - t, W_MIX), BF16)], axis=1)
        lfn_t = jnp.concatenate([to3(lf)[:, :, :N_HEADS].transpose(0, 2, 1),
                                 jnp.zeros((b, N_HEADS, PAGE - t), F32)], axis=2)
        o_a = _fox_sample(page_table, to3(qb).astype(F32), pad_keys(kb), pad_keys(vb), lfn_t,
                          pool_k, pool_v, pool_lf_t, t).astype(BF16)
        t_pad = tt
        u3 = jnp.concatenate([to3(rw), jnp.zeros((b, t_pad - t, RW_PAD), F32)], axis=1)
    o_b, wkv_t = _rwkv(u3, shift0, jnp.swapaxes(wkv0, -1, -2), p["vecs"], p["w2p"], p["a2p"], p["g2"],
                       p["e_heads"], tt, t)
    o_b = o_b[:, :t].reshape(n, W_MIX)
    wkv_new = jnp.swapaxes(wkv_t, -1, -2)
    x1, hn, route = _mix_route(x2, o_a.reshape(n, W_MIX), o_b, p["gm"], p["wg"], p["pa"], p["pb"], p["wo"],
                               p["gf"], p["wr"], tm)
    y = _final_norm(x1, _moe(hn, route, p["wgu"], p["wd"], tm_moe), gn, tm)
    shift_new = _unpad_rw_cols(to3(rw)[:, t - 1, :])
    return (y.reshape(b, t, D_MODEL), k.reshape(b, t, N_HEADS, D_HEAD), v.reshape(b, t, N_HEADS, D_HEAD),
            to3(lf)[:, :, :N_HEADS], wkv_new, shift_new)


def kernel(x_prompt, x_sample, cache_k, cache_v, cache_logf, page_table, state_wkv, state_shift, norm_mix, w_in, b_f, mu_rw, w0, w2, a0, a2, g2, k_k, k_a, r_k, lnx_w, lnx_b, p_a, p_b, w_o, norm_ffn, w_grp, w_exp, we_gate, we_up, we_down, norm_final):
    depth = w_in.shape[0]
    assert depth == 1, "final norm is fused into the layer's last kernel"
    bp, tp, _ = x_prompt.shape
    bs, ts, _ = x_sample.shape
    n_phys = cache_k.shape[1]
    gn = norm_final.reshape(1, D_MODEL)
    l = 0
    p = _prep_weights(l, norm_mix, w_in, b_f, mu_rw, w0, w2, a0, a2, g2, k_k, k_a, r_k, lnx_w, lnx_b,
                      p_a, p_b, w_o, norm_ffn, w_grp, w_exp, we_gate, we_up, we_down)
    tq = min(1024, tp)
    prompt = _layer(x_prompt, None, jnp.zeros((bp, 1, RW_PAD), F32),
                    jnp.zeros((bp, N_HEADS, D_HEAD, D_HEAD), F32), None, p, gn,
                    (min(512, bp * tp), tq, min(256, tp), min(1024, bp * tp)))
    past = (cache_k[l].transpose(0, 2, 3, 1).reshape(n_phys, W_MIX, PAGE),
            cache_v[l].transpose(0, 2, 3, 1).reshape(n_phys, W_MIX, PAGE),
            cache_logf[l].transpose(0, 2, 1))
    sample = _layer(x_sample, past, _pad_rw_cols(state_shift[l])[:, None, :], state_wkv[l], page_table, p, gn,
                    (bs * ts, None, CHUNK, bs * ts))
    outs = []
    for y, k, v, lf, wkv, sh in (prompt, sample):
        outs.append((y, k[None], v[None], lf[None], wkv[None], sh[None]))
    (yp, kp, vp, lp, wp, sp), (ys, ks, vs, ls, ws, ss) = outs
    return (yp, ys, kp, vp, lp, wp, sp, ks, vs, ls, ws, ss)
```

```python
import functools

import numpy as np
import jax
import jax.numpy as jnp
from jax import lax
from jax.experimental import pallas as pl
from jax.experimental.pallas import tpu as pltpu

F32 = jnp.float32
BF16 = jnp.bfloat16

D_MODEL = 1024
N_HEADS = 8
D_HEAD = 64
W_MIX = N_HEADS * D_HEAD
LORA_W = 64
LORA_G = 128
RW_COLS = 3 * W_MIX + 2 * LORA_W + LORA_G
RW_PAD = 3 * W_MIX + 3 * 128
FOX_COLS = 3 * W_MIX + N_HEADS
N_EXPERTS = 32
EXPERTS_PER_GROUP = 8
N_GROUPS = 4
D_EXPERT = 256
PAGE = 128
MAX_PAGES_PER_STEP = 16
NORM_EPS = 1e-6
GN_EPS = 64e-5
QK_SCALE = D_HEAD ** -0.5
LANES = 128
CHUNK = 64
EXP_NEG_HALF = 0.6065306597126334
NEG = -1e30
VMEM_LIMIT = 56 * 1024 * 1024


def _dot(a, b):
    return jnp.dot(a.astype(BF16), b.astype(BF16), preferred_element_type=F32)


def _bdot(a, b):
    return jnp.dot(a, b, preferred_element_type=F32)


def _bdot_nt(a, b):
    return lax.dot_general(a, b, (((1,), (1,)), ((), ())), preferred_element_type=F32)


def _each(f, *lists):
    return [f(*xs) for xs in zip(*lists)]


def _split3(x):
    hi = x.astype(BF16)
    r1 = x - hi.astype(F32)
    mid = r1.astype(BF16)
    lo = (r1 - mid.astype(F32)).astype(BF16)
    return hi, mid, lo


def _dot_sel_lhs(sel, x):
    hi, mid, lo = _split3(x)
    d = lambda p: jnp.dot(sel, p, preferred_element_type=F32)
    return d(hi) + d(mid) + d(lo)


def _dot_sel_rhs(x, sel):
    hi, mid, lo = _split3(x)
    d = lambda p: jnp.dot(p, sel, preferred_element_type=F32)
    return d(hi) + d(mid) + d(lo)


def _sigmoid(x):
    return 0.5 * jnp.tanh(0.5 * x) + 0.5


def _log_sigmoid(x):
    return jnp.minimum(x, 0.0) - jnp.log(1.0 + jnp.exp(-jnp.abs(x)))


def _rms(x, g):
    return (x * lax.rsqrt(jnp.mean(x * x, axis=-1, keepdims=True) + NORM_EPS)) * g


def _params(*sem):
    return pltpu.CompilerParams(dimension_semantics=sem, vmem_limit_bytes=VMEM_LIMIT)


def _inproj_kernel(x_ref, g_ref, w_ref, wkvt_ref, bf_ref, q_ref, kb_ref, lf_ref, rw_ref, *kv_refs,
                   q_scale, seq_major_kv):
    xb = _rms(x_ref[...], g_ref[...]).astype(BF16)
    d = lambda lo, hi: jnp.dot(xb, w_ref[:, lo:hi], preferred_element_type=F32)
    q_ref[...] = (d(0, W_MIX) * q_scale).astype(BF16)
    k = d(W_MIX, 2 * W_MIX)
    kb_ref[...] = k.astype(BF16)
    lf_ref[...] = _log_sigmoid(d(3 * W_MIX, 3 * W_MIX + LANES) + bf_ref[...])
    rw_ref[...] = d(3 * W_MIX + LANES, 3 * W_MIX + LANES + RW_PAD)
    k_ref, v_ref, vb_ref = kv_refs
    if seq_major_kv:
        k_ref[0] = _bdot_nt(wkvt_ref[0:W_MIX, :], xb)
        vt = _bdot_nt(wkvt_ref[W_MIX:2 * W_MIX, :], xb)
        v_ref[0] = vt
        vb_ref[0] = vt.astype(BF16)
    else:
        v = d(2 * W_MIX, 3 * W_MIX)
        k_ref[...] = k
        v_ref[...] = v
        vb_ref[...] = v.astype(BF16)


def _inproj(x2, g, w_a, w_kvt, bf_pad, tm, q_scale, seq=None):
    n = x2.shape[0]
    wcols = w_a.shape[1]
    row = lambda c: pl.BlockSpec((tm, c), lambda i: (i, 0))
    full = lambda r, c: pl.BlockSpec((r, c), lambda i: (0, 0))
    out_specs = [row(W_MIX), row(W_MIX), row(LANES), row(RW_PAD)]
    out_shape = [jax.ShapeDtypeStruct((n, W_MIX), BF16), jax.ShapeDtypeStruct((n, W_MIX), BF16),
                 jax.ShapeDtypeStruct((n, LANES), F32), jax.ShapeDtypeStruct((n, RW_PAD), F32)]
    if seq is not None:
        per_seq = seq // tm
        kv_spec = pl.BlockSpec((1, W_MIX, tm), lambda i: (i // per_seq, 0, i % per_seq))
        kv_shape = lambda dt: jax.ShapeDtypeStruct((n // seq, W_MIX, seq), dt)
    else:
        kv_spec = row(W_MIX)
        kv_shape = lambda dt: jax.ShapeDtypeStruct((n, W_MIX), dt)
    out_specs += [kv_spec] * 3
    out_shape += [kv_shape(F32), kv_shape(F32), kv_shape(BF16)]
    return pl.pallas_call(
        functools.partial(_inproj_kernel, q_scale=q_scale, seq_major_kv=seq is not None),
        grid=(n // tm,),
        in_specs=[row(D_MODEL), full(1, D_MODEL), full(D_MODEL, wcols), full(2 * W_MIX, D_MODEL), full(1, LANES)],
        out_specs=out_specs,
        out_shape=out_shape,
        compiler_params=_params("arbitrary"),
        name="inproj",
    )(x2, g, w_a, w_kvt, bf_pad)


PAIR_W = 2 * D_HEAD
N_PAIRS = N_HEADS // 2
QK_W = PAIR_W + LANES
PAIRS_PER_STEP = 2
HEADS_PER_STEP = 2 * PAIRS_PER_STEP
KEY_BLK = 32
LOG2E = 1.4426950408889634


def _bias_tables():
    pq = np.zeros((3, LANES, N_PAIRS * LANES), np.float32)
    pk = np.zeros((3, LANES, N_PAIRS * LANES), np.float32)
    one_q = np.zeros((1, N_PAIRS * LANES), np.float32)
    one_k = np.zeros((1, N_PAIRS * LANES), np.float32)
    for h in range(N_HEADS):
        base = LANES * (h // 2)
        for piece in range(3):
            pq[piece, h, base + 3 * (h % 2) + piece] = 1.0
            pk[piece, h, base + 6 + 3 * (h % 2) + piece] = -1.0
    for g in range(N_PAIRS):
        one_q[0, LANES * g + 6:LANES * g + 12] = 1.0
        one_k[0, LANES * g + 0:LANES * g + 6] = 1.0
    return (jnp.asarray(pq, BF16), jnp.asarray(pk, BF16), jnp.asarray(one_q), jnp.asarray(one_k))


def _qk_aug_kernel(lf_ref, q_ref, k_ref, pq_ref, pk_ref, oq_ref, ok_ref, qa_ref, ka_ref, carry):
    @pl.when(pl.program_id(1) == 0)
    def _():
        carry[...] = jnp.zeros_like(carry)

    lf = lf_ref[0] * LOG2E
    tm = lf.shape[0]
    row = lax.broadcasted_iota(jnp.int32, (tm, tm), 0)
    col = lax.broadcasted_iota(jnp.int32, (tm, tm), 1)
    c = _dot_sel_lhs((col <= row).astype(BF16), lf) + carry[...]
    carry[...] = c[tm - 1:tm, :]
    pieces = _split3(c)
    aug_q = oq_ref[...] + sum(_bdot(pc, pq_ref[i]) for i, pc in enumerate(pieces))
    aug_k = ok_ref[...] + sum(_bdot(pc, pk_ref[i]) for i, pc in enumerate(pieces))
    q = q_ref[0]
    k = k_ref[0]
    for g in range(N_PAIRS):
        qa_ref[0, :, g * QK_W:g * QK_W + PAIR_W] = q[:, g * PAIR_W:(g + 1) * PAIR_W]
        qa_ref[0, :, g * QK_W + PAIR_W:(g + 1) * QK_W] = aug_q[:, g * LANES:(g + 1) * LANES].astype(BF16)
        ka_ref[0, :, g * QK_W:g * QK_W + PAIR_W] = k[:, g * PAIR_W:(g + 1) * PAIR_W]
        ka_ref[0, :, g * QK_W + PAIR_W:(g + 1) * QK_W] = aug_k[:, g * LANES:(g + 1) * LANES].astype(BF16)


def _qk_aug(lf3, qb, kb, tm):
    b, t, _ = lf3.shape
    pq, pk, one_q, one_k = _bias_tables()
    blk = lambda c: pl.BlockSpec((1, tm, c), lambda i, j: (i, j, 0))
    cst = lambda a: pl.BlockSpec(a.shape, lambda i, j: (0,) * a.ndim)
    return pl.pallas_call(
        _qk_aug_kernel,
        grid=(b, t // tm),
        in_specs=[blk(LANES), blk(W_MIX), blk(W_MIX), cst(pq), cst(pk), cst(one_q), cst(one_k)],
        out_specs=[blk(N_PAIRS * QK_W), blk(N_PAIRS * QK_W)],
        out_shape=[jax.ShapeDtypeStruct((b, t, N_PAIRS * QK_W), BF16)] * 2,
        scratch_shapes=[pltpu.VMEM((1, LANES), F32)],
        compiler_params=_params("arbitrary", "arbitrary"),
        name="logf_cumsum_qk_aug",
    )(lf3, qb, kb, pq, pk, one_q, one_k)


def _fox_prompt_kernel(qi_ref, kj_ref, q_ref, k_ref, vt_ref, o_ref, qm, m_s, l_s, acc, s_scr, p_scr,
                       *, tq, tk):
    step = pl.program_id(2)
    i = qi_ref[step]
    j = kj_ref[step]

    @pl.when(j == 0)
    def _():
        lane = lax.broadcasted_iota(jnp.int32, (tq, QK_W), 1)
        off = lane - PAIR_W
        for hh in range(HEADS_PER_STEP):
            pr, h = divmod(hh, 2)
            q = q_ref[0, :, pr * QK_W:(pr + 1) * QK_W].astype(F32)
            keep = ((lane >= h * D_HEAD) & (lane < (h + 1) * D_HEAD)) \
                | ((off >= 3 * h) & (off < 3 * h + 3)) | ((off >= 6 + 3 * h) & (off < 9 + 3 * h))
            qm[hh] = jnp.where(keep, q, 0.0).astype(BF16)
        m_s[...] = jnp.full_like(m_s, NEG)
        l_s[...] = jnp.zeros_like(l_s)
        acc[...] = jnp.zeros_like(acc)

    def block(diag):
        key = lax.broadcasted_iota(jnp.int32, (KEY_BLK, tq), 0)
        qry = lax.broadcasted_iota(jnp.int32, (KEY_BLK, tq), 1)
        n_kb = tk // KEY_BLK
        groups = lambda t: [t[8 * r:8 * (r + 1), :] for r in range(KEY_BLK // 8)]
        ones = jnp.ones((16, tk), BF16)

        def scores(h, kb):
            s = s_scr[h, kb * KEY_BLK:(kb + 1) * KEY_BLK, :]
            if diag:
                s = jnp.where(key + kb * KEY_BLK <= qry, s, NEG)
            return s

        for h in range(HEADS_PER_STEP):
            pr = h // 2
            s_scr[h] = _bdot_nt(k_ref[0, :, pr * QK_W:(pr + 1) * QK_W], qm[h])
        for h in range(HEADS_PER_STEP):
            m8 = jnp.full((8, tq), NEG, F32)
            for kb in range(n_kb):
                for g in groups(scores(h, kb)):
                    m8 = jnp.maximum(m8, g)
            m_prev = m_s[h]
            m_new = jnp.maximum(m_prev, jnp.max(m8, axis=0, keepdims=True))
            alpha = jnp.exp2(m_prev - m_new)
            for kb in range(n_kb):
                p_scr[h, kb * KEY_BLK:(kb + 1) * KEY_BLK, :] = jnp.exp2(scores(h, kb) - m_new).astype(BF16)
            m_s[h] = m_new
            vt = jnp.concatenate([vt_ref[0, h * D_HEAD:(h + 1) * D_HEAD, :], ones], axis=0)
            pv = _bdot(vt, p_scr[h])
            acc[h] = alpha * acc[h] + pv[:D_HEAD]
            l_s[h] = alpha * l_s[h] + pv[D_HEAD:D_HEAD + 1]

    @pl.when(j < i)
    def _():
        block(False)

    @pl.when(j == i)
    def _():
        block(True)
        o_t = jnp.concatenate([acc[h] / l_s[h] for h in range(HEADS_PER_STEP)], axis=0)
        o_ref[0] = o_t.T.astype(o_ref.dtype)


def _fox_prompt(qa, ka, vt, tq):
    b, _, t = vt.shape
    tk = tq
    nq = t // tq
    qi = np.array([i for i in range(nq) for _ in range(i + 1)], np.int32)
    kj = np.array([j for i in range(nq) for j in range(i + 1)], np.int32)
    kern = functools.partial(_fox_prompt_kernel, tq=tq, tk=tk)
    grid_spec = pltpu.PrefetchScalarGridSpec(
        num_scalar_prefetch=2,
        grid=(b, N_PAIRS // PAIRS_PER_STEP, len(qi)),
        in_specs=[
            pl.BlockSpec((1, tq, PAIRS_PER_STEP * QK_W), lambda bi, g, s, qi_, kj_: (bi, qi_[s], g)),
            pl.BlockSpec((1, tk, PAIRS_PER_STEP * QK_W), lambda bi, g, s, qi_, kj_: (bi, kj_[s], g)),
            pl.BlockSpec((1, PAIRS_PER_STEP * PAIR_W, tk), lambda bi, g, s, qi_, kj_: (bi, g, kj_[s])),
        ],
        out_specs=pl.BlockSpec((1, tq, PAIRS_PER_STEP * PAIR_W), lambda bi, g, s, qi_, kj_: (bi, qi_[s], g)),
        scratch_shapes=[pltpu.VMEM((HEADS_PER_STEP, tq, QK_W), BF16), pltpu.VMEM((HEADS_PER_STEP, 1, tq), F32),
                        pltpu.VMEM((HEADS_PER_STEP, 1, tq), F32), pltpu.VMEM((HEADS_PER_STEP, D_HEAD, tq), F32),
                        pltpu.VMEM((HEADS_PER_STEP, tk, tq), F32), pltpu.VMEM((HEADS_PER_STEP, tk, tq), BF16)],
    )
    return pl.pallas_call(
        kern,
        grid_spec=grid_spec,
        out_shape=jax.ShapeDtypeStruct((b, t, W_MIX), BF16),
        compiler_params=_params("arbitrary", "arbitrary", "arbitrary"),
        name="fox_prompt",
    )(jnp.asarray(qi), jnp.asarray(kj), qa, ka, vt)


def _fox_sample_kernel(pt_ref, q_ref, kn_ref, vn_ref, lfn_ref, *rest, n_new, n_steps, n_grp):
    del pt_ref
    kp_refs, vp_refs, lfp_refs = rest[:n_grp], rest[n_grp:2 * n_grp], rest[2 * n_grp:3 * n_grp]
    o_ref, qm, m_s, l_s, acc, carry, cq_s = rest[3 * n_grp:]
    j = pl.program_id(1)
    rows = n_new * N_HEADS
    lane8 = lax.broadcasted_iota(jnp.int32, (N_HEADS, W_MIX), 1)
    head8 = lax.broadcasted_iota(jnp.int32, (N_HEADS, W_MIX), 0)
    head_sel = (lane8 >> 6) == head8
    kr = lax.broadcasted_iota(jnp.int32, (PAGE, PAGE), 0)
    kc = lax.broadcasted_iota(jnp.int32, (PAGE, PAGE), 1)
    tile = lambda x: jnp.concatenate([x] * n_new, axis=0)

    def update(s, pv):
        m_prev = m_s[...]
        m_new = jnp.maximum(m_prev, jnp.max(s, axis=1, keepdims=True))
        alpha = jnp.exp(m_prev - m_new)
        p = jnp.exp(s - m_new)
        l_s[...] = alpha * l_s[...] + jnp.sum(p, axis=1, keepdims=True)
        acc[...] = alpha * acc[...] + pv(p.astype(BF16))
        m_s[...] = m_new

    @pl.when(j == 0)
    def _():
        q = q_ref[0].astype(F32)
        qrows = [jnp.where(head_sel, jnp.broadcast_to(q[t:t + 1, :], (N_HEADS, W_MIX)), 0.0)
                 for t in range(n_new)]
        qm[...] = jnp.concatenate(qrows, axis=0).astype(BF16)
        m_s[...] = jnp.full_like(m_s, NEG)
        l_s[...] = jnp.zeros_like(l_s)
        acc[...] = jnp.zeros_like(acc)
        carry[...] = jnp.zeros_like(carry)
        cn = _dot_sel_rhs(lfn_ref[0], (kr <= kc).astype(BF16))
        for t in range(n_new):
            cq_s[t * N_HEADS:(t + 1) * N_HEADS, :] = cn[:, t:t + 1]
        s = lax.dot_general(qm[...], kn_ref[0], (((1,), (1,)), ((), ())), preferred_element_type=F32)
        s = s + (cq_s[...] - tile(cn))
        tok = lax.broadcasted_iota(jnp.int32, (rows, PAGE), 0) >> 3
        key = lax.broadcasted_iota(jnp.int32, (rows, PAGE), 1)
        s = jnp.where(key <= tok, s, NEG)
        update(s, lambda p: _bdot(p, vn_ref[0]))

    @pl.when(j > 0)
    def _():
        lfs = [ref[0] for ref in lfp_refs]
        suffix = _dot_sel_rhs(jnp.concatenate(lfs, axis=0), (kr > kc).astype(BF16))
        later = carry[...]
        cq = cq_s[...]
        parts = []
        for g in range(n_grp):
            bias8 = suffix[g * N_HEADS:(g + 1) * N_HEADS, :] + later
            s_g = _bdot(qm[...], kp_refs[g][0].astype(BF16))
            parts.append(s_g + (cq + tile(bias8)))
            later = later + jnp.sum(lfs[g], axis=1, keepdims=True)
        carry[...] = later
        s = jnp.concatenate(parts, axis=1)

        def pv(p):
            out = _bdot_nt(p[:, 0:PAGE], vp_refs[0][0].astype(BF16))
            for g in range(1, n_grp):
                out = out + _bdot_nt(p[:, g * PAGE:(g + 1) * PAGE], vp_refs[g][0].astype(BF16))
            return out

        update(s, pv)

    @pl.when(j == n_steps - 1)
    def _():
        o = acc[...] / l_s[...]
        for t in range(n_new):
            ot = jnp.where(head_sel, o[t * N_HEADS:(t + 1) * N_HEADS, :], 0.0)
            o_ref[0, t:t + 1, :] = jnp.sum(ot, axis=0, keepdims=True).astype(o_ref.dtype)


def _fox_sample(page_table, qb, kn, vn, lfn_t, pool_k, pool_v, pool_lf_t, n_new):
    r, n_pages = page_table.shape
    rows = n_new * N_HEADS
    n_grp = max(g for g in range(1, MAX_PAGES_PER_STEP + 1) if n_pages % g == 0)
    n_steps = n_pages // n_grp + 1
    kern = functools.partial(_fox_sample_kernel, n_new=n_new, n_steps=n_steps, n_grp=n_grp)

    def page(g):
        return lambda ri, j, pt: (pt[ri, n_pages - 1 - ((jnp.maximum(j, 1) - 1) * n_grp + g)], 0, 0)

    req = lambda ri, j, pt: (ri, 0, 0)
    grid_spec = pltpu.PrefetchScalarGridSpec(
        num_scalar_prefetch=1,
        grid=(r, n_steps),
        in_specs=[
            pl.BlockSpec((1, n_new, W_MIX), req),
            pl.BlockSpec((1, PAGE, W_MIX), req),
            pl.BlockSpec((1, PAGE, W_MIX), req),
            pl.BlockSpec((1, N_HEADS, PAGE), req),
        ] + [pl.BlockSpec((1, W_MIX, PAGE), page(g)) for g in range(n_grp)]
          + [pl.BlockSpec((1, W_MIX, PAGE), page(g)) for g in range(n_grp)]
          + [pl.BlockSpec((1, N_HEADS, PAGE), page(g)) for g in range(n_grp)],
        out_specs=pl.BlockSpec((1, n_new, W_MIX), req),
        scratch_shapes=[pltpu.VMEM((rows, W_MIX), BF16), pltpu.VMEM((rows, 1), F32),
                        pltpu.VMEM((rows, 1), F32), pltpu.VMEM((rows, W_MIX), F32),
                        pltpu.VMEM((N_HEADS, 1), F32), pltpu.VMEM((rows, 1), F32)],
    )
    return pl.pallas_call(
        kern,
        grid_spec=grid_spec,
        out_shape=jax.ShapeDtypeStruct((r, n_new, W_MIX), F32),
        compiler_params=_params("arbitrary", "arbitrary"),
        name="fox_sample",
    )(page_table, qb, kn, vn, lfn_t, *([pool_k] * n_grp), *([pool_v] * n_grp), *([pool_lf_t] * n_grp))


HEADS_PER_ITER = 8


def _rwkv_kernel(u_ref, sh_ref, s0_ref, mu_ref, w0_ref, w2_ref, a0_ref, a2_ref, g2_ref, kk_ref, ka_ref,
                 rk_ref, lw_ref, lb_ref, e_ref, msk_ref, mskb_ref, o_ref, sout_ref,
                 carry, st, kt_s, rt_s, kh_s, bh_s, k0_s, r0_s, khp_s, bhp_s, v_s, gt_s, o_s,
                 *, tt, t_real, t_total):
    ti = pl.program_id(1)
    nch = tt // CHUNK

    @pl.when(ti == 0)
    def _():
        carry[...] = sh_ref[0]
        st[...] = s0_ref[0]

    u = u_ref[0]
    row1 = lax.broadcasted_iota(jnp.int32, (tt, 1), 0)
    u_prev = jnp.where(row1 == 0, carry[...], pltpu.roll(u, 1, 0))
    carry[...] = u[tt - 1:tt, :]
    us = u + mu_ref[...] * (u_prev - u)
    r = us[:, 0:W_MIX]
    k = us[:, W_MIX:2 * W_MIX]
    v = us[:, 2 * W_MIX:3 * W_MIX]
    wl = us[:, 3 * W_MIX:3 * W_MIX + LANES]
    al = us[:, 3 * W_MIX + LANES:3 * W_MIX + 2 * LANES]
    gl = us[:, 3 * W_MIX + 2 * LANES:3 * W_MIX + 3 * LANES]
    z = w0_ref[...] + _dot(jnp.tanh(wl), w2_ref[...])
    logd = -EXP_NEG_HALF * _sigmoid(z)
    a = _sigmoid(a0_ref[...] + _dot(al, a2_ref[...]))
    g = _dot(_sigmoid(gl), g2_ref[...])
    e_heads = e_ref[...]
    kk = k * kk_ref[...]
    kk = kk * lax.rsqrt(jnp.maximum(_dot(kk * kk, e_heads), 1e-24))
    k_mod = k * (1.0 + (a - 1.0) * ka_ref[...])
    b = kk * a
    bonus = _dot(r * k_mod * rk_ref[...], e_heads) * v
    if t_real < t_total:
        valid = (ti * tt + row1) < t_real
        zero = lambda t: jnp.where(valid, t, 0.0)
        kk, b, k_mod, v, logd = zero(kk), zero(b), zero(k_mod), zero(v), zero(logd)

    cum = _dot_sel_lhs(msk_ref[1].astype(BF16), logd)
    if nch == 1:
        ref = _dot_sel_lhs(msk_ref[3].astype(BF16), logd)
        tot = _dot_sel_lhs(msk_ref[4].astype(BF16), logd)
    else:
        on_rows = lambda r: jnp.concatenate(
            [jnp.broadcast_to(cum[c * CHUNK + r:c * CHUNK + r + 1, :], (CHUNK, W_MIX)) for c in range(nch)], axis=0)
        ref = on_rows(CHUNK // 2 - 1)
        tot = on_rows(CHUNK - 1)
    cum_prev = cum - logd
    e_inv = jnp.exp(ref - cum)
    e_tot = jnp.exp(tot - cum)
    wide = dict(kt=kk * jnp.exp(cum_prev - ref), rt=r * jnp.exp(cum - ref), kh=k_mod * e_inv, bh=b * e_inv,
                k0=kk * jnp.exp(cum_prev), r0=r * jnp.exp(cum), khp=k_mod * e_tot, bhp=b * e_tot, v=v,
                gt=jnp.exp(tot))
    dst = dict(kt=kt_s, rt=rt_s, kh=kh_s, bh=bh_s, k0=k0_s, r0=r0_s, khp=khp_s, bhp=bhp_s, v=v_s, gt=gt_s)
    for name, val in wide.items():
        val = val.astype(dst[name].dtype)
        for h in range(N_HEADS):
            dst[name][h] = val[:, h * D_HEAD:(h + 1) * D_HEAD]

    def head_body(p, _):
        hs = [p * HEADS_PER_ITER + i for i in range(HEADS_PER_ITER)]
        ld = lambda s: [s[h] for h in hs]
        ldb = lambda s: [s[h].astype(BF16) for h in hs]
        kt, rt, kh, bh = ldb(kt_s), ldb(rt_s), ldb(kh_s), ldb(bh_s)
        strict = msk_ref[0] != 0.0
        incl = msk_ref[1] != 0.0
        a_kk = _each(lambda x, y: jnp.where(strict, _bdot_nt(x, y), 0.0).astype(BF16), kt, kh)
        a_kb = _each(lambda x, y: jnp.where(strict, _bdot_nt(x, y), 0.0).astype(BF16), kt, bh)
        a_rk = _each(lambda x, y: jnp.where(incl, _bdot_nt(x, y), 0.0).astype(BF16), rt, kh)
        a_rb = _each(lambda x, y: jnp.where(incl, _bdot_nt(x, y), 0.0).astype(BF16), rt, bh)
        x = _each(lambda t: -(t * mskb_ref[0]), a_kb)
        x2 = _each(lambda t: _bdot(t, t).astype(BF16), x)
        pm = _each(lambda t: msk_ref[2] + t.astype(F32), x)
        pm = _each(lambda q, t: q + _bdot(q.astype(BF16), t), pm, x2)
        x4 = _each(lambda t: _bdot(t, t).astype(BF16), x2)
        pm = _each(lambda q, t: (q + _bdot(q.astype(BF16), t)).astype(BF16), pm, x4)
        for lvl in (1, 2, 3):
            a_off = _each(lambda t: t * mskb_ref[lvl], a_kb)
            pa = _each(lambda q, t: _bdot(q, t).astype(BF16), pm, a_off)
            pm = _each(lambda q, s: q - _bdot(s, q).astype(BF16), pm, pa)
        t_inv = pm
        vb = ldb(v_s)
        w1 = _each(lambda t, y: _bdot(t, y), t_inv, ldb(k0_s))
        av = _each(lambda t, y: _bdot(t, y).astype(BF16), a_kk, vb)
        u1 = _each(lambda t, y: _bdot(t, y), t_inv, av)
        w1b = _each(lambda t: t.astype(BF16), w1)
        u1b = _each(lambda t: t.astype(BF16), u1)
        q_t = _each(lambda r0, t, y: (r0 - _bdot(t, y)).astype(BF16), ld(r0_s), a_rb, w1b)
        o1 = _each(lambda t, y, t2, y2: _bdot(t, y) - _bdot(t2, y2), a_rk, vb, a_rb, u1b)
        khp, bhp, gt = ld(khp_s), ld(bhp_s), ld(gt_s)
        for c in range(nch):
            rows = slice(c * CHUNK, (c + 1) * CHUNK)
            bt = _each(lambda t: t[rows].T.astype(BF16), bhp)
            kt_ = _each(lambda t: t[rows].T.astype(BF16), khp)
            gcol = _each(lambda t: t[rows].T[:, 0:1], gt)
            m_low = _each(lambda t, y: _bdot(t, y[rows]).astype(BF16), bt, w1b)
            n1 = _each(lambda t, y, t2, y2: _bdot(t, y[rows]) - _bdot(t2, y2[rows]), kt_, vb, bt, u1b)
            for i, h in enumerate(hs):
                s_t = st[h]
                s_hi = s_t.astype(BF16)
                s_lo = (s_t - s_hi.astype(F32)).astype(BF16)
                qc = q_t[i][rows]
                o_s[h, rows, :] = _bdot(qc, s_hi) + _bdot(qc, s_lo) + o1[i][rows]
                st[h] = gcol[i] * s_t - (_bdot(m_low[i], s_hi) + _bdot(m_low[i], s_lo)) + n1[i]
        return 0

    lax.fori_loop(0, N_HEADS // HEADS_PER_ITER, head_body, 0)

    o = jnp.concatenate([o_s[h] for h in range(N_HEADS)], axis=1)
    mean = _dot(o, e_heads) * (1.0 / D_HEAD)
    xc = o - mean
    var = _dot(xc * xc, e_heads) * (1.0 / D_HEAD)
    on = xc * lax.rsqrt(var + GN_EPS) * lw_ref[...] + lb_ref[...]
    o_ref[0] = ((on + bonus) * g).astype(o_ref.dtype)
    sout_ref[0] = st[...]


def _chunk_masks(tt):
    i = jnp.arange(tt)[:, None]
    j = jnp.arange(tt)[None, :]
    blk = lambda s: (i >> s) == (j >> s)
    same = blk(6)
    ms = [same & (j < i), same & (j <= i), i == j, same & ((j & (CHUNK - 1)) < CHUNK // 2), same]
    mb = [blk(3), blk(4) & ~blk(3), blk(5) & ~blk(4), same & ~blk(5)]
    return jnp.stack(ms).astype(F32), jnp.stack(mb).astype(BF16)


def _rwkv(u3, shift0, s0_t, vecs, w2p, a2p, g2, e_heads, tt, t_real):
    b, t, _ = u3.shape
    mu, w0, a0, k_k, k_a, r_k, lnw, lnb = vecs
    masks, masks_b = _chunk_masks(tt)
    kern = functools.partial(_rwkv_kernel, tt=tt, t_real=t_real, t_total=t)
    vec = lambda c: pl.BlockSpec((1, c), lambda bi, ti: (0, 0))
    mat = lambda r_, c: pl.BlockSpec((r_, c), lambda bi, ti: (0, 0))
    wide = lambda dt: pltpu.VMEM((N_HEADS, tt, D_HEAD), dt)
    return pl.pallas_call(
        kern,
        grid=(b, t // tt),
        in_specs=[
            pl.BlockSpec((1, tt, RW_PAD), lambda bi, ti: (bi, ti, 0)),
            pl.BlockSpec((1, 1, RW_PAD), lambda bi, ti: (bi, 0, 0)),
            pl.BlockSpec((1, N_HEADS, D_HEAD, D_HEAD), lambda bi, ti: (bi, 0, 0, 0)),
            vec(RW_PAD), vec(W_MIX), mat(LANES, W_MIX), vec(W_MIX), mat(LANES, W_MIX), mat(LANES, W_MIX),
            vec(W_MIX), vec(W_MIX), vec(W_MIX), vec(W_MIX), vec(W_MIX), mat(W_MIX, W_MIX),
            pl.BlockSpec(masks.shape, lambda bi, ti: (0, 0, 0)),
            pl.BlockSpec(masks_b.shape, lambda bi, ti: (0, 0, 0)),
        ],
        out_specs=[pl.BlockSpec((1, tt, W_MIX), lambda bi, ti: (bi, ti, 0)),
                   pl.BlockSpec((1, N_HEADS, D_HEAD, D_HEAD), lambda bi, ti: (bi, 0, 0, 0))],
        out_shape=[jax.ShapeDtypeStruct((b, t, W_MIX), BF16),
                   jax.ShapeDtypeStruct((b, N_HEADS, D_HEAD, D_HEAD), F32)],
        scratch_shapes=[pltpu.VMEM((1, RW_PAD), F32), pltpu.VMEM((N_HEADS, D_HEAD, D_HEAD), F32)]
                       + [wide(BF16 if name in ("kt", "rt", "kh", "bh", "k0", "v") else F32)
                          for name in ("kt", "rt", "kh", "bh", "k0", "r0", "khp", "bhp", "v", "gt", "o")],
        compiler_params=_params("arbitrary", "arbitrary"),
        name="rwkv7_chunked",
    )(u3, shift0, s0_t, mu, w0, w2p, a0, a2p, g2, k_k, k_a, r_k, lnw, lnb, e_heads, masks, masks_b)


def _mix_route_kernel(x_ref, oa_ref, ob_ref, gm_ref, wg_ref, pa_ref, pb_ref, wo_ref, gf_ref, wr_ref,
                      x1_ref, hn_ref, rw_ref):
    x = x_ref[...]
    xb = _rms(x, gm_ref[...]).astype(BF16)
    gate = lambda lo: _sigmoid(_bdot(xb, wg_ref[:, lo:lo + D_MODEL]))
    mixed = gate(0) * _bdot(oa_ref[...], pa_ref[...])
    mixed = mixed + gate(D_MODEL) * _bdot(ob_ref[...], pb_ref[...])
    x1 = x + _dot(mixed, wo_ref[...])
    x1_ref[...] = x1
    hb = _rms(x1, gf_ref[...]).astype(BF16)
    hn_ref[...] = hb
    logits = jnp.dot(hb, wr_ref[...], preferred_element_type=F32)
    tm = logits.shape[0]
    lane = lax.broadcasted_iota(jnp.int32, (tm, LANES), 1)
    lane_f = lane.astype(F32)
    first = lambda hit: jnp.min(jnp.where(hit, lane_f, 1e9), axis=1, keepdims=True)
    is_grp = (lane >= N_EXPERTS) & (lane < N_EXPERTS + N_GROUPS)
    gl = jnp.where(is_grp, logits, NEG)
    gmax = jnp.max(gl, axis=1, keepdims=True)
    g_idx = first(gl == gmax) - float(N_EXPERTS)
    p_grp = 1.0 / jnp.sum(jnp.where(is_grp, jnp.exp(gl - gmax), 0.0), axis=1, keepdims=True)
    in_grp = (lane < N_EXPERTS) & ((lane >> 3).astype(F32) == g_idx)
    el = jnp.where(in_grp, logits, NEG)
    v1 = jnp.max(el, axis=1, keepdims=True)
    i1 = first(el == v1)
    el2 = jnp.where(lane_f == i1, NEG, el)
    v2 = jnp.max(el2, axis=1, keepdims=True)
    i2 = first(el2 == v2)
    e2 = jnp.exp(v2 - v1)
    w1 = p_grp / (1.0 + e2)
    rw_ref[...] = jnp.where(lane_f == i1, w1, 0.0) + jnp.where(lane_f == i2, w1 * e2, 0.0)


def _mix_route(x2, oa, ob, gm, wg, pa, pb, wo, gf, wr, tm):
    n = x2.shape[0]
    row = lambda c: pl.BlockSpec((tm, c), lambda i: (i, 0))
    full = lambda r, c: pl.BlockSpec((r, c), lambda i: (0, 0))
    return pl.pallas_call(
        _mix_route_kernel,
        grid=(n // tm,),
        in_specs=[row(D_MODEL), row(W_MIX), row(W_MIX), full(1, D_MODEL), full(D_MODEL, 2 * D_MODEL),
                  full(W_MIX, D_MODEL), full(W_MIX, D_MODEL), full(D_MODEL, D_MODEL), full(1, D_MODEL),
                  full(D_MODEL, LANES)],
        out_specs=[row(D_MODEL), row(D_MODEL), row(LANES)],
        out_shape=[jax.ShapeDtypeStruct((n, D_MODEL), F32), jax.ShapeDtypeStruct((n, D_MODEL), BF16),
                   jax.ShapeDtypeStruct((n, LANES), F32)],
        compiler_params=_params("arbitrary"),
        name="mix_route",
    )(x2, oa, ob, gm, wg, pa, pb, wo, gf, wr)


MOE_CHUNK = 256
MOE_TAIL_CHUNK = 128
MOE_VMEM_LIMIT = 60 * 1024 * 1024


def _moe_kernel(cnt_ref, hn_ref, rw_ref, wgu_ref, wd_ref, tri_ref, x1_ref, gn_ref, out_ref, acc, *, tile, chunks):
    i = pl.program_id(0)
    g = pl.program_id(1)

    @pl.when(g == 0)
    def _():
        acc[...] = jnp.zeros_like(acc)

    rw = rw_ref[...]
    lane = lax.broadcasted_iota(jnp.int32, (tile, LANES), 1)
    in_g = (lane >> 3) == g
    member = jnp.sum(jnp.where(in_g, rw, 0.0), axis=1, keepdims=True) > 0.0
    ind = jnp.broadcast_to(jnp.where(member, 1.0, 0.0), (tile, LANES))
    rank = _bdot(tri_ref[...], ind.astype(BF16))
    ind_row = ind.T[0:1, :]
    rank_row = rank.T[0:1, :]
    count = cnt_ref[i * N_GROUPS + g]
    pieces = _split3(rw)[:2]

    for start, chunk in chunks:
        @pl.when(start < count)
        def _():
            lane_c = lax.broadcasted_iota(jnp.int32, (chunk, LANES), 1)
            slot_r = (lax.broadcasted_iota(jnp.int32, (chunk, tile), 0) + start).astype(F32)
            slot_c = (lax.broadcasted_iota(jnp.int32, (tile, chunk), 1) + start).astype(F32)
            gather = jnp.where((rank_row == slot_r) & (ind_row > 0.0), 1.0, 0.0).astype(BF16)
            scatter = jnp.where((rank[:, 0:1] == slot_c) & member, 1.0, 0.0).astype(BF16)
            xg = _bdot(gather, hn_ref[...]).astype(BF16)
            wg = sum(_bdot(gather, pc) for pc in pieces)
            acts = []
            for e in range(EXPERTS_PER_GROUP):
                h = _bdot(xg, wgu_ref[0, e])
                gate = h[:, :D_EXPERT]
                w_e = jnp.sum(jnp.where(lane_c == g * EXPERTS_PER_GROUP + e, wg, 0.0), axis=1, keepdims=True)
                acts.append((gate * _sigmoid(gate) * h[:, D_EXPERT:] * w_e).astype(BF16))
            yg = _bdot(jnp.concatenate(acts, axis=1), wd_ref[0])
            acc[...] += _bdot(scatter, yg.astype(BF16))

    @pl.when(g == N_GROUPS - 1)
    def _():
        out_ref[...] = _rms(x1_ref[...] + acc[...], gn_ref[...])


def _moe(hn, rw, wgu, wd, x1, gn, tile):
    n = hn.shape[0]
    chunks = [(0, min(MOE_CHUNK, tile))]
    while sum(chunks[-1]) < tile:
        chunks.append((sum(chunks[-1]), MOE_TAIL_CHUNK))
    n_tiles = n // tile
    grp_w = rw[:, :N_EXPERTS].reshape(n_tiles, tile, N_GROUPS, EXPERTS_PER_GROUP).sum(-1)
    counts = (grp_w > 0.0).sum(1).astype(jnp.int32).reshape(-1)
    t = jnp.arange(tile)
    tri = (t[None, :] < t[:, None]).astype(BF16)
    kern = functools.partial(_moe_kernel, tile=tile, chunks=tuple(chunks))
    row = lambda c: pl.BlockSpec((tile, c), lambda i, g, cnt: (i, 0))
    grid_spec = pltpu.PrefetchScalarGridSpec(
        num_scalar_prefetch=1,
        grid=(n_tiles, N_GROUPS),
        in_specs=[row(D_MODEL), row(LANES),
                  pl.BlockSpec((1, EXPERTS_PER_GROUP, D_MODEL, 2 * D_EXPERT), lambda i, g, cnt: (g, 0, 0, 0)),
                  pl.BlockSpec((1, EXPERTS_PER_GROUP * D_EXPERT, D_MODEL), lambda i, g, cnt: (g, 0, 0)),
                  pl.BlockSpec((tile, tile), lambda i, g, cnt: (0, 0)),
                  row(D_MODEL), pl.BlockSpec((1, D_MODEL), lambda i, g, cnt: (0, 0))],
        out_specs=row(D_MODEL),
        scratch_shapes=[pltpu.VMEM((tile, D_MODEL), F32)],
    )
    return pl.pallas_call(
        kern,
        grid_spec=grid_spec,
        out_shape=jax.ShapeDtypeStruct((n, D_MODEL), F32),
        compiler_params=pltpu.CompilerParams(dimension_semantics=("arbitrary", "arbitrary"),
                                             vmem_limit_bytes=MOE_VMEM_LIMIT),
        name="moe_grouped",
    )(counts, hn, rw, wgu, wd, tri, x1, gn)


def _pad_rw_cols(m):
    z = jnp.zeros(m.shape[:-1] + (LANES - LORA_W,), m.dtype)
    c = 3 * W_MIX
    return jnp.concatenate([m[..., :c], m[..., c:c + LORA_W], z, m[..., c + LORA_W:c + 2 * LORA_W], z,
                            m[..., c + 2 * LORA_W:]], axis=-1)


def _unpad_rw_cols(m):
    c = 3 * W_MIX
    return jnp.concatenate([m[..., :c], m[..., c:c + LORA_W], m[..., c + LANES:c + LANES + LORA_W],
                            m[..., c + 2 * LANES:]], axis=-1)


def _pad_rows(m, rows):
    return jnp.concatenate([m, jnp.zeros((rows - m.shape[0],) + m.shape[1:], m.dtype)], axis=0)


def _prep_weights(l, norm_mix, w_in, b_f, mu_rw, w0, w2, a0, a2, g2, k_k, k_a, r_k, lnx_w, lnx_b,
                  p_a, p_b, w_o, norm_ffn, w_grp, w_exp, we_gate, we_up, we_down):
    w = w_in[l]
    w_fl = jnp.concatenate([w[:, 3 * W_MIX:FOX_COLS], jnp.zeros((D_MODEL, LANES - N_HEADS), F32)], axis=1)
    w_rw = _pad_rw_cols(w[:, FOX_COLS:FOX_COLS + RW_COLS])
    row = lambda vct: vct.reshape(1, -1)
    head_id = jnp.arange(W_MIX) // D_HEAD
    return dict(
        gm=row(norm_mix[l]),
        w_a=jnp.concatenate([w[:, :3 * W_MIX], w_fl, w_rw], axis=1).astype(BF16),
        w_kvt=w[:, W_MIX:3 * W_MIX].T.astype(BF16),
        bf=jnp.concatenate([b_f[l], jnp.zeros((LANES - N_HEADS,), F32)]).reshape(1, LANES),
        wg=w[:, FOX_COLS + RW_COLS:].astype(BF16),
        vecs=(row(_pad_rw_cols(mu_rw[l])), row(w0[l]), row(a0[l]), row(k_k[l]), row(k_a[l]),
              row(r_k[l].reshape(-1)), row(lnx_w[l]), row(lnx_b[l])),
        w2p=_pad_rows(w2[l], LANES).astype(BF16),
        a2p=_pad_rows(a2[l], LANES).astype(BF16),
        g2=g2[l].astype(BF16),
        e_heads=(head_id[:, None] == head_id[None, :]).astype(BF16),
        pa=p_a[l].astype(BF16), pb=p_b[l].astype(BF16), wo=w_o[l].astype(BF16),
        gf=row(norm_ffn[l]),
        wr=jnp.concatenate([w_exp[l], w_grp[l], jnp.zeros((D_MODEL, LANES - N_EXPERTS - N_GROUPS), F32)],
                           axis=1).astype(BF16),
        wgu=jnp.concatenate([we_gate[l], we_up[l]], axis=-1).astype(BF16).reshape(
            N_GROUPS, EXPERTS_PER_GROUP, D_MODEL, 2 * D_EXPERT),
        wd=we_down[l].astype(BF16).reshape(N_GROUPS, EXPERTS_PER_GROUP * D_EXPERT, D_MODEL),
    )


def _layer(x, past, shift0, wkv0, page_table, p, gn, tiles):
    b, t, _ = x.shape
    n = b * t
    tm, tq, tt, tm_moe = tiles
    x2 = x.reshape(n, D_MODEL)
    to3 = lambda m: m.reshape(b, t, m.shape[-1])
    if past is None:
        qb, kb, lf, rw, k_t, v_t, vt = _inproj(x2, p["gm"], p["w_a"], p["w_kvt"], p["bf"], tm,
                                               QK_SCALE * LOG2E, seq=t)
        qa, ka = _qk_aug(to3(lf), to3(qb), to3(kb), tm)
        o_a = _fox_prompt(qa, ka, vt, tq)
        t_pad = t
        u3 = to3(rw)
        k, v = (m.reshape(b, N_HEADS, D_HEAD, t).transpose(0, 3, 1, 2) for m in (k_t, v_t))
    else:
        qb, kb, lf, rw, k, v, vb = _inproj(x2, p["gm"], p["w_a"], p["w_kvt"], p["bf"], tm, QK_SCALE)
        pool_k, pool_v, pool_lf_t = past
        pad_keys = lambda m: jnp.concatenate([to3(m), jnp.zeros((b, PAGE - t, W_MIX), BF16)], axis=1)
        lfn_t = jnp.concatenate([to3(lf)[:, :, :N_HEADS].transpose(0, 2, 1),
                                 jnp.zeros((b, N_HEADS, PAGE - t), F32)], axis=2)
        o_a = _fox_sample(page_table, to3(qb).astype(F32), pad_keys(kb), pad_keys(vb), lfn_t,
                          pool_k, pool_v, pool_lf_t, t).astype(BF16)
        t_pad = tt
        u3 = jnp.concatenate([to3(rw), jnp.zeros((b, t_pad - t, RW_PAD), F32)], axis=1)
    o_b, wkv_t = _rwkv(u3, shift0, jnp.swapaxes(wkv0, -1, -2), p["vecs"], p["w2p"], p["a2p"], p["g2"],
                       p["e_heads"], tt, t)
    o_b = o_b[:, :t].reshape(n, W_MIX)
    wkv_new = jnp.swapaxes(wkv_t, -1, -2)
    x1, hn, route = _mix_route(x2, o_a.reshape(n, W_MIX), o_b, p["gm"], p["wg"], p["pa"], p["pb"], p["wo"],
                               p["gf"], p["wr"], tm_moe)
    y = _moe(hn, route, p["wgu"], p["wd"], x1, gn, tm_moe)
    shift_new = _unpad_rw_cols(to3(rw)[:, t - 1, :])
    return (y.reshape(b, t, D_MODEL), k.reshape(b, t, N_HEADS, D_HEAD), v.reshape(b, t, N_HEADS, D_HEAD),
            to3(lf)[:, :, :N_HEADS], wkv_new, shift_new)


def kernel(x_prompt, x_sample, cache_k, cache_v, cache_logf, page_table, state_wkv, state_shift, norm_mix, w_in, b_f, mu_rw, w0, w2, a0, a2, g2, k_k, k_a, r_k, lnx_w, lnx_b, p_a, p_b, w_o, norm_ffn, w_grp, w_exp, we_gate, we_up, we_down, norm_final):
    depth = w_in.shape[0]
    assert depth == 1, "final norm is fused into the layer's last kernel"
    bp, tp, _ = x_prompt.shape
    bs, ts, _ = x_sample.shape
    n_phys = cache_k.shape[1]
    gn = norm_final.reshape(1, D_MODEL)
    l = 0
    p = _prep_weights(l, norm_mix, w_in, b_f, mu_rw, w0, w2, a0, a2, g2, k_k, k_a, r_k, lnx_w, lnx_b,
                      p_a, p_b, w_o, norm_ffn, w_grp, w_exp, we_gate, we_up, we_down)
    tq = min(1024, tp)
    prompt = _layer(x_prompt, None, jnp.zeros((bp, 1, RW_PAD), F32),
                    jnp.zeros((bp, N_HEADS, D_HEAD, D_HEAD), F32), None, p, gn,
                    (min(512, bp * tp), tq, min(256, tp), min(1024, bp * tp)))
    past = (cache_k[l].transpose(0, 2, 3, 1).reshape(n_phys, W_MIX, PAGE),
            cache_v[l].transpose(0, 2, 3, 1).reshape(n_phys, W_MIX, PAGE),
            cache_logf[l].transpose(0, 2, 1))
    sample = _layer(x_sample, past, _pad_rw_cols(state_shift[l])[:, None, :], state_wkv[l], page_table, p, gn,
                    (bs * ts, None, CHUNK, bs * ts))
    outs = []
    for y, k, v, lf, wkv, sh in (prompt, sample):
        outs.append((y, k[None], v[None], lf[None], wkv[None], sh[None]))
    (yp, kp, vp, lp, wp, sp), (ys, ks, vs, ls, ws, ss) = outs
    return (yp, ys, kp, vp, lp, wp, sp, ks, vs, ls, ws, ss)
```

```python
import functools

import numpy as np
import jax
import jax.numpy as jnp
from jax import lax
from jax.experimental import pallas as pl
from jax.experimental.pallas import tpu as pltpu

F32 = jnp.float32
BF16 = jnp.bfloat16

D_MODEL = 1024
N_HEADS = 8
D_HEAD = 64
W_MIX = N_HEADS * D_HEAD
LORA_W = 64
LORA_G = 128
RW_COLS = 3 * W_MIX + 2 * LORA_W + LORA_G
RW_PAD = 3 * W_MIX + 3 * 128
FOX_COLS = 3 * W_MIX + N_HEADS
N_EXPERTS = 32
EXPERTS_PER_GROUP = 8
N_GROUPS = 4
D_EXPERT = 256
PAGE = 128
MAX_PAGES_PER_STEP = 32
NORM_EPS = 1e-6
GN_EPS = 64e-5
QK_SCALE = D_HEAD ** -0.5
LANES = 128
CHUNK = 64
EXP_NEG_HALF = 0.6065306597126334
NEG = -1e30
VMEM_LIMIT = 56 * 1024 * 1024


def _dot(a, b):
    return jnp.dot(a.astype(BF16), b.astype(BF16), preferred_element_type=F32)


def _bdot(a, b):
    return jnp.dot(a, b, preferred_element_type=F32)


def _bdot_nt(a, b):
    return lax.dot_general(a, b, (((1,), (1,)), ((), ())), preferred_element_type=F32)


def _each(f, *lists):
    return [f(*xs) for xs in zip(*lists)]


def _split3(x):
    hi = x.astype(BF16)
    r1 = x - hi.astype(F32)
    mid = r1.astype(BF16)
    lo = (r1 - mid.astype(F32)).astype(BF16)
    return hi, mid, lo


def _dot_sel_lhs(sel, x):
    hi, mid, lo = _split3(x)
    d = lambda p: jnp.dot(sel, p, preferred_element_type=F32)
    return d(hi) + d(mid) + d(lo)


def _dot_sel_rhs(x, sel):
    hi, mid, lo = _split3(x)
    d = lambda p: jnp.dot(p, sel, preferred_element_type=F32)
    return d(hi) + d(mid) + d(lo)


def _sigmoid(x):
    return 0.5 * jnp.tanh(0.5 * x) + 0.5


def _log_sigmoid(x):
    return jnp.minimum(x, 0.0) - jnp.log(1.0 + jnp.exp(-jnp.abs(x)))


def _rms(x, g):
    return (x * lax.rsqrt(jnp.mean(x * x, axis=-1, keepdims=True) + NORM_EPS)) * g


def _params(*sem):
    return pltpu.CompilerParams(dimension_semantics=sem, vmem_limit_bytes=VMEM_LIMIT)


def _inproj_kernel(x_ref, g_ref, w_ref, wkvt_ref, bf_ref, q_ref, kb_ref, lf_ref, rw_ref, *kv_refs,
                   q_scale, seq_major_kv):
    xb = _rms(x_ref[...], g_ref[...]).astype(BF16)
    d = lambda lo, hi: jnp.dot(xb, w_ref[:, lo:hi], preferred_element_type=F32)
    q_ref[...] = (d(0, W_MIX) * q_scale).astype(BF16)
    k = d(W_MIX, 2 * W_MIX)
    kb_ref[...] = k.astype(BF16)
    lf_ref[...] = _log_sigmoid(d(3 * W_MIX, 3 * W_MIX + LANES) + bf_ref[...])
    rw_ref[...] = d(3 * W_MIX + LANES, 3 * W_MIX + LANES + RW_PAD)
    k_ref, v_ref, vb_ref = kv_refs
    if seq_major_kv:
        k_ref[0] = _bdot_nt(wkvt_ref[0:W_MIX, :], xb)
        vt = _bdot_nt(wkvt_ref[W_MIX:2 * W_MIX, :], xb)
        v_ref[0] = vt
        vb_ref[0] = vt.astype(BF16)
    else:
        v = d(2 * W_MIX, 3 * W_MIX)
        k_ref[...] = k
        v_ref[...] = v
        vb_ref[...] = v.astype(BF16)


def _inproj(x2, g, w_a, w_kvt, bf_pad, tm, q_scale, seq=None):
    n = x2.shape[0]
    wcols = w_a.shape[1]
    row = lambda c: pl.BlockSpec((tm, c), lambda i: (i, 0))
    full = lambda r, c: pl.BlockSpec((r, c), lambda i: (0, 0))
    out_specs = [row(W_MIX), row(W_MIX), row(LANES), row(RW_PAD)]
    out_shape = [jax.ShapeDtypeStruct((n, W_MIX), BF16), jax.ShapeDtypeStruct((n, W_MIX), BF16),
                 jax.ShapeDtypeStruct((n, LANES), F32), jax.ShapeDtypeStruct((n, RW_PAD), F32)]
    if seq is not None:
        per_seq = seq // tm
        kv_spec = pl.BlockSpec((1, W_MIX, tm), lambda i: (i // per_seq, 0, i % per_seq))
        kv_shape = lambda dt: jax.ShapeDtypeStruct((n // seq, W_MIX, seq), dt)
    else:
        kv_spec = row(W_MIX)
        kv_shape = lambda dt: jax.ShapeDtypeStruct((n, W_MIX), dt)
    out_specs += [kv_spec] * 3
    out_shape += [kv_shape(F32), kv_shape(F32), kv_shape(BF16)]
    return pl.pallas_call(
        functools.partial(_inproj_kernel, q_scale=q_scale, seq_major_kv=seq is not None),
        grid=(n // tm,),
        in_specs=[row(D_MODEL), full(1, D_MODEL), full(D_MODEL, wcols), full(2 * W_MIX, D_MODEL), full(1, LANES)],
        out_specs=out_specs,
        out_shape=out_shape,
        compiler_params=_params("arbitrary"),
        name="inproj",
    )(x2, g, w_a, w_kvt, bf_pad)


PAIR_W = 2 * D_HEAD
N_PAIRS = N_HEADS // 2
QK_W = PAIR_W + LANES
PAIRS_PER_STEP = 2
HEADS_PER_STEP = 2 * PAIRS_PER_STEP
KEY_BLK = 32
LOG2E = 1.4426950408889634


def _bias_tables():
    pq = np.zeros((3, LANES, N_PAIRS * LANES), np.float32)
    pk = np.zeros((3, LANES, N_PAIRS * LANES), np.float32)
    one_q = np.zeros((1, N_PAIRS * LANES), np.float32)
    one_k = np.zeros((1, N_PAIRS * LANES), np.float32)
    for h in range(N_HEADS):
        base = LANES * (h // 2)
        for piece in range(3):
            pq[piece, h, base + 3 * (h % 2) + piece] = 1.0
            pk[piece, h, base + 6 + 3 * (h % 2) + piece] = -1.0
    for g in range(N_PAIRS):
        one_q[0, LANES * g + 6:LANES * g + 12] = 1.0
        one_k[0, LANES * g + 0:LANES * g + 6] = 1.0
    return (jnp.asarray(pq, BF16), jnp.asarray(pk, BF16), jnp.asarray(one_q), jnp.asarray(one_k))


def _qk_aug_kernel(lf_ref, q_ref, k_ref, pq_ref, pk_ref, oq_ref, ok_ref, qa_ref, ka_ref, carry):
    @pl.when(pl.program_id(1) == 0)
    def _():
        carry[...] = jnp.zeros_like(carry)

    lf = lf_ref[0] * LOG2E
    tm = lf.shape[0]
    row = lax.broadcasted_iota(jnp.int32, (tm, tm), 0)
    col = lax.broadcasted_iota(jnp.int32, (tm, tm), 1)
    c = _dot_sel_lhs((col <= row).astype(BF16), lf) + carry[...]
    carry[...] = c[tm - 1:tm, :]
    pieces = _split3(c)
    aug_q = oq_ref[...] + sum(_bdot(pc, pq_ref[i]) for i, pc in enumerate(pieces))
    aug_k = ok_ref[...] + sum(_bdot(pc, pk_ref[i]) for i, pc in enumerate(pieces))
    q = q_ref[0]
    k = k_ref[0]
    for g in range(N_PAIRS):
        qa_ref[0, :, g * QK_W:g * QK_W + PAIR_W] = q[:, g * PAIR_W:(g + 1) * PAIR_W]
        qa_ref[0, :, g * QK_W + PAIR_W:(g + 1) * QK_W] = aug_q[:, g * LANES:(g + 1) * LANES].astype(BF16)
        ka_ref[0, :, g * QK_W:g * QK_W + PAIR_W] = k[:, g * PAIR_W:(g + 1) * PAIR_W]
        ka_ref[0, :, g * QK_W + PAIR_W:(g + 1) * QK_W] = aug_k[:, g * LANES:(g + 1) * LANES].astype(BF16)


def _qk_aug(lf3, qb, kb, tm):
    b, t, _ = lf3.shape
    pq, pk, one_q, one_k = _bias_tables()
    blk = lambda c: pl.BlockSpec((1, tm, c), lambda i, j: (i, j, 0))
    cst = lambda a: pl.BlockSpec(a.shape, lambda i, j: (0,) * a.ndim)
    return pl.pallas_call(
        _qk_aug_kernel,
        grid=(b, t // tm),
        in_specs=[blk(LANES), blk(W_MIX), blk(W_MIX), cst(pq), cst(pk), cst(one_q), cst(one_k)],
        out_specs=[blk(N_PAIRS * QK_W), blk(N_PAIRS * QK_W)],
        out_shape=[jax.ShapeDtypeStruct((b, t, N_PAIRS * QK_W), BF16)] * 2,
        scratch_shapes=[pltpu.VMEM((1, LANES), F32)],
        compiler_params=_params("arbitrary", "arbitrary"),
        name="logf_cumsum_qk_aug",
    )(lf3, qb, kb, pq, pk, one_q, one_k)


def _fox_prompt_kernel(qi_ref, kj_ref, q_ref, k_ref, vt_ref, o_ref, qm, m_s, l_s, acc, s_scr, p_scr,
                       *, tq, tk):
    step = pl.program_id(2)
    i = qi_ref[step]
    j = kj_ref[step]

    @pl.when(j == 0)
    def _():
        lane = lax.broadcasted_iota(jnp.int32, (tq, QK_W), 1)
        off = lane - PAIR_W
        for hh in range(HEADS_PER_STEP):
            pr, h = divmod(hh, 2)
            q = q_ref[0, :, pr * QK_W:(pr + 1) * QK_W].astype(F32)
            keep = ((lane >= h * D_HEAD) & (lane < (h + 1) * D_HEAD)) \
                | ((off >= 3 * h) & (off < 3 * h + 3)) | ((off >= 6 + 3 * h) & (off < 9 + 3 * h))
            qm[hh] = jnp.where(keep, q, 0.0).astype(BF16)
        m_s[...] = jnp.full_like(m_s, NEG)
        l_s[...] = jnp.zeros_like(l_s)
        acc[...] = jnp.zeros_like(acc)

    def block(diag):
        key = lax.broadcasted_iota(jnp.int32, (KEY_BLK, tq), 0)
        qry = lax.broadcasted_iota(jnp.int32, (KEY_BLK, tq), 1)
        n_kb = tk // KEY_BLK
        groups = lambda t: [t[8 * r:8 * (r + 1), :] for r in range(KEY_BLK // 8)]
        ones = jnp.ones((16, tk), BF16)

        def scores(h, kb):
            s = s_scr[h, kb * KEY_BLK:(kb + 1) * KEY_BLK, :]
            if diag:
                s = jnp.where(key + kb * KEY_BLK <= qry, s, NEG)
            return s

        for h in range(HEADS_PER_STEP):
            pr = h // 2
            s_scr[h] = _bdot_nt(k_ref[0, :, pr * QK_W:(pr + 1) * QK_W], qm[h])
        for h in range(HEADS_PER_STEP):
            m8 = jnp.full((8, tq), NEG, F32)
            for kb in range(n_kb):
                for g in groups(scores(h, kb)):
                    m8 = jnp.maximum(m8, g)
            m_prev = m_s[h]
            m_new = jnp.maximum(m_prev, jnp.max(m8, axis=0, keepdims=True))
            alpha = jnp.exp2(m_prev - m_new)
            for kb in range(n_kb):
                p_scr[h, kb * KEY_BLK:(kb + 1) * KEY_BLK, :] = jnp.exp2(scores(h, kb) - m_new).astype(BF16)
            m_s[h] = m_new
            vt = jnp.concatenate([vt_ref[0, h * D_HEAD:(h + 1) * D_HEAD, :], ones], axis=0)
            pv = _bdot(vt, p_scr[h])
            acc[h] = alpha * acc[h] + pv[:D_HEAD]
            l_s[h] = alpha * l_s[h] + pv[D_HEAD:D_HEAD + 1]

    @pl.when(j < i)
    def _():
        block(False)

    @pl.when(j == i)
    def _():
        block(True)
        o_t = jnp.concatenate([acc[h] / l_s[h] for h in range(HEADS_PER_STEP)], axis=0)
        o_ref[0] = o_t.T.astype(o_ref.dtype)


def _fox_prompt(qa, ka, vt, tq):
    b, _, t = vt.shape
    tk = tq
    nq = t // tq
    qi = np.array([i for i in range(nq) for _ in range(i + 1)], np.int32)
    kj = np.array([j for i in range(nq) for j in range(i + 1)], np.int32)
    kern = functools.partial(_fox_prompt_kernel, tq=tq, tk=tk)
    grid_spec = pltpu.PrefetchScalarGridSpec(
        num_scalar_prefetch=2,
        grid=(b, N_PAIRS // PAIRS_PER_STEP, len(qi)),
        in_specs=[
            pl.BlockSpec((1, tq, PAIRS_PER_STEP * QK_W), lambda bi, g, s, qi_, kj_: (bi, qi_[s], g)),
            pl.BlockSpec((1, tk, PAIRS_PER_STEP * QK_W), lambda bi, g, s, qi_, kj_: (bi, kj_[s], g)),
            pl.BlockSpec((1, PAIRS_PER_STEP * PAIR_W, tk), lambda bi, g, s, qi_, kj_: (bi, g, kj_[s])),
        ],
        out_specs=pl.BlockSpec((1, tq, PAIRS_PER_STEP * PAIR_W), lambda bi, g, s, qi_, kj_: (bi, qi_[s], g)),
        scratch_shapes=[pltpu.VMEM((HEADS_PER_STEP, tq, QK_W), BF16), pltpu.VMEM((HEADS_PER_STEP, 1, tq), F32),
                        pltpu.VMEM((HEADS_PER_STEP, 1, tq), F32), pltpu.VMEM((HEADS_PER_STEP, D_HEAD, tq), F32),
                        pltpu.VMEM((HEADS_PER_STEP, tk, tq), F32), pltpu.VMEM((HEADS_PER_STEP, tk, tq), BF16)],
    )
    return pl.pallas_call(
        kern,
        grid_spec=grid_spec,
        out_shape=jax.ShapeDtypeStruct((b, t, W_MIX), BF16),
        compiler_params=_params("arbitrary", "arbitrary", "arbitrary"),
        name="fox_prompt",
    )(jnp.asarray(qi), jnp.asarray(kj), qa, ka, vt)


def _fox_sample_kernel(pt_ref, q_ref, kn_ref, vn_ref, lfn_ref, *rest, n_new, n_steps, n_grp):
    del pt_ref
    kp_refs, vp_refs, lfp_refs = rest[:n_grp], rest[n_grp:2 * n_grp], rest[2 * n_grp:3 * n_grp]
    o_ref, qm, m_s, l_s, acc, carry, cq_s = rest[3 * n_grp:]
    j = pl.program_id(1)
    rows = n_new * N_HEADS
    lane8 = lax.broadcasted_iota(jnp.int32, (N_HEADS, W_MIX), 1)
    head8 = lax.broadcasted_iota(jnp.int32, (N_HEADS, W_MIX), 0)
    head_sel = (lane8 >> 6) == head8
    kr = lax.broadcasted_iota(jnp.int32, (PAGE, PAGE), 0)
    kc = lax.broadcasted_iota(jnp.int32, (PAGE, PAGE), 1)
    tile = lambda x: jnp.concatenate([x] * n_new, axis=0)

    def update(s, pv):
        m_prev = m_s[...]
        m_new = jnp.maximum(m_prev, jnp.max(s, axis=1, keepdims=True))
        alpha = jnp.exp(m_prev - m_new)
        p = jnp.exp(s - m_new)
        l_s[...] = alpha * l_s[...] + jnp.sum(p, axis=1, keepdims=True)
        acc[...] = alpha * acc[...] + pv(p.astype(BF16))
        m_s[...] = m_new

    @pl.when(j == 0)
    def _():
        q = q_ref[0].astype(F32)
        qrows = [jnp.where(head_sel, jnp.broadcast_to(q[t:t + 1, :], (N_HEADS, W_MIX)), 0.0)
                 for t in range(n_new)]
        qm[...] = jnp.concatenate(qrows, axis=0).astype(BF16)
        m_s[...] = jnp.full_like(m_s, NEG)
        l_s[...] = jnp.zeros_like(l_s)
        acc[...] = jnp.zeros_like(acc)
        carry[...] = jnp.zeros_like(carry)
        cn = _dot_sel_rhs(lfn_ref[0], (kr <= kc).astype(BF16))
        for t in range(n_new):
            cq_s[t * N_HEADS:(t + 1) * N_HEADS, :] = cn[:, t:t + 1]
        s = lax.dot_general(qm[...], kn_ref[0], (((1,), (1,)), ((), ())), preferred_element_type=F32)
        s = s + (cq_s[...] - tile(cn))
        tok = lax.broadcasted_iota(jnp.int32, (rows, PAGE), 0) >> 3
        key = lax.broadcasted_iota(jnp.int32, (rows, PAGE), 1)
        s = jnp.where(key <= tok, s, NEG)
        update(s, lambda p: _bdot(p, vn_ref[0]))

    @pl.when(j > 0)
    def _():
        lfs = [ref[0] for ref in lfp_refs]
        suffix = _dot_sel_rhs(jnp.concatenate(lfs, axis=0), (kr > kc).astype(BF16))
        later = carry[...]
        cq = cq_s[...]
        parts = []
        for g in range(n_grp):
            bias8 = suffix[g * N_HEADS:(g + 1) * N_HEADS, :] + later
            s_g = _bdot(qm[...], kp_refs[g][0].astype(BF16))
            parts.append(s_g + (cq + tile(bias8)))
            later = later + jnp.sum(lfs[g], axis=1, keepdims=True)
        carry[...] = later
        s = jnp.concatenate(parts, axis=1)

        def pv(p):
            out = _bdot_nt(p[:, 0:PAGE], vp_refs[0][0].astype(BF16))
            for g in range(1, n_grp):
                out = out + _bdot_nt(p[:, g * PAGE:(g + 1) * PAGE], vp_refs[g][0].astype(BF16))
            return out

        update(s, pv)

    @pl.when(j == n_steps - 1)
    def _():
        o = acc[...] / l_s[...]
        for t in range(n_new):
            ot = jnp.where(head_sel, o[t * N_HEADS:(t + 1) * N_HEADS, :], 0.0)
            o_ref[0, t:t + 1, :] = jnp.sum(ot, axis=0, keepdims=True).astype(o_ref.dtype)


def _fox_sample(page_table, qb, kn, vn, lfn_t, pool_k, pool_v, pool_lf_t, n_new):
    r, n_pages = page_table.shape
    rows = n_new * N_HEADS
    n_grp = max(g for g in range(1, MAX_PAGES_PER_STEP + 1) if n_pages % g == 0)
    n_steps = n_pages // n_grp + 1
    kern = functools.partial(_fox_sample_kernel, n_new=n_new, n_steps=n_steps, n_grp=n_grp)

    def page(g):
        return lambda ri, j, pt: (pt[ri, n_pages - 1 - ((jnp.maximum(j, 1) - 1) * n_grp + g)], 0, 0)

    req = lambda ri, j, pt: (ri, 0, 0)
    grid_spec = pltpu.PrefetchScalarGridSpec(
        num_scalar_prefetch=1,
        grid=(r, n_steps),
        in_specs=[
            pl.BlockSpec((1, n_new, W_MIX), req),
            pl.BlockSpec((1, PAGE, W_MIX), req),
            pl.BlockSpec((1, PAGE, W_MIX), req),
            pl.BlockSpec((1, N_HEADS, PAGE), req),
        ] + [pl.BlockSpec((1, W_MIX, PAGE), page(g)) for g in range(n_grp)]
          + [pl.BlockSpec((1, W_MIX, PAGE), page(g)) for g in range(n_grp)]
          + [pl.BlockSpec((1, N_HEADS, PAGE), page(g)) for g in range(n_grp)],
        out_specs=pl.BlockSpec((1, n_new, W_MIX), req),
        scratch_shapes=[pltpu.VMEM((rows, W_MIX), BF16), pltpu.VMEM((rows, 1), F32),
                        pltpu.VMEM((rows, 1), F32), pltpu.VMEM((rows, W_MIX), F32),
                        pltpu.VMEM((N_HEADS, 1), F32), pltpu.VMEM((rows, 1), F32)],
    )
    return pl.pallas_call(
        kern,
        grid_spec=grid_spec,
        out_shape=jax.ShapeDtypeStruct((r, n_new, W_MIX), F32),
        compiler_params=_params("arbitrary", "arbitrary"),
        name="fox_sample",
    )(page_table, qb, kn, vn, lfn_t, *([pool_k] * n_grp), *([pool_v] * n_grp), *([pool_lf_t] * n_grp))


HEADS_PER_ITER = 8


def _rwkv_kernel(u_ref, sh_ref, s0_ref, mu_ref, w0_ref, w2_ref, a0_ref, a2_ref, g2_ref, kk_ref, ka_ref,
                 rk_ref, lw_ref, lb_ref, e_ref, msk_ref, mskb_ref, o_ref, sout_ref,
                 carry, st, kt_s, rt_s, kh_s, bh_s, k0_s, r0_s, khp_s, bhp_s, v_s, gt_s, o_s,
                 *, tt, t_real, t_total):
    ti = pl.program_id(1)
    nch = tt // CHUNK

    @pl.when(ti == 0)
    def _():
        carry[...] = sh_ref[0]
        st[...] = s0_ref[0]

    u = u_ref[0]
    row1 = lax.broadcasted_iota(jnp.int32, (tt, 1), 0)
    u_prev = jnp.where(row1 == 0, carry[...], pltpu.roll(u, 1, 0))
    carry[...] = u[tt - 1:tt, :]
    us = u + mu_ref[...] * (u_prev - u)
    r = us[:, 0:W_MIX]
    k = us[:, W_MIX:2 * W_MIX]
    v = us[:, 2 * W_MIX:3 * W_MIX]
    wl = us[:, 3 * W_MIX:3 * W_MIX + LANES]
    al = us[:, 3 * W_MIX + LANES:3 * W_MIX + 2 * LANES]
    gl = us[:, 3 * W_MIX + 2 * LANES:3 * W_MIX + 3 * LANES]
    z = w0_ref[...] + _dot(jnp.tanh(wl), w2_ref[...])
    logd = -EXP_NEG_HALF * _sigmoid(z)
    a = _sigmoid(a0_ref[...] + _dot(al, a2_ref[...]))
    g = _dot(_sigmoid(gl), g2_ref[...])
    e_heads = e_ref[...]
    kk = k * kk_ref[...]
    kk = kk * lax.rsqrt(jnp.maximum(_dot(kk * kk, e_heads), 1e-24))
    k_mod = k * (1.0 + (a - 1.0) * ka_ref[...])
    b = kk * a
    bonus = _dot(r * k_mod * rk_ref[...], e_heads) * v
    if t_real < t_total:
        valid = (ti * tt + row1) < t_real
        zero = lambda t: jnp.where(valid, t, 0.0)
        kk, b, k_mod, v, logd = zero(kk), zero(b), zero(k_mod), zero(v), zero(logd)

    cum = _dot_sel_lhs(msk_ref[1].astype(BF16), logd)
    if nch == 1:
        ref = _dot_sel_lhs(msk_ref[3].astype(BF16), logd)
        tot = _dot_sel_lhs(msk_ref[4].astype(BF16), logd)
    else:
        on_rows = lambda r: jnp.concatenate(
            [jnp.broadcast_to(cum[c * CHUNK + r:c * CHUNK + r + 1, :], (CHUNK, W_MIX)) for c in range(nch)], axis=0)
        ref = on_rows(CHUNK // 2 - 1)
        tot = on_rows(CHUNK - 1)
    cum_prev = cum - logd
    e_inv = jnp.exp(ref - cum)
    e_tot = jnp.exp(tot - cum)
    wide = dict(kt=kk * jnp.exp(cum_prev - ref), rt=r * jnp.exp(cum - ref), kh=k_mod * e_inv, bh=b * e_inv,
                k0=kk * jnp.exp(cum_prev), r0=r * jnp.exp(cum), khp=k_mod * e_tot, bhp=b * e_tot, v=v,
                gt=jnp.exp(tot))
    dst = dict(kt=kt_s, rt=rt_s, kh=kh_s, bh=bh_s, k0=k0_s, r0=r0_s, khp=khp_s, bhp=bhp_s, v=v_s, gt=gt_s)
    for name, val in wide.items():
        val = val.astype(dst[name].dtype)
        for h in range(N_HEADS):
            dst[name][h] = val[:, h * D_HEAD:(h + 1) * D_HEAD]

    def head_body(p, _):
        hs = [p * HEADS_PER_ITER + i for i in range(HEADS_PER_ITER)]
        ld = lambda s: [s[h] for h in hs]
        ldb = lambda s: [s[h].astype(BF16) for h in hs]
        kt, rt, kh, bh = ldb(kt_s), ldb(rt_s), ldb(kh_s), ldb(bh_s)
        strict = msk_ref[0] != 0.0
        incl = msk_ref[1] != 0.0
        a_kk = _each(lambda x, y: jnp.where(strict, _bdot_nt(x, y), 0.0).astype(BF16), kt, kh)
        a_kb = _each(lambda x, y: jnp.where(strict, _bdot_nt(x, y), 0.0).astype(BF16), kt, bh)
        a_rk = _each(lambda x, y: jnp.where(incl, _bdot_nt(x, y), 0.0).astype(BF16), rt, kh)
        a_rb = _each(lambda x, y: jnp.where(incl, _bdot_nt(x, y), 0.0).astype(BF16), rt, bh)
        x = _each(lambda t: -(t * mskb_ref[0]), a_kb)
        x2 = _each(lambda t: _bdot(t, t).astype(BF16), x)
        pm = _each(lambda t: msk_ref[2] + t.astype(F32), x)
        pm = _each(lambda q, t: q + _bdot(q.astype(BF16), t), pm, x2)
        x4 = _each(lambda t: _bdot(t, t).astype(BF16), x2)
        pm = _each(lambda q, t: (q + _bdot(q.astype(BF16), t)).astype(BF16), pm, x4)
        for lvl in (1, 2, 3):
            a_off = _each(lambda t: t * mskb_ref[lvl], a_kb)
            pa = _each(lambda q, t: _bdot(q, t).astype(BF16), pm, a_off)
            pm = _each(lambda q, s: q - _bdot(s, q).astype(BF16), pm, pa)
        t_inv = pm
        vb = ldb(v_s)
        w1 = _each(lambda t, y: _bdot(t, y), t_inv, ldb(k0_s))
        av = _each(lambda t, y: _bdot(t, y).astype(BF16), a_kk, vb)
        u1 = _each(lambda t, y: _bdot(t, y), t_inv, av)
        w1b = _each(lambda t: t.astype(BF16), w1)
        u1b = _each(lambda t: t.astype(BF16), u1)
        q_t = _each(lambda r0, t, y: (r0 - _bdot(t, y)).astype(BF16), ld(r0_s), a_rb, w1b)
        o1 = _each(lambda t, y, t2, y2: _bdot(t, y) - _bdot(t2, y2), a_rk, vb, a_rb, u1b)
        khp, bhp, gt = ld(khp_s), ld(bhp_s), ld(gt_s)
        for c in range(nch):
            rows = slice(c * CHUNK, (c + 1) * CHUNK)
            bt = _each(lambda t: t[rows].T.astype(BF16), bhp)
            kt_ = _each(lambda t: t[rows].T.astype(BF16), khp)
            gcol = _each(lambda t: t[rows].T[:, 0:1], gt)
            m_low = _each(lambda t, y: _bdot(t, y[rows]).astype(BF16), bt, w1b)
            n1 = _each(lambda t, y, t2, y2: _bdot(t, y[rows]) - _bdot(t2, y2[rows]), kt_, vb, bt, u1b)
            for i, h in enumerate(hs):
                s_t = st[h]
                s_hi = s_t.astype(BF16)
                s_lo = (s_t - s_hi.astype(F32)).astype(BF16)
                qc = q_t[i][rows]
                o_s[h, rows, :] = _bdot(qc, s_hi) + _bdot(qc, s_lo) + o1[i][rows]
                st[h] = gcol[i] * s_t - (_bdot(m_low[i], s_hi) + _bdot(m_low[i], s_lo)) + n1[i]
        return 0

    lax.fori_loop(0, N_HEADS // HEADS_PER_ITER, head_body, 0)

    o = jnp.concatenate([o_s[h] for h in range(N_HEADS)], axis=1)
    mean = _dot(o, e_heads) * (1.0 / D_HEAD)
    xc = o - mean
    var = _dot(xc * xc, e_heads) * (1.0 / D_HEAD)
    on = xc * lax.rsqrt(var + GN_EPS) * lw_ref[...] + lb_ref[...]
    o_ref[0] = ((on + bonus) * g).astype(o_ref.dtype)
    sout_ref[0] = st[...]


def _chunk_masks(tt):
    i = jnp.arange(tt)[:, None]
    j = jnp.arange(tt)[None, :]
    blk = lambda s: (i >> s) == (j >> s)
    same = blk(6)
    ms = [same & (j < i), same & (j <= i), i == j, same & ((j & (CHUNK - 1)) < CHUNK // 2), same]
    mb = [blk(3), blk(4) & ~blk(3), blk(5) & ~blk(4), same & ~blk(5)]
    return jnp.stack(ms).astype(F32), jnp.stack(mb).astype(BF16)


def _rwkv(u3, shift0, s0_t, vecs, w2p, a2p, g2, e_heads, tt, t_real):
    b, t, _ = u3.shape
    mu, w0, a0, k_k, k_a, r_k, lnw, lnb = vecs
    masks, masks_b = _chunk_masks(tt)
    kern = functools.partial(_rwkv_kernel, tt=tt, t_real=t_real, t_total=t)
    vec = lambda c: pl.BlockSpec((1, c), lambda bi, ti: (0, 0))
    mat = lambda r_, c: pl.BlockSpec((r_, c), lambda bi, ti: (0, 0))
    wide = lambda dt: pltpu.VMEM((N_HEADS, tt, D_HEAD), dt)
    return pl.pallas_call(
        kern,
        grid=(b, t // tt),
        in_specs=[
            pl.BlockSpec((1, tt, RW_PAD), lambda bi, ti: (bi, ti, 0)),
            pl.BlockSpec((1, 1, RW_PAD), lambda bi, ti: (bi, 0, 0)),
            pl.BlockSpec((1, N_HEADS, D_HEAD, D_HEAD), lambda bi, ti: (bi, 0, 0, 0)),
            vec(RW_PAD), vec(W_MIX), mat(LANES, W_MIX), vec(W_MIX), mat(LANES, W_MIX), mat(LANES, W_MIX),
            vec(W_MIX), vec(W_MIX), vec(W_MIX), vec(W_MIX), vec(W_MIX), mat(W_MIX, W_MIX),
            pl.BlockSpec(masks.shape, lambda bi, ti: (0, 0, 0)),
            pl.BlockSpec(masks_b.shape, lambda bi, ti: (0, 0, 0)),
        ],
        out_specs=[pl.BlockSpec((1, tt, W_MIX), lambda bi, ti: (bi, ti, 0)),
                   pl.BlockSpec((1, N_HEADS, D_HEAD, D_HEAD), lambda bi, ti: (bi, 0, 0, 0))],
        out_shape=[jax.ShapeDtypeStruct((b, t, W_MIX), BF16),
                   jax.ShapeDtypeStruct((b, N_HEADS, D_HEAD, D_HEAD), F32)],
        scratch_shapes=[pltpu.VMEM((1, RW_PAD), F32), pltpu.VMEM((N_HEADS, D_HEAD, D_HEAD), F32)]
                       + [wide(BF16 if name in ("kt", "rt", "kh", "bh", "k0", "v") else F32)
                          for name in ("kt", "rt", "kh", "bh", "k0", "r0", "khp", "bhp", "v", "gt", "o")],
        compiler_params=_params("arbitrary", "arbitrary"),
        name="rwkv7_chunked",
    )(u3, shift0, s0_t, mu, w0, w2p, a0, a2p, g2, k_k, k_a, r_k, lnw, lnb, e_heads, masks, masks_b)


def _mix_route_kernel(x_ref, oa_ref, ob_ref, gm_ref, wg_ref, pa_ref, pb_ref, wo_ref, gf_ref, wr_ref,
                      x1_ref, hn_ref, rw_ref):
    x = x_ref[...]
    xb = _rms(x, gm_ref[...]).astype(BF16)
    gate = lambda lo: _sigmoid(_bdot(xb, wg_ref[:, lo:lo + D_MODEL]))
    mixed = gate(0) * _bdot(oa_ref[...], pa_ref[...])
    mixed = mixed + gate(D_MODEL) * _bdot(ob_ref[...], pb_ref[...])
    x1 = x + _dot(mixed, wo_ref[...])
    x1_ref[...] = x1
    hb = _rms(x1, gf_ref[...]).astype(BF16)
    hn_ref[...] = hb
    logits = jnp.dot(hb, wr_ref[...], preferred_element_type=F32)
    tm = logits.shape[0]
    lane = lax.broadcasted_iota(jnp.int32, (tm, LANES), 1)
    lane_f = lane.astype(F32)
    first = lambda hit: jnp.min(jnp.where(hit, lane_f, 1e9), axis=1, keepdims=True)
    is_grp = (lane >= N_EXPERTS) & (lane < N_EXPERTS + N_GROUPS)
    gl = jnp.where(is_grp, logits, NEG)
    gmax = jnp.max(gl, axis=1, keepdims=True)
    g_idx = first(gl == gmax) - float(N_EXPERTS)
    p_grp = 1.0 / jnp.sum(jnp.where(is_grp, jnp.exp(gl - gmax), 0.0), axis=1, keepdims=True)
    in_grp = (lane < N_EXPERTS) & ((lane >> 3).astype(F32) == g_idx)
    el = jnp.where(in_grp, logits, NEG)
    v1 = jnp.max(el, axis=1, keepdims=True)
    i1 = first(el == v1)
    el2 = jnp.where(lane_f == i1, NEG, el)
    v2 = jnp.max(el2, axis=1, keepdims=True)
    i2 = first(el2 == v2)
    e2 = jnp.exp(v2 - v1)
    w1 = p_grp / (1.0 + e2)
    rw_ref[...] = jnp.where(lane_f == i1, w1, 0.0) + jnp.where(lane_f == i2, w1 * e2, 0.0)


def _mix_route(x2, oa, ob, gm, wg, pa, pb, wo, gf, wr, tm):
    n = x2.shape[0]
    row = lambda c: pl.BlockSpec((tm, c), lambda i: (i, 0))
    full = lambda r, c: pl.BlockSpec((r, c), lambda i: (0, 0))
    return pl.pallas_call(
        _mix_route_kernel,
        grid=(n // tm,),
        in_specs=[row(D_MODEL), row(W_MIX), row(W_MIX), full(1, D_MODEL), full(D_MODEL, 2 * D_MODEL),
                  full(W_MIX, D_MODEL), full(W_MIX, D_MODEL), full(D_MODEL, D_MODEL), full(1, D_MODEL),
                  full(D_MODEL, LANES)],
        out_specs=[row(D_MODEL), row(D_MODEL), row(LANES)],
        out_shape=[jax.ShapeDtypeStruct((n, D_MODEL), F32), jax.ShapeDtypeStruct((n, D_MODEL), BF16),
                   jax.ShapeDtypeStruct((n, LANES), F32)],
        compiler_params=_params("arbitrary"),
        name="mix_route",
    )(x2, oa, ob, gm, wg, pa, pb, wo, gf, wr)


MOE_CHUNK = 256
MOE_TAIL_CHUNK = 128
MOE_VMEM_LIMIT = 60 * 1024 * 1024


def _moe_kernel(cnt_ref, hn_ref, rw_ref, wgu_ref, wd_ref, tri_ref, x1_ref, gn_ref, out_ref, acc, *, tile, chunks):
    i = pl.program_id(0)
    g = pl.program_id(1)

    @pl.when(g == 0)
    def _():
        acc[...] = jnp.zeros_like(acc)

    rw = rw_ref[...]
    lane = lax.broadcasted_iota(jnp.int32, (tile, LANES), 1)
    in_g = (lane >> 3) == g
    member = jnp.sum(jnp.where(in_g, rw, 0.0), axis=1, keepdims=True) > 0.0
    ind = jnp.broadcast_to(jnp.where(member, 1.0, 0.0), (tile, LANES))
    rank = _bdot(tri_ref[...], ind.astype(BF16))
    ind_row = ind.T[0:1, :]
    rank_row = rank.T[0:1, :]
    count = cnt_ref[i * N_GROUPS + g]
    pieces = _split3(rw)[:2]

    for start, chunk in chunks:
        @pl.when(start < count)
        def _():
            lane_c = lax.broadcasted_iota(jnp.int32, (chunk, LANES), 1)
            slot_r = (lax.broadcasted_iota(jnp.int32, (chunk, tile), 0) + start).astype(F32)
            slot_c = (lax.broadcasted_iota(jnp.int32, (tile, chunk), 1) + start).astype(F32)
            gather = jnp.where((rank_row == slot_r) & (ind_row > 0.0), 1.0, 0.0).astype(BF16)
            scatter = jnp.where((rank[:, 0:1] == slot_c) & member, 1.0, 0.0).astype(BF16)
            xg = _bdot(gather, hn_ref[...]).astype(BF16)
            wg = sum(_bdot(gather, pc) for pc in pieces)
            acts = []
            for e in range(EXPERTS_PER_GROUP):
                h = _bdot(xg, wgu_ref[0, e])
                gate = h[:, :D_EXPERT]
                w_e = jnp.sum(jnp.where(lane_c == g * EXPERTS_PER_GROUP + e, wg, 0.0), axis=1, keepdims=True)
                acts.append((gate * _sigmoid(gate) * h[:, D_EXPERT:] * w_e).astype(BF16))
            yg = _bdot(jnp.concatenate(acts, axis=1), wd_ref[0])
            acc[...] += _bdot(scatter, yg.astype(BF16))

    @pl.when(g == N_GROUPS - 1)
    def _():
        out_ref[...] = _rms(x1_ref[...] + acc[...], gn_ref[...])


def _moe(hn, rw, wgu, wd, x1, gn, tile):
    n = hn.shape[0]
    chunks = [(0, min(MOE_CHUNK, tile))]
    while sum(chunks[-1]) < tile:
        chunks.append((sum(chunks[-1]), MOE_TAIL_CHUNK))
    n_tiles = n // tile
    grp_w = rw[:, :N_EXPERTS].reshape(n_tiles, tile, N_GROUPS, EXPERTS_PER_GROUP).sum(-1)
    counts = (grp_w > 0.0).sum(1).astype(jnp.int32).reshape(-1)
    t = jnp.arange(tile)
    tri = (t[None, :] < t[:, None]).astype(BF16)
    kern = functools.partial(_moe_kernel, tile=tile, chunks=tuple(chunks))
    row = lambda c: pl.BlockSpec((tile, c), lambda i, g, cnt: (i, 0))
    grid_spec = pltpu.PrefetchScalarGridSpec(
        num_scalar_prefetch=1,
        grid=(n_tiles, N_GROUPS),
        in_specs=[row(D_MODEL), row(LANES),
                  pl.BlockSpec((1, EXPERTS_PER_GROUP, D_MODEL, 2 * D_EXPERT), lambda i, g, cnt: (g, 0, 0, 0)),
                  pl.BlockSpec((1, EXPERTS_PER_GROUP * D_EXPERT, D_MODEL), lambda i, g, cnt: (g, 0, 0)),
                  pl.BlockSpec((tile, tile), lambda i, g, cnt: (0, 0)),
                  row(D_MODEL), pl.BlockSpec((1, D_MODEL), lambda i, g, cnt: (0, 0))],
        out_specs=row(D_MODEL),
        scratch_shapes=[pltpu.VMEM((tile, D_MODEL), F32)],
    )
    return pl.pallas_call(
        kern,
        grid_spec=grid_spec,
        out_shape=jax.ShapeDtypeStruct((n, D_MODEL), F32),
        compiler_params=pltpu.CompilerParams(dimension_semantics=("arbitrary", "arbitrary"),
                                             vmem_limit_bytes=MOE_VMEM_LIMIT),
        name="moe_grouped",
    )(counts, hn, rw, wgu, wd, tri, x1, gn)


def _pad_rw_cols(m):
    z = jnp.zeros(m.shape[:-1] + (LANES - LORA_W,), m.dtype)
    c = 3 * W_MIX
    return jnp.concatenate([m[..., :c], m[..., c:c + LORA_W], z, m[..., c + LORA_W:c + 2 * LORA_W], z,
                            m[..., c + 2 * LORA_W:]], axis=-1)


def _unpad_rw_cols(m):
    c = 3 * W_MIX
    return jnp.concatenate([m[..., :c], m[..., c:c + LORA_W], m[..., c + LANES:c + LANES + LORA_W],
                            m[..., c + 2 * LANES:]], axis=-1)


def _pad_rows(m, rows):
    return jnp.concatenate([m, jnp.zeros((rows - m.shape[0],) + m.shape[1:], m.dtype)], axis=0)


def _prep_weights(l, norm_mix, w_in, b_f, mu_rw, w0, w2, a0, a2, g2, k_k, k_a, r_k, lnx_w, lnx_b,
                  p_a, p_b, w_o, norm_ffn, w_grp, w_exp, we_gate, we_up, we_down):
    w = w_in[l]
    w_fl = jnp.concatenate([w[:, 3 * W_MIX:FOX_COLS], jnp.zeros((D_MODEL, LANES - N_HEADS), F32)], axis=1)
    w_rw = _pad_rw_cols(w[:, FOX_COLS:FOX_COLS + RW_COLS])
    row = lambda vct: vct.reshape(1, -1)
    head_id = jnp.arange(W_MIX) // D_HEAD
    return dict(
        gm=row(norm_mix[l]),
        w_a=jnp.concatenate([w[:, :3 * W_MIX], w_fl, w_rw], axis=1).astype(BF16),
        w_kvt=w[:, W_MIX:3 * W_MIX].T.astype(BF16),
        bf=jnp.concatenate([b_f[l], jnp.zeros((LANES - N_HEADS,), F32)]).reshape(1, LANES),
        wg=w[:, FOX_COLS + RW_COLS:].astype(BF16),
        vecs=(row(_pad_rw_cols(mu_rw[l])), row(w0[l]), row(a0[l]), row(k_k[l]), row(k_a[l]),
              row(r_k[l].reshape(-1)), row(lnx_w[l]), row(lnx_b[l])),
        w2p=_pad_rows(w2[l], LANES).astype(BF16),
        a2p=_pad_rows(a2[l], LANES).astype(BF16),
        g2=g2[l].astype(BF16),
        e_heads=(head_id[:, None] == head_id[None, :]).astype(BF16),
        pa=p_a[l].astype(BF16), pb=p_b[l].astype(BF16), wo=w_o[l].astype(BF16),
        gf=row(norm_ffn[l]),
        wr=jnp.concatenate([w_exp[l], w_grp[l], jnp.zeros((D_MODEL, LANES - N_EXPERTS - N_GROUPS), F32)],
                           axis=1).astype(BF16),
        wgu=jnp.concatenate([we_gate[l], we_up[l]], axis=-1).astype(BF16).reshape(
            N_GROUPS, EXPERTS_PER_GROUP, D_MODEL, 2 * D_EXPERT),
        wd=we_down[l].astype(BF16).reshape(N_GROUPS, EXPERTS_PER_GROUP * D_EXPERT, D_MODEL),
    )


def _layer(x, past, shift0, wkv0, page_table, p, gn, tiles):
    b, t, _ = x.shape
    n = b * t
    tm, tq, tt, tm_moe = tiles
    x2 = x.reshape(n, D_MODEL)
    to3 = lambda m: m.reshape(b, t, m.shape[-1])
    if past is None:
        qb, kb, lf, rw, k_t, v_t, vt = _inproj(x2, p["gm"], p["w_a"], p["w_kvt"], p["bf"], tm,
                                               QK_SCALE * LOG2E, seq=t)
        qa, ka = _qk_aug(to3(lf), to3(qb), to3(kb), tm)
        o_a = _fox_prompt(qa, ka, vt, tq)
        t_pad = t
        u3 = to3(rw)
        k, v = (m.reshape(b, N_HEADS, D_HEAD, t).transpose(0, 3, 1, 2) for m in (k_t, v_t))
    else:
        qb, kb, lf, rw, k, v, vb = _inproj(x2, p["gm"], p["w_a"], p["w_kvt"], p["bf"], tm, QK_SCALE)
        pool_k, pool_v, pool_lf_t = past
        pad_keys = lambda m: jnp.concatenate([to3(m), jnp.zeros((b, PAGE - t, W_MIX), BF16)], axis=1)
        lfn_t = jnp.concatenate([to3(lf)[:, :, :N_HEADS].transpose(0, 2, 1),
                                 jnp.zeros((b, N_HEADS, PAGE - t), F32)], axis=2)
        o_a = _fox_sample(page_table, to3(qb).astype(F32), pad_keys(kb), pad_keys(vb), lfn_t,
                          pool_k, pool_v, pool_lf_t, t).astype(BF16)
        t_pad = tt
        u3 = jnp.concatenate([to3(rw), jnp.zeros((b, t_pad - t, RW_PAD), F32)], axis=1)
    o_b, wkv_t = _rwkv(u3, shift0, jnp.swapaxes(wkv0, -1, -2), p["vecs"], p["w2p"], p["a2p"], p["g2"],
                       p["e_heads"], tt, t)
    o_b = o_b[:, :t].reshape(n, W_MIX)
    wkv_new = jnp.swapaxes(wkv_t, -1, -2)
    x1, hn, route = _mix_route(x2, o_a.reshape(n, W_MIX), o_b, p["gm"], p["wg"], p["pa"], p["pb"], p["wo"],
                               p["gf"], p["wr"], tm_moe)
    y = _moe(hn, route, p["wgu"], p["wd"], x1, gn, tm_moe)
    shift_new = _unpad_rw_cols(to3(rw)[:, t - 1, :])
    return (y.reshape(b, t, D_MODEL), k.reshape(b, t, N_HEADS, D_HEAD), v.reshape(b, t, N_HEADS, D_HEAD),
            to3(lf)[:, :, :N_HEADS], wkv_new, shift_new)


def kernel(x_prompt, x_sample, cache_k, cache_v, cache_logf, page_table, state_wkv, state_shift, norm_mix, w_in, b_f, mu_rw, w0, w2, a0, a2, g2, k_k, k_a, r_k, lnx_w, lnx_b, p_a, p_b, w_o, norm_ffn, w_grp, w_exp, we_gate, we_up, we_down, norm_final):
    depth = w_in.shape[0]
    assert depth == 1, "final norm is fused into the layer's last kernel"
    bp, tp, _ = x_prompt.shape
    bs, ts, _ = x_sample.shape
    n_phys = cache_k.shape[1]
    gn = norm_final.reshape(1, D_MODEL)
    l = 0
    p = _prep_weights(l, norm_mix, w_in, b_f, mu_rw, w0, w2, a0, a2, g2, k_k, k_a, r_k, lnx_w, lnx_b,
                      p_a, p_b, w_o, norm_ffn, w_grp, w_exp, we_gate, we_up, we_down)
    tq = min(1024, tp)
    prompt = _layer(x_prompt, None, jnp.zeros((bp, 1, RW_PAD), F32),
                    jnp.zeros((bp, N_HEADS, D_HEAD, D_HEAD), F32), None, p, gn,
                    (min(512, bp * tp), tq, min(256, tp), min(1024, bp * tp)))
    past = (cache_k[l].transpose(0, 2, 3, 1).reshape(n_phys, W_MIX, PAGE),
            cache_v[l].transpose(0, 2, 3, 1).reshape(n_phys, W_MIX, PAGE),
            cache_logf[l].transpose(0, 2, 1))
    sample = _layer(x_sample, past, _pad_rw_cols(state_shift[l])[:, None, :], state_wkv[l], page_table, p, gn,
                    (bs * ts, None, CHUNK, bs * ts))
    outs = []
    for y, k, v, lf, wkv, sh in (prompt, sample):
        outs.append((y, k[None], v[None], lf[None], wkv[None], sh[None]))
    (yp, kp, vp, lp, wp, sp), (ys, ks, vs, ls, ws, ss) = outs
    return (yp, ys, kp, vp, lp, wp, sp, ks, vs, ls, ws, ss)
```
